```python
import jax, jax.numpy as jnp
from jax import lax
import numpy as np

D_MODEL = 2048
BATCH = 8
SEQ = 2048
DEPTH = 2

GRID_W = 64
CTX_LEN = 256
HEAD_DIM = 128
N_HEAD_SLOTS = D_MODEL // HEAD_DIM
FOURIER_GROUPS = N_HEAD_SLOTS // 4
ATTN_Q_HEADS = N_HEAD_SLOTS - FOURIER_GROUPS
ATTN_KV_HEADS = ATTN_Q_HEADS // 3
GQA_GROUP = ATTN_Q_HEADS // ATTN_KV_HEADS
WINDOW = 128
BLOCK = 128
ROPE_BASE = 10000.0
Q_END = ATTN_Q_HEADS * HEAD_DIM
KV_W = ATTN_KV_HEADS * HEAD_DIM
K_END = Q_END + KV_W
V_END = K_END + KV_W
AB_IN = V_END + FOURIER_GROUPS * HEAD_DIM
AB_MIX = Q_END + FOURIER_GROUPS * HEAD_DIM
SG_GROUPS = N_HEAD_SLOTS // 2
SG_WIDTH = SG_GROUPS * HEAD_DIM
CHUNK = 128
CONV_CH = D_MODEL - SG_WIDTH
CONV_WIDTH = 31
CD_IN = 2 * SG_WIDTH + 2 * CONV_CH
CD_MIX = SG_WIDTH + CONV_CH
N_EXPERTS = 16
CAPACITY_FACTOR = 2
EXPERT_FF = 2048
ALPHA = (2 * DEPTH) ** 0.25
BETA = (8 * DEPTH) ** -0.25
LN_EPS = 1e-6
N_EVEN = (DEPTH + 1) // 2
N_ODD = DEPTH // 2

kernel_name = "hybrid_diffusion_swa_fnet_gmlp_conformer_ecmoe"


def layer_norm(x, g=None, b=None):
    xf = x.astype(jnp.float32)
    mu = jnp.mean(xf, -1, keepdims=True)
    var = jnp.mean(jnp.square(xf - mu), -1, keepdims=True)
    y = (xf - mu) * lax.rsqrt(var + LN_EPS)
    if g is not None:
        y = y * g.astype(jnp.float32) + b.astype(jnp.float32)
    return y.astype(x.dtype)


def modulate(x, shift, scale):
    return layer_norm(x) * (1 + scale) + shift


def axial_rope_angles(n):
    rows = n // GRID_W
    r, col = jnp.meshgrid(jnp.arange(rows), jnp.arange(GRID_W), indexing="ij")
    r = r.reshape(n).astype(jnp.float32)
    col = col.reshape(n).astype(jnp.float32)
    quarter = HEAD_DIM // 4
    inv = ROPE_BASE ** (-jnp.arange(quarter, dtype=jnp.float32) / quarter)
    return r[:, None] * inv, col[:, None] * inv


def _rotate(x, ang):
    cos = jnp.cos(ang)[None, :, None, :].astype(x.dtype)
    sin = jnp.sin(ang)[None, :, None, :].astype(x.dtype)
    x1, x2 = jnp.split(x, 2, axis=-1)
    return jnp.concatenate([x1 * cos - x2 * sin, x1 * sin + x2 * cos], -1)


def apply_axial_rope(x, ang_r, ang_c):
    half = HEAD_DIM // 2
    return jnp.concatenate([_rotate(x[..., :half], ang_r), _rotate(x[..., half:], ang_c)], -1)


def banded_gqa_with_context(q, k, v, kc, vc, sink):
    b, n, hq, d = q.shape
    lc = kc.shape[1]
    nb = n // BLOCK
    kw = BLOCK + 2 * WINDOW
    scale = HEAD_DIM ** -0.5
    qb = q.reshape(b, nb, BLOCK, ATTN_KV_HEADS, GQA_GROUP, d)
    span = jnp.arange(nb)[:, None] * BLOCK + jnp.arange(kw)[None, :]
    pad = ((0, 0), (WINDOW, WINDOW), (0, 0), (0, 0))
    kb = jnp.pad(k, pad)[:, span]
    vb = jnp.pad(v, pad)[:, span]
    key_pos = span - WINDOW
    rel = (jnp.arange(BLOCK)[:, None] + WINDOW) - jnp.arange(kw)[None, :]
    mask = (jnp.abs(rel) <= WINDOW)[None] & ((key_pos >= 0) & (key_pos < n))[:, None, :]
    s_win = jnp.einsum("bnqhgd,bnkhd->bhgnqk", qb, kb).astype(jnp.float32) * scale
    s_win = jnp.where(mask, s_win, -jnp.inf)
    s_ctx = jnp.einsum("bnqhgd,bchd->bhgnqc", qb, kc).astype(jnp.float32) * scale
    s_sink = jnp.broadcast_to(sink.astype(jnp.float32).reshape(1, ATTN_KV_HEADS, GQA_GROUP, 1, 1, 1),
                              s_win.shape[:-1] + (1,))
    p = jax.nn.softmax(jnp.concatenate([s_win, s_ctx, s_sink], -1), axis=-1).astype(v.dtype)
    o = (jnp.einsum("bhgnqk,bnkhd->bnqhgd", p[..., :kw], vb)
         + jnp.einsum("bhgnqc,bchd->bnqhgd", p[..., kw:kw + lc], vc))
    return o.reshape(b, n, hq * d)


def context_gqa(qc, kc, vc, sink):
    b, lc, hq, d = qc.shape
    qg = qc.reshape(b, lc, ATTN_KV_HEADS, GQA_GROUP, d)
    s = jnp.einsum("bqhgd,bkhd->bhgqk", qg, kc).astype(jnp.float32) * HEAD_DIM ** -0.5
    s_sink = jnp.broadcast_to(sink.astype(jnp.float32).reshape(1, ATTN_KV_HEADS, GQA_GROUP, 1, 1),
                              s.shape[:-1] + (1,))
    p = jax.nn.softmax(jnp.concatenate([s, s_sink], -1), axis=-1)[..., :lc].astype(vc.dtype)
    return jnp.einsum("bhgqk,bkhd->bqhgd", p, vc).reshape(b, lc, hq * d)


def fourier_mix(z):
    b, n, _ = z.shape
    zf = z.astype(jnp.float32).reshape(b, n, FOURIER_GROUPS, HEAD_DIM)
    f = jnp.fft.fft2(zf, axes=(1, 3), norm="ortho").real
    return f.reshape(b, n, FOURIER_GROUPS * HEAD_DIM).astype(z.dtype)


def mixer_ab(h_lat, h_ctx, w_in, w_out, sink, ang_r, ang_c, ctx_out):
    b, n, _ = h_lat.shape
    lc = h_ctx.shape[1]
    p = h_lat @ w_in
    q = apply_axial_rope(p[..., :Q_END].reshape(b, n, ATTN_Q_HEADS, HEAD_DIM), ang_r, ang_c)
    k = apply_axial_rope(p[..., Q_END:K_END].reshape(b, n, ATTN_KV_HEADS, HEAD_DIM), ang_r, ang_c)
    v = p[..., K_END:V_END].reshape(b, n, ATTN_KV_HEADS, HEAD_DIM)
    pkv = h_ctx @ w_in[:, Q_END:V_END]
    kc = pkv[..., :KV_W].reshape(b, lc, ATTN_KV_HEADS, HEAD_DIM)
    vc = pkv[..., KV_W:].reshape(b, lc, ATTN_KV_HEADS, HEAD_DIM)
    attn = banded_gqa_with_context(q, k, v, kc, vc, sink)
    y_lat = jnp.concatenate([attn, fourier_mix(p[..., V_END:])], -1) @ w_out
    if not ctx_out:
        return y_lat, None
    qc = (h_ctx @ w_in[:, :Q_END]).reshape(b, lc, ATTN_Q_HEADS, HEAD_DIM)
    fc = fourier_mix(h_ctx @ w_in[:, V_END:])
    y_ctx = jnp.concatenate([context_gqa(qc, kc, vc, sink), fc], -1) @ w_out
    return y_lat, y_ctx


def mixer_cd(h, w_in, w_out, sg_ln_g, sg_ln_b, sg_w, sg_b, conv_w, conv_b, conv_ln_g, conv_ln_b):
    b, n, _ = h.shape
    p = h @ w_in
    z = jax.nn.gelu(p[..., :2 * SG_WIDTH])
    u = z[..., :SG_WIDTH].reshape(b, n // CHUNK, CHUNK, SG_GROUPS, HEAD_DIM)
    vg = layer_norm(z[..., SG_WIDTH:].reshape(b, n, SG_GROUPS, HEAD_DIM), sg_ln_g, sg_ln_b)
    vg = vg.reshape(b, n // CHUNK, CHUNK, SG_GROUPS, HEAD_DIM)
    spatial = jnp.einsum("gpq,bnqgc->bnpgc", sg_w, vg) + sg_b.T[None, None, :, :, None]
    y_sg = (u * spatial).reshape(b, n, SG_WIDTH)
    a, gt = jnp.split(p[..., 2 * SG_WIDTH:], 2, axis=-1)
    xg = a * jax.nn.sigmoid(gt)
    pad = CONV_WIDTH // 2
    xc = lax.conv_general_dilated(xg, conv_w, window_strides=(1,), padding=[(pad, pad)],
                                  dimension_numbers=("NWC", "WIO", "NWC"),
                                  feature_group_count=CONV_CH) + conv_b
    y_cv = jax.nn.silu(layer_norm(xc, conv_ln_g, conv_ln_b))
    return jnp.concatenate([y_sg, y_cv], -1) @ w_out


def expert_choice_ffn(h, w_router, w_gate, w_up, w_down):
    b, n, _ = h.shape
    cap = CAPACITY_FACTOR * n // N_EXPERTS
    aff = jax.nn.softmax(jnp.einsum("bnd,de->bne", h, w_router).astype(jnp.float32), axis=-1)
    g, idx = lax.top_k(jnp.swapaxes(aff, 1, 2), cap)
    bidx = jnp.arange(b)[:, None, None]
    xs = h[bidx, idx]
    hid = (jax.nn.silu(jnp.einsum("becd,edf->becf", xs, w_gate))
           * jnp.einsum("becd,edf->becf", xs, w_up))
    ye = jnp.einsum("becf,efd->becd", hid, w_down) * g[..., None].astype(h.dtype)
    return jnp.zeros_like(h).at[bidx, idx].add(ye)


def context_read_after(l):
    return any(j % 2 == 0 for j in range(l + 1, DEPTH))


def setup_inputs(seed: int = 0) -> dict:
    key = jax.random.key(seed)
    ks = iter(jax.random.split(key, 32))

    def nrm(shape, scale):
        return jax.random.normal(next(ks), shape, jnp.float32) * scale

    D = D_MODEL
    return {
        "x": nrm((BATCH, SEQ, D), 1.0),
        "c": nrm((BATCH, D), 1.0),
        "ctx": nrm((BATCH, CTX_LEN, D), 1.0),
        "c_ctx": nrm((D,), 1.0),
        "w_mod": nrm((DEPTH, D, 6 * D), 0.5 * D ** -0.5),
        "b_mod": nrm((DEPTH, 6 * D), 0.02),
        "ln1_g": 1.0 + nrm((DEPTH, D), 0.02),
        "ln1_b": nrm((DEPTH, D), 0.02),
        "ln2_g": 1.0 + nrm((DEPTH, D), 0.02),
        "ln2_b": nrm((DEPTH, D), 0.02),
        "w_router": nrm((DEPTH, D, N_EXPERTS), D ** -0.5),
        "w_gate": nrm((DEPTH, N_EXPERTS, D, EXPERT_FF), D ** -0.5),
        "w_up": nrm((DEPTH, N_EXPERTS, D, EXPERT_FF), D ** -0.5),
        "w_down": nrm((DEPTH, N_EXPERTS, EXPERT_FF, D), BETA * EXPERT_FF ** -0.5),
        "ab_w_in": nrm((N_EVEN, D, AB_IN), D ** -0.5),
        "ab_w_out": nrm((N_EVEN, AB_MIX, D), BETA * AB_MIX ** -0.5),
        "sink": nrm((N_EVEN, ATTN_Q_HEADS), 1.0),
        "cd_w_in": nrm((N_ODD, D, CD_IN), D ** -0.5),
        "cd_w_out": nrm((N_ODD, CD_MIX, D), BETA * CD_MIX ** -0.5),
        "sg_ln_g": 1.0 + nrm((N_ODD, SG_GROUPS, HEAD_DIM), 0.02),
        "sg_ln_b": nrm((N_ODD, SG_GROUPS, HEAD_DIM), 0.02),
        "sg_w": nrm((N_ODD, SG_GROUPS, CHUNK, CHUNK), CHUNK ** -0.5),
        "sg_b": 1.0 + nrm((N_ODD, SG_GROUPS, CHUNK), 0.02),
        "conv_w": nrm((N_ODD, CONV_WIDTH, 1, CONV_CH), CONV_WIDTH ** -0.5),
        "conv_b": nrm((N_ODD, CONV_CH), 0.02),
        "conv_ln_g": 1.0 + nrm((N_ODD, CONV_CH), 0.02),
        "conv_ln_b": nrm((N_ODD, CONV_CH), 0.02),
    }


def reference(x, c, ctx, c_ctx, w_mod, b_mod, ln1_g, ln1_b, ln2_g, ln2_b, w_router, w_gate, w_up,
              w_down, ab_w_in, ab_w_out, sink, cd_w_in, cd_w_out, sg_ln_g, sg_ln_b, sg_w, sg_b,
              conv_w, conv_b, conv_ln_g, conv_ln_b):
    n = x.shape[1]
    ang_r, ang_c = axial_rope_angles(n)
    s_lat = jax.nn.silu(c)
    s_ctx = jax.nn.silu(c_ctx)
    x_lat, x_ctx = x, ctx
    for l in range(DEPTH):
        i = l // 2
        upd_ctx = context_read_after(l)
        sh1, sc1, g1, sh2, sc2, g2 = [m[:, None, :] for m in
                                      jnp.split(s_lat @ w_mod[l] + b_mod[l], 6, axis=-1)]
        use_ctx = (l % 2 == 0) or upd_ctx
        if use_ctx:
            csh1, csc1, cg1, csh2, csc2, cg2 = jnp.split(s_ctx @ w_mod[l] + b_mod[l], 6, axis=-1)
            h_ctx = modulate(x_ctx, csh1, csc1)
        h_lat = modulate(x_lat, sh1, sc1)
        if l % 2 == 0:
            y_lat, y_ctx = mixer_ab(h_lat, h_ctx, ab_w_in[i], ab_w_out[i], sink[i], ang_r, ang_c, upd_ctx)
        else:
            cd = (cd_w_in[i], cd_w_out[i], sg_ln_g[i], sg_ln_b[i], sg_w[i], sg_b[i],
                  conv_w[i], conv_b[i], conv_ln_g[i], conv_ln_b[i])
            y_lat = mixer_cd(h_lat, *cd)
            y_ctx = mixer_cd(h_ctx, *cd) if upd_ctx else None
        moe = (w_router[l], w_gate[l], w_up[l], w_down[l])
        x_lat = layer_norm(ALPHA * x_lat + g1 * y_lat, ln1_g[l], ln1_b[l])
        x_lat = layer_norm(ALPHA * x_lat + g2 * expert_choice_ffn(modulate(x_lat, sh2, sc2), *moe),
                           ln2_g[l], ln2_b[l])
        if upd_ctx:
            x_ctx = layer_norm(ALPHA * x_ctx + cg1 * y_ctx, ln1_g[l], ln1_b[l])
            x_ctx = layer_norm(ALPHA * x_ctx + cg2 * expert_choice_ffn(modulate(x_ctx, csh2, csc2), *moe),
                               ln2_g[l], ln2_b[l])
    return x_lat
```

```python
import functools

import jax
import jax.numpy as jnp
from jax import lax
from jax.experimental import pallas as pl
from jax.experimental.pallas import tpu as pltpu

F32 = jnp.float32
BF16 = jnp.bfloat16

D = 2048
HEAD = 128
N_Q = 12
N_KV = 4
GQA = 3
WINDOW = 128
GRID_W = 64
ROPE_BASE = 10000.0
Q_END = N_Q * HEAD
KV_W = N_KV * HEAD
K_END = Q_END + KV_W
V_END = K_END + KV_W
FOUR_G = 4
FOUR_W = FOUR_G * HEAD
AB_IN = V_END + FOUR_W
SG_G = 8
SG_W = SG_G * HEAD
CHUNK = 128
CONV_CH = 1024
CONV_K = 31
CD_IN = 4096
N_EXP = 16
EXP_FF = 2048
DEPTH = 2
ALPHA = (2 * DEPTH) ** 0.25
LN_EPS = 1e-6
ATTN_SCALE = HEAD ** -0.5
MOD_ROWS = 16
LANES = 128
VMEM_LIMIT = 56 * 1024 * 1024


def _cparams(sem):
    return pltpu.CompilerParams(dimension_semantics=sem, vmem_limit_bytes=VMEM_LIMIT)


def _ln(x):
    mu = jnp.mean(x, axis=-1, keepdims=True)
    xc = x - mu
    var = jnp.mean(xc * xc, axis=-1, keepdims=True)
    return xc * lax.rsqrt(var + LN_EPS)


def _mod_kernel(c_ref, w_ref, b_ref, o_ref):
    s = jax.nn.silu(c_ref[...]).astype(BF16)
    o_ref[...] = jnp.dot(s, w_ref[...].astype(BF16), preferred_element_type=F32) + b_ref[...]


def _modulation(c_pad, w_mod, b_mod):
    tn = 1024
    return pl.pallas_call(
        _mod_kernel,
        out_shape=jax.ShapeDtypeStruct((DEPTH, MOD_ROWS, 6 * D), F32),
        grid=(DEPTH, 6 * D // tn),
        in_specs=[
            pl.BlockSpec((MOD_ROWS, D), lambda l, j: (0, 0)),
            pl.BlockSpec((None, D, tn), lambda l, j: (l, 0, j)),
            pl.BlockSpec((None, 1, tn), lambda l, j: (l, 0, j)),
        ],
        out_specs=pl.BlockSpec((None, MOD_ROWS, tn), lambda l, j: (l, 0, j)),
        compiler_params=_cparams(("arbitrary", "arbitrary")),
        name="modulation",
    )(c_pad, w_mod, b_mod.reshape(DEPTH, 1, 6 * D))


def _rope(a, cos, sa, sb):
    return a * cos + pltpu.roll(a, HEAD - 32, 1) * sa + pltpu.roll(a, 32, 1) * sb


def _ab_in_kernel(x_ref, sh_ref, sc_ref, w_ref, cos_ref, sa_ref, sb_ref, o_ref, *, n_cols, rope_cols, q_cols):
    h = (_ln(x_ref[...]) * (1.0 + sc_ref[...]) + sh_ref[...]).astype(BF16)
    chunk = 512
    for c0 in range(0, n_cols, chunk):
        acc = jnp.dot(h, w_ref[:, c0:c0 + chunk], preferred_element_type=F32)
        if c0 < rope_cols:
            cos, sa, sb = cos_ref[...], sa_ref[...], sb_ref[...]
            parts = []
            for j in range(chunk // HEAD):
                r = _rope(acc[:, j * HEAD:(j + 1) * HEAD], cos, sa, sb)
                parts.append(r * ATTN_SCALE if c0 < q_cols else r)
            acc = jnp.concatenate(parts, axis=1)
        o_ref[:, c0:c0 + chunk] = acc.astype(BF16)


def _ab_in_proj(x2d, mod5, layer, row_of_tile, w_bf, rope_tabs, *, n_seq, rope_cols, q_cols, name):
    rows, _ = x2d.shape
    n_cols = w_bf.shape[1]
    tm = 256
    tiles_per_seq = n_seq // tm
    cos, sa, sb = rope_tabs
    kern = functools.partial(_ab_in_kernel, n_cols=n_cols, rope_cols=rope_cols, q_cols=q_cols)
    mod_spec = lambda chunk: pl.BlockSpec((None, None, None, 1, D),
                                          lambda i: (layer, row_of_tile(i // tiles_per_seq), chunk, 0, 0))
    tab_spec = pl.BlockSpec((tm, HEAD), lambda i: (i % tiles_per_seq, 0))
    return pl.pallas_call(
        kern,
        out_shape=jax.ShapeDtypeStruct((rows, n_cols), BF16),
        grid=(rows // tm,),
        in_specs=[
            pl.BlockSpec((tm, D), lambda i: (i, 0)),
            mod_spec(0), mod_spec(1),
            pl.BlockSpec((D, n_cols), lambda i: (0, 0)),
            tab_spec, tab_spec, tab_spec,
        ],
        out_specs=pl.BlockSpec((tm, n_cols), lambda i: (i, 0)),
        compiler_params=_cparams(("arbitrary",)),
        name=name,
    )(x2d, mod5, mod5, w_bf, cos, sa, sb)


def _rope_tables(n):
    quarter = HEAD // 4
    t = jnp.arange(n)
    r = (t // GRID_W).astype(F32)
    col = (t % GRID_W).astype(F32)
    inv = ROPE_BASE ** (-jnp.arange(quarter, dtype=F32) / quarter)
    ang_r, ang_c = r[:, None] * inv, col[:, None] * inv
    zero = jnp.zeros_like(ang_r)
    cos = jnp.concatenate([jnp.cos(ang_r)] * 2 + [jnp.cos(ang_c)] * 2, axis=1)
    sa = jnp.concatenate([-jnp.sin(ang_r), zero, -jnp.sin(ang_c), zero], axis=1)
    sb = jnp.concatenate([zero, jnp.sin(ang_r), zero, jnp.sin(ang_c)], axis=1)
    return cos, sa, sb


def _attn_kernel(sink_ref, q_ref, k_ref, v_ref, kc_ref, vc_ref, o_ref, *, n_seq):
    hk = pl.program_id(1)
    kc = kc_ref[...]
    vc = vc_ref[...]
    kw_len = 3 * WINDOW
    rows = GQA * WINDOW
    row = lax.broadcasted_iota(jnp.int32, (rows, 1), 0)
    qoff = row & (WINDOW - 1)
    sink = jnp.where(row < WINDOW, sink_ref[hk * GQA],
                     jnp.where(row < 2 * WINDOW, sink_ref[hk * GQA + 1], sink_ref[hk * GQA + 2]))
    kiota = lax.broadcasted_iota(jnp.int32, (1, kw_len), 1)
    dn = (((1,), (1,)), ((), ()))

    def body(n, carry):
        r0 = pl.multiple_of(n * WINDOW, WINDOW)
        start = pl.multiple_of(jnp.clip(r0 - WINDOW, 0, n_seq - kw_len), WINDOW)
        qs = q_ref[pl.ds(r0, WINDOW), :]
        q3 = jnp.concatenate([qs[:, g * HEAD:(g + 1) * HEAD] for g in range(GQA)], axis=0)
        kw = k_ref[pl.ds(start, kw_len), :]
        vw = v_ref[pl.ds(start, kw_len), :]
        s_w = lax.dot_general(q3, kw, dn, preferred_element_type=F32)
        s_c = lax.dot_general(q3, kc, dn, preferred_element_type=F32)
        rel = (r0 + qoff) - (start + kiota)
        s_w = jnp.where(jnp.abs(rel) <= WINDOW, s_w, -1e30)
        m = jnp.maximum(jnp.maximum(jnp.max(s_w, axis=1, keepdims=True), jnp.max(s_c, axis=1, keepdims=True)), sink)
        p_w = jnp.exp(s_w - m)
        p_c = jnp.exp(s_c - m)
        den = jnp.sum(p_w, axis=1, keepdims=True) + jnp.sum(p_c, axis=1, keepdims=True) + jnp.exp(sink - m)
        o = (jnp.dot(p_w.astype(BF16), vw, preferred_element_type=F32)
             + jnp.dot(p_c.astype(BF16), vc, preferred_element_type=F32)) / den
        o_ref[pl.ds(r0, WINDOW), :] = jnp.concatenate(
            [o[g * WINDOW:(g + 1) * WINDOW, :] for g in range(GQA)], axis=1).astype(BF16)
        return carry

    lax.fori_loop(0, n_seq // WINDOW, body, 0)


def _attention(p_lat, p_ctx, sink, *, batch, n_seq, n_ctx):
    qw = GQA * HEAD
    return pl.pallas_call(
        functools.partial(_attn_kernel, n_seq=n_seq),
        out_shape=jax.ShapeDtypeStruct((batch * n_seq, Q_END), BF16),
        grid=(batch, N_KV),
        in_specs=[
            pl.BlockSpec(memory_space=pltpu.SMEM),
            pl.BlockSpec((n_seq, qw), lambda b, h: (b, h)),
            pl.BlockSpec((n_seq, HEAD), lambda b, h: (b, Q_END // HEAD + h)),
            pl.BlockSpec((n_seq, HEAD), lambda b, h: (b, K_END // HEAD + h)),
            pl.BlockSpec((n_ctx, HEAD), lambda b, h: (b, h)),
            pl.BlockSpec((n_ctx, HEAD), lambda b, h: (b, N_KV + h)),
        ],
        out_specs=pl.BlockSpec((n_seq, qw), lambda b, h: (b, h)),
        compiler_params=_cparams(("arbitrary", "arbitrary")),
        name="banded_gqa",
    )(sink, p_lat, p_lat, p_lat, p_ctx, p_ctx)


def _fourier_kernel(z_ref, cs_ref, cn_ref, sn_ref, o_ref):
    cs = cs_ref[...]
    a_parts, b_parts = [], []
    for g in range(FOUR_G):
        ab = jnp.dot(z_ref[:, g * HEAD:(g + 1) * HEAD], cs, preferred_element_type=F32)
        a_parts.append(ab[:, :HEAD])
        b_parts.append(ab[:, HEAD:])
    a = jnp.concatenate(a_parts, axis=1).astype(BF16)
    b = jnp.concatenate(b_parts, axis=1).astype(BF16)
    out = (jnp.dot(cn_ref[...], a, preferred_element_type=F32)
           - jnp.dot(sn_ref[...], b, preferred_element_type=F32))
    o_ref[...] = (out * (1.0 / (z_ref.shape[0] * HEAD) ** 0.5)).astype(BF16)


def _dft_tables(n):
    def cs(m):
        k = jnp.arange(m, dtype=jnp.int32)
        ang = ((k[:, None] * k[None, :]) % m).astype(F32) * (2.0 * jnp.pi / m)
        return jnp.cos(ang), jnp.sin(ang)
    cn, sn = cs(n)
    cc, sc = cs(HEAD)
    return jnp.concatenate([cc, sc], axis=1).astype(BF16), cn.astype(BF16), sn.astype(BF16)


def _fourier(p_lat, *, batch, n_seq):
    cs, cn, sn = _dft_tables(n_seq)
    return pl.pallas_call(
        _fourier_kernel,
        out_shape=jax.ShapeDtypeStruct((batch * n_seq, FOUR_W), BF16),
        grid=(batch,),
        in_specs=[
            pl.BlockSpec((n_seq, FOUR_W), lambda b: (b, V_END // FOUR_W)),
            pl.BlockSpec((HEAD, 2 * HEAD), lambda b: (0, 0)),
            pl.BlockSpec((n_seq, n_seq), lambda b: (0, 0)),
            pl.BlockSpec((n_seq, n_seq), lambda b: (0, 0)),
        ],
        out_specs=pl.BlockSpec((n_seq, FOUR_W), lambda b: (b, 0)),
        compiler_params=_cparams(("arbitrary",)),
        name="fourier_mix",
    )(p_lat, cs, cn, sn)


def _out_proj_kernel(*refs, n_parts):
    a_refs = refs[:n_parts]
    (w_ref, x_ref, g1_ref, lg_ref, lb_ref, sh_ref, sc_ref, wr_ref, x1_ref, h2_ref, lo_ref) = refs[n_parts:]
    y = None
    k0 = 0
    for a_ref in a_refs:
        kk = a_ref.shape[1]
        t = jnp.dot(a_ref[...], w_ref[k0:k0 + kk, :], preferred_element_type=F32)
        y = t if y is None else y + t
        k0 += kk
    x1 = _ln(ALPHA * x_ref[...] + g1_ref[...] * y) * lg_ref[...] + lb_ref[...]
    x1_ref[...] = x1
    h2 = (_ln(x1) * (1.0 + sc_ref[...]) + sh_ref[...]).astype(BF16)
    h2_ref[...] = h2
    lo_ref[...] = jnp.dot(h2, wr_ref[...], preferred_element_type=F32)


def _out_proj(parts, w_bf, x2d, mod5, layer, ln_g, ln_b, w_router_pad, *, n_seq):
    rows = x2d.shape[0]
    tm = 256
    tiles_per_seq = n_seq // tm
    mod_spec = lambda chunk: pl.BlockSpec((None, None, None, 1, D),
                                          lambda i: (layer, i // tiles_per_seq, chunk, 0, 0))
    vec_spec = pl.BlockSpec((1, D), lambda i: (0, 0))
    in_specs = [pl.BlockSpec((tm, p.shape[1]), lambda i: (i, 0)) for p in parts]
    in_specs += [
        pl.BlockSpec((D, D), lambda i: (0, 0)),
        pl.BlockSpec((tm, D), lambda i: (i, 0)),
        mod_spec(2), vec_spec, vec_spec, mod_spec(3), mod_spec(4),
        pl.BlockSpec((D, LANES), lambda i: (0, 0)),
    ]
    return pl.pallas_call(
        functools.partial(_out_proj_kernel, n_parts=len(parts)),
        out_shape=(jax.ShapeDtypeStruct((rows, D), F32),
                   jax.ShapeDtypeStruct((rows, D), BF16),
                   jax.ShapeDtypeStruct((rows, LANES), F32)),
        grid=(rows // tm,),
        in_specs=in_specs,
        out_specs=(pl.BlockSpec((tm, D), lambda i: (i, 0)),
                   pl.BlockSpec((tm, D), lambda i: (i, 0)),
                   pl.BlockSpec((tm, LANES), lambda i: (i, 0))),
        compiler_params=_cparams(("arbitrary",)),
        name="out_proj_ln",
    )(*parts, w_bf, x2d, mod5, ln_g.reshape(1, D), ln_b.reshape(1, D), mod5, mod5, w_router_pad)


def _route_kernel(lo_ref, slot_row_ref, slot_col_ref, gate_col_ref, tri_ref, *, n_seq, cap):
    @pl.when(pl.program_id(0) == 0)
    def _():
        blk = 256
        r = lax.broadcasted_iota(jnp.int32, (blk, n_seq), 0)
        c = lax.broadcasted_iota(jnp.int32, (blk, n_seq), 1)
        for r0 in range(0, n_seq, blk):
            tri_ref[r0:r0 + blk, :] = jnp.where(r + r0 < c, 1.0, 0.0).astype(BF16)

    lt = jnp.transpose(lo_ref[...])[:N_EXP, :]
    ex = jnp.exp(lt - jnp.max(lt, axis=0, keepdims=True))
    aff = ex / jnp.sum(ex, axis=0, keepdims=True)
    bits = lax.bitcast_convert_type(aff, jnp.int32)

    def bisect(_, lohi):
        lo, hi = lohi
        mid = lo + ((hi - lo) >> 1)
        cnt = jnp.sum((bits >= mid).astype(jnp.int32), axis=1, keepdims=True)
        ok = cnt >= cap
        return jnp.where(ok, mid, lo), jnp.where(ok, hi, mid)

    lo0 = jnp.zeros((N_EXP, 1), jnp.int32)
    hi0 = jnp.full((N_EXP, 1), 0x7F800000, jnp.int32)
    thr, _ = lax.fori_loop(0, 31, bisect, (lo0, hi0))
    gt = bits > thr
    eq = bits == thr
    need = cap - jnp.sum(gt.astype(jnp.int32), axis=1, keepdims=True)
    tri = tri_ref[...]
    eq_before = jnp.dot(jnp.where(eq, 1.0, 0.0).astype(BF16), tri, preferred_element_type=F32)
    sel = gt | (eq & (eq_before < need.astype(F32)))
    slot = jnp.dot(jnp.where(sel, 1.0, 0.0).astype(BF16), tri, preferred_element_type=F32)
    slot = jnp.where(sel, slot, -1.0)
    gate = jnp.where(sel, aff, 0.0)
    slot_row_ref[...] = slot
    pad = jnp.full((LANES - N_EXP, n_seq), -1.0, F32)
    slot_col_ref[...] = jnp.transpose(jnp.concatenate([slot, pad], axis=0))
    gate_col_ref[...] = jnp.transpose(jnp.concatenate([gate, pad * 0.0], axis=0))


def _route(logits, *, batch, n_seq, cap):
    return pl.pallas_call(
        functools.partial(_route_kernel, n_seq=n_seq, cap=cap),
        out_shape=(jax.ShapeDtypeStruct((batch, N_EXP, n_seq), F32),
                   jax.ShapeDtypeStruct((batch * n_seq, LANES), F32),
                   jax.ShapeDtypeStruct((batch * n_seq, LANES), F32)),
        grid=(batch,),
        in_specs=[pl.BlockSpec((n_seq, LANES), lambda b: (b, 0))],
        out_specs=(pl.BlockSpec((None, N_EXP, n_seq), lambda b: (b, 0, 0)),
                   pl.BlockSpec((n_seq, LANES), lambda b: (b, 0)),
                   pl.BlockSpec((n_seq, LANES), lambda b: (b, 0))),
        scratch_shapes=[pltpu.VMEM((n_seq, n_seq), BF16)],
        compiler_params=_cparams(("arbitrary",)),
        name="ec_route",
    )(logits)


def _dispatch_kernel(slot_ref, h_ref, o_ref, *, e_blk, cap):
    srow = lax.broadcasted_iota(jnp.int32, (cap, 1), 0).astype(F32)
    n_cols = h_ref.shape[1]
    chunk = 512
    for e in range(e_blk):
        onehot = jnp.where(slot_ref[e:e + 1, :] == srow, 1.0, 0.0).astype(BF16)
        for c0 in range(0, n_cols, chunk):
            o_ref[e, :, c0:c0 + chunk] = jnp.dot(onehot, h_ref[:, c0:c0 + chunk],
                                                 preferred_element_type=F32).astype(BF16)


def _dispatch(slot_row, h2, *, batch, n_seq, cap):
    e_blk = 8
    return pl.pallas_call(
        functools.partial(_dispatch_kernel, e_blk=e_blk, cap=cap),
        out_shape=jax.ShapeDtypeStruct((N_EXP, batch * cap, D), BF16),
        grid=(batch, N_EXP // e_blk),
        in_specs=[
            pl.BlockSpec((None, e_blk, n_seq), lambda b, j: (b, j, 0)),
            pl.BlockSpec((n_seq, D), lambda b, j: (b, 0)),
        ],
        out_specs=pl.BlockSpec((e_blk, cap, D), lambda b, j: (j, b, 0)),
        compiler_params=_cparams(("arbitrary", "arbitrary")),
        name="ec_dispatch",
    )(slot_row, h2)


def _gate_up_kernel(x_ref, wg_ref, wu_ref, o_ref):
    x = x_ref[...]
    g = jnp.dot(x, wg_ref[...].astype(BF16), preferred_element_type=F32)
    u = jnp.dot(x, wu_ref[...].astype(BF16), preferred_element_type=F32)
    o_ref[...] = (jax.nn.silu(g) * u).astype(BF16)


def _down_kernel(h_ref, wd_ref, o_ref):
    y = jnp.dot(h_ref[...], wd_ref[...].astype(BF16), preferred_element_type=F32).astype(BF16)
    cap = o_ref.shape[1]
    for b in range(o_ref.shape[0]):
        o_ref[b] = y[b * cap:(b + 1) * cap, :]


def _experts(xs, w_gate, w_up, w_down, layer, *, batch, cap):
    m = batch * cap
    tn = 256
    hid = pl.pallas_call(
        _gate_up_kernel,
        out_shape=jax.ShapeDtypeStruct((N_EXP, m, EXP_FF), BF16),
        grid=(N_EXP, EXP_FF // tn),
        in_specs=[
            pl.BlockSpec((None, m, D), lambda e, j: (e, 0, 0)),
            pl.BlockSpec((None, None, D, tn), lambda e, j: (layer, e, 0, j)),
            pl.BlockSpec((None, None, D, tn), lambda e, j: (layer, e, 0, j)),
        ],
        out_specs=pl.BlockSpec((None, m, tn), lambda e, j: (e, 0, j)),
        compiler_params=_cparams(("arbitrary", "arbitrary")),
        name="expert_gate_up",
    )(xs, w_gate, w_up)
    return pl.pallas_call(
        _down_kernel,
        out_shape=jax.ShapeDtypeStruct((batch, N_EXP * cap, D), BF16),
        grid=(N_EXP, D // tn),
        in_specs=[
            pl.BlockSpec((None, m, EXP_FF), lambda e, j: (e, 0, 0)),
            pl.BlockSpec((None, None, EXP_FF, tn), lambda e, j: (layer, e, 0, j)),
        ],
        out_specs=pl.BlockSpec((batch, cap, tn), lambda e, j: (0, e, j)),
        compiler_params=_cparams(("arbitrary", "arbitrary")),
        name="expert_down",
    )(hid, w_down)


def _combine_kernel(ye_ref, slot_ref, gate_ref, x_ref, g2_ref, lg_ref, lb_ref, o_ref, *, cap):
    ye = ye_ref[...]
    siota = lax.broadcasted_iota(jnp.int32, (1, cap), 1).astype(F32)
    slot = slot_ref[...]
    gate = gate_ref[...]
    parts = [jnp.where(slot[:, e:e + 1] == siota, gate[:, e:e + 1], 0.0).astype(BF16) for e in range(N_EXP)]
    comb = jnp.concatenate(parts, axis=1)
    moe = jnp.dot(comb, ye, preferred_element_type=F32)
    o_ref[...] = _ln(ALPHA * x_ref[...] + g2_ref[...] * moe) * lg_ref[...] + lb_ref[...]


def _combine(ye, slot_col, gate_col, x1, mod5, layer, ln_g, ln_b, *, batch, n_seq, cap):
    tm = 256
    tiles = n_seq // tm
    vec_spec = pl.BlockSpec((1, D), lambda b, i: (0, 0))
    return pl.pallas_call(
        functools.partial(_combine_kernel, cap=cap),
        out_shape=jax.ShapeDtypeStruct((batch * n_seq, D), F32),
        grid=(batch, tiles),
        in_specs=[
            pl.BlockSpec((None, N_EXP * cap, D), lambda b, i: (b, 0, 0)),
            pl.BlockSpec((tm, LANES), lambda b, i: (b * tiles + i, 0)),
            pl.BlockSpec((tm, LANES), lambda b, i: (b * tiles + i, 0)),
            pl.BlockSpec((tm, D), lambda b, i: (b * tiles + i, 0)),
            pl.BlockSpec((None, None, None, 1, D), lambda b, i: (layer, b, 5, 0, 0)),
            vec_spec, vec_spec,
        ],
        out_specs=pl.BlockSpec((tm, D), lambda b, i: (b * tiles + i, 0)),
        compiler_params=_cparams(("arbitrary", "arbitrary")),
        name="ec_combine_ln",
    )(ye, slot_col, gate_col, x1, mod5, ln_g.reshape(1, D), ln_b.reshape(1, D))


def _moe_block(x1, h2, logits, mod5, layer, w_gate, w_up, w_down, ln_g, ln_b, *, batch, n_seq):
    cap = 2 * n_seq // N_EXP
    slot_row, slot_col, gate_col = _route(logits, batch=batch, n_seq=n_seq, cap=cap)
    xs = _dispatch(slot_row, h2, batch=batch, n_seq=n_seq, cap=cap)
    ye = _experts(xs, w_gate, w_up, w_down, layer, batch=batch, cap=cap)
    return _combine(ye, slot_col, gate_col, x1, mod5, layer, ln_g, ln_b, batch=batch, n_seq=n_seq, cap=cap)


def _cd_in_kernel(x_ref, sh_ref, sc_ref, w_ref, lg_ref, lb_ref, u_ref, vg_ref, xg_ref):
    h = (_ln(x_ref[...]) * (1.0 + sc_ref[...]) + sh_ref[...]).astype(BF16)
    chunk = 512
    for c0 in range(0, SG_W, chunk):
        u_ref[:, c0:c0 + chunk] = jax.nn.gelu(
            jnp.dot(h, w_ref[:, c0:c0 + chunk], preferred_element_type=F32)).astype(BF16)
    for c0 in range(0, SG_W, chunk):
        z = jax.nn.gelu(jnp.dot(h, w_ref[:, SG_W + c0:SG_W + c0 + chunk], preferred_element_type=F32))
        parts = [_ln(z[:, j * HEAD:(j + 1) * HEAD]) for j in range(chunk // HEAD)]
        vg = jnp.concatenate(parts, axis=1) * lg_ref[:, c0:c0 + chunk] + lb_ref[:, c0:c0 + chunk]
        vg_ref[:, c0:c0 + chunk] = vg.astype(BF16)
    for c0 in range(0, CONV_CH, chunk):
        a = jnp.dot(h, w_ref[:, 2 * SG_W + c0:2 * SG_W + c0 + chunk], preferred_element_type=F32)
        gt = jnp.dot(h, w_ref[:, 2 * SG_W + CONV_CH + c0:2 * SG_W + CONV_CH + c0 + chunk],
                     preferred_element_type=F32)
        xg_ref[:, c0:c0 + chunk] = a * jax.nn.sigmoid(gt)


def _cd_in_proj(x2d, mod5, layer, w_bf, sg_ln_g, sg_ln_b, *, n_seq):
    rows = x2d.shape[0]
    tm = 256
    tiles_per_seq = n_seq // tm
    mod_spec = lambda chunk: pl.BlockSpec((None, None, None, 1, D),
                                          lambda i: (layer, i // tiles_per_seq, chunk, 0, 0))
    vec_spec = pl.BlockSpec((1, SG_W), lambda i: (0, 0))
    row_spec = lambda w: pl.BlockSpec((tm, w), lambda i: (i, 0))
    return pl.pallas_call(
        _cd_in_kernel,
        out_shape=(jax.ShapeDtypeStruct((rows, SG_W), BF16),
                   jax.ShapeDtypeStruct((rows, SG_W), BF16),
                   jax.ShapeDtypeStruct((rows, CONV_CH), F32)),
        grid=(rows // tm,),
        in_specs=[row_spec(D), mod_spec(0), mod_spec(1),
                  pl.BlockSpec((D, CD_IN), lambda i: (0, 0)), vec_spec, vec_spec],
        out_specs=(row_spec(SG_W), row_spec(SG_W), row_spec(CONV_CH)),
        compiler_params=_cparams(("arbitrary",)),
        name="cd_in_proj",
    )(x2d, mod5, mod5, w_bf, sg_ln_g.reshape(1, SG_W), sg_ln_b.reshape(1, SG_W))


CONV_HALO = 16
CONV_ROWS = 32


def _cd_mix_kernel(u_ref, vg_ref, xp_ref, xc_ref, xn_ref, sgw_ref, sgb_ref, cw_ref, cb_ref, lg_ref, lb_ref,
                   o_ref, pad_ref, cv_ref, *, tiles_per_seq):
    i = pl.program_id(1)
    tm = xc_ref.shape[0]
    for c in range(tm // CHUNK):
        for g in range(SG_G):
            rs, cs = slice(c * CHUNK, (c + 1) * CHUNK), slice(g * HEAD, (g + 1) * HEAD)
            sp = jnp.dot(sgw_ref[g], vg_ref[rs, cs], preferred_element_type=F32) + sgb_ref[g]
            o_ref[rs, cs] = (u_ref[rs, cs].astype(F32) * sp).astype(BF16)
    pad_ref[0:CONV_HALO, :] = jnp.where(i > 0, xp_ref[...], 0.0)
    pad_ref[CONV_HALO:CONV_HALO + tm, :] = xc_ref[...]
    pad_ref[CONV_HALO + tm:, :] = jnp.where(i < tiles_per_seq - 1, xn_ref[...], 0.0)
    off = CONV_HALO - CONV_K // 2
    for r0 in range(0, tm, CONV_ROWS):
        for c0 in range(0, CONV_CH, LANES):
            acc = jnp.zeros((CONV_ROWS, LANES), F32) + cb_ref[:, c0:c0 + LANES]
            for k in range(CONV_K):
                acc = acc + pad_ref[r0 + off + k:r0 + off + k + CONV_ROWS, c0:c0 + LANES] * cw_ref[k:k + 1, c0:c0 + LANES]
            cv_ref[r0:r0 + CONV_ROWS, c0:c0 + LANES] = acc
    y = _ln(cv_ref[...]) * lg_ref[...] + lb_ref[...]
    o_ref[:, SG_W:] = jax.nn.silu(y).astype(BF16)


def _cd_mix(u, vg, xg, sg_w, sg_b, conv_w, conv_b, conv_ln_g, conv_ln_b, *, batch, n_seq):
    tm = 128
    tiles = n_seq // tm
    hpt = tm // CONV_HALO
    n_halo = batch * n_seq // CONV_HALO
    row_spec = lambda w: pl.BlockSpec((tm, w), lambda b, i: (b * tiles + i, 0))
    vec_spec = pl.BlockSpec((1, CONV_CH), lambda b, i: (0, 0))
    sgb_full = jnp.broadcast_to(sg_b[:, :, None], (SG_G, CHUNK, HEAD))
    return pl.pallas_call(
        functools.partial(_cd_mix_kernel, tiles_per_seq=tiles),
        out_shape=jax.ShapeDtypeStruct((batch * n_seq, D), BF16),
        grid=(batch, tiles),
        in_specs=[
            row_spec(SG_W), row_spec(SG_W),
            pl.BlockSpec((CONV_HALO, CONV_CH), lambda b, i: (jnp.maximum((b * tiles + i) * hpt - 1, 0), 0)),
            row_spec(CONV_CH),
            pl.BlockSpec((CONV_HALO, CONV_CH), lambda b, i: (jnp.minimum((b * tiles + i + 1) * hpt, n_halo - 1), 0)),
            pl.BlockSpec((SG_G, CHUNK, CHUNK), lambda b, i: (0, 0, 0)),
            pl.BlockSpec((SG_G, CHUNK, HEAD), lambda b, i: (0, 0, 0)),
            pl.BlockSpec((CONV_K + 1, CONV_CH), lambda b, i: (0, 0)),
            vec_spec, vec_spec, vec_spec,
        ],
        out_specs=pl.BlockSpec((tm, D), lambda b, i: (b * tiles + i, 0)),
        scratch_shapes=[pltpu.VMEM((tm + 2 * CONV_HALO, CONV_CH), F32), pltpu.VMEM((tm, CONV_CH), F32)],
        compiler_params=_cparams(("arbitrary", "arbitrary")),
        name="cd_mix",
    )(u, vg, xg, xg, xg, sg_w.astype(BF16), sgb_full,
      jnp.pad(conv_w.reshape(CONV_K, CONV_CH), ((0, 1), (0, 0))),
      conv_b.reshape(1, CONV_CH), conv_ln_g.reshape(1, CONV_CH), conv_ln_b.reshape(1, CONV_CH))


def kernel(x, c, ctx, c_ctx, w_mod, b_mod, ln1_g, ln1_b, ln2_g, ln2_b, w_router, w_gate, w_up, w_down,
           ab_w_in, ab_w_out, sink, cd_w_in, cd_w_out, sg_ln_g, sg_ln_b, sg_w, sg_b,
           conv_w, conv_b, conv_ln_g, conv_ln_b):
    batch, n_seq, _ = x.shape
    n_ctx = ctx.shape[1]
    assert x.shape[2] == D and batch + 1 <= MOD_ROWS

    c_pad = jnp.concatenate([c, c_ctx[None, :], jnp.zeros((MOD_ROWS - batch - 1, D), F32)], axis=0)
    mod = _modulation(c_pad, w_mod, b_mod)
    mod5 = mod.reshape(DEPTH, MOD_ROWS, 6, 1, D)
    w_router_pad = jnp.pad(w_router, ((0, 0), (0, 0), (0, LANES - N_EXP))).astype(BF16)

    x0 = x.reshape(batch * n_seq, D)
    ctx2d = ctx.reshape(batch * n_ctx, D)

    w_in = ab_w_in[0].astype(BF16)
    tabs = _rope_tables(n_seq)
    p_lat = _ab_in_proj(x0, mod5, 0, lambda b: b, w_in, tabs, n_seq=n_seq,
                        rope_cols=K_END, q_cols=Q_END, name="ab_in_proj")
    p_ctx = _ab_in_proj(ctx2d, mod5, 0, lambda b: batch, w_in[:, Q_END:V_END], tabs, n_seq=n_ctx,
                        rope_cols=0, q_cols=0, name="ab_ctx_proj")
    attn = _attention(p_lat, p_ctx, sink[0], batch=batch, n_seq=n_seq, n_ctx=n_ctx)
    four = _fourier(p_lat, batch=batch, n_seq=n_seq)
    x1, h2, logits = _out_proj([attn, four], ab_w_out[0].astype(BF16), x0, mod5, 0, ln1_g[0], ln1_b[0],
                               w_router_pad[0], n_seq=n_seq)
    x2 = _moe_block(x1, h2, logits, mod5, 0, w_gate, w_up, w_down, ln2_g[0], ln2_b[0], batch=batch, n_seq=n_seq)

    u, vg, xg = _cd_in_proj(x2, mod5, 1, cd_w_in[0].astype(BF16), sg_ln_g[0], sg_ln_b[0], n_seq=n_seq)
    mix = _cd_mix(u, vg, xg, sg_w[0], sg_b[0], conv_w[0], conv_b[0], conv_ln_g[0], conv_ln_b[0],
                  batch=batch, n_seq=n_seq)
    x3, h4, logits1 = _out_proj([mix], cd_w_out[0].astype(BF16), x2, mod5, 1, ln1_g[1], ln1_b[1],
                                w_router_pad[1], n_seq=n_seq)
    x4 = _moe_block(x3, h4, logits1, mod5, 1, w_gate, w_up, w_down, ln2_g[1], ln2_b[1], batch=batch, n_seq=n_seq)
    return x4.reshape(batch, n_seq, D)
```

```python
import functools

import jax
import jax.numpy as jnp
from jax import lax
from jax.experimental import pallas as pl
from jax.experimental.pallas import tpu as pltpu

F32 = jnp.float32
BF16 = jnp.bfloat16

D = 2048
HEAD = 128
N_Q = 12
N_KV = 4
GQA = 3
WINDOW = 128
GRID_W = 64
ROPE_BASE = 10000.0
Q_END = N_Q * HEAD
KV_W = N_KV * HEAD
K_END = Q_END + KV_W
V_END = K_END + KV_W
FOUR_G = 4
FOUR_W = FOUR_G * HEAD
AB_IN = V_END + FOUR_W
SG_G = 8
SG_W = SG_G * HEAD
CHUNK = 128
CONV_CH = 1024
CONV_K = 31
CD_IN = 4096
N_EXP = 16
EXP_FF = 2048
DEPTH = 2
ALPHA = (2 * DEPTH) ** 0.25
LN_EPS = 1e-6
ATTN_SCALE = HEAD ** -0.5
MOD_ROWS = 16
LANES = 128
SUBLANES = 8
VMEM_LIMIT = 56 * 1024 * 1024
MOE_TILE = 256
MOE_WIN = 64
OFF_STRIDE = 16
TN_UP = 512
TN_DOWN = 512


def _cparams(sem):
    return pltpu.CompilerParams(dimension_semantics=sem, vmem_limit_bytes=VMEM_LIMIT)


def _ln(x):
    mu = jnp.mean(x, axis=-1, keepdims=True)
    xc = x - mu
    var = jnp.mean(xc * xc, axis=-1, keepdims=True)
    return xc * lax.rsqrt(var + LN_EPS)


def _mod_kernel(c_ref, w_ref, b_ref, o_ref):
    s = jax.nn.silu(c_ref[...]).astype(BF16)
    o_ref[...] = jnp.dot(s, w_ref[...].astype(BF16), preferred_element_type=F32) + b_ref[...]


def _modulation(c_pad, w_mod, b_mod):
    tn = 1024
    return pl.pallas_call(
        _mod_kernel,
        out_shape=jax.ShapeDtypeStruct((DEPTH, MOD_ROWS, 6 * D), F32),
        grid=(DEPTH, 6 * D // tn),
        in_specs=[
            pl.BlockSpec((MOD_ROWS, D), lambda l, j: (0, 0)),
            pl.BlockSpec((None, D, tn), lambda l, j: (l, 0, j)),
            pl.BlockSpec((None, 1, tn), lambda l, j: (l, 0, j)),
        ],
        out_specs=pl.BlockSpec((None, MOD_ROWS, tn), lambda l, j: (l, 0, j)),
        compiler_params=_cparams(("arbitrary", "arbitrary")),
        name="modulation",
    )(c_pad, w_mod, b_mod.reshape(DEPTH, 1, 6 * D))


def _rope(a, cos, sa, sb):
    return a * cos + pltpu.roll(a, HEAD - 32, 1) * sa + pltpu.roll(a, 32, 1) * sb


def _ab_in_kernel(x_ref, sh_ref, sc_ref, w_ref, cos_ref, sa_ref, sb_ref, o_ref, *, n_cols, rope_cols, q_cols):
    h = (_ln(x_ref[...]) * (1.0 + sc_ref[...]) + sh_ref[...]).astype(BF16)
    chunk = 512
    for c0 in range(0, n_cols, chunk):
        acc = jnp.dot(h, w_ref[:, c0:c0 + chunk], preferred_element_type=F32)
        if c0 < rope_cols:
            cos, sa, sb = cos_ref[...], sa_ref[...], sb_ref[...]
            parts = []
            for j in range(chunk // HEAD):
                r = _rope(acc[:, j * HEAD:(j + 1) * HEAD], cos, sa, sb)
                parts.append(r * ATTN_SCALE if c0 < q_cols else r)
            acc = jnp.concatenate(parts, axis=1)
        o_ref[:, c0:c0 + chunk] = acc.astype(BF16)


def _ab_in_proj(x2d, mod5, layer, row_of_tile, w_bf, rope_tabs, *, n_seq, rope_cols, q_cols, name):
    rows, _ = x2d.shape
    n_cols = w_bf.shape[1]
    tm = 256
    tiles_per_seq = n_seq // tm
    cos, sa, sb = rope_tabs
    kern = functools.partial(_ab_in_kernel, n_cols=n_cols, rope_cols=rope_cols, q_cols=q_cols)
    mod_spec = lambda chunk: pl.BlockSpec((None, None, None, 1, D),
                                          lambda i: (layer, row_of_tile(i // tiles_per_seq), chunk, 0, 0))
    tab_spec = pl.BlockSpec((tm, HEAD), lambda i: (i % tiles_per_seq, 0))
    return pl.pallas_call(
        kern,
        out_shape=jax.ShapeDtypeStruct((rows, n_cols), BF16),
        grid=(rows // tm,),
        in_specs=[
            pl.BlockSpec((tm, D), lambda i: (i, 0)),
            mod_spec(0), mod_spec(1),
            pl.BlockSpec((D, n_cols), lambda i: (0, 0)),
            tab_spec, tab_spec, tab_spec,
        ],
        out_specs=pl.BlockSpec((tm, n_cols), lambda i: (i, 0)),
        compiler_params=_cparams(("arbitrary",)),
        name=name,
    )(x2d, mod5, mod5, w_bf, cos, sa, sb)


def _rope_tables(n):
    quarter = HEAD // 4
    t = jnp.arange(n)
    r = (t // GRID_W).astype(F32)
    col = (t % GRID_W).astype(F32)
    inv = ROPE_BASE ** (-jnp.arange(quarter, dtype=F32) / quarter)
    ang_r, ang_c = r[:, None] * inv, col[:, None] * inv
    zero = jnp.zeros_like(ang_r)
    cos = jnp.concatenate([jnp.cos(ang_r)] * 2 + [jnp.cos(ang_c)] * 2, axis=1)
    sa = jnp.concatenate([-jnp.sin(ang_r), zero, -jnp.sin(ang_c), zero], axis=1)
    sb = jnp.concatenate([zero, jnp.sin(ang_r), zero, jnp.sin(ang_c)], axis=1)
    return cos, sa, sb


def _attn_kernel(sink_ref, q_ref, k_ref, v_ref, kc_ref, vc_ref, o_ref, bias_ref, *, n_seq):
    hk = pl.program_id(1)
    kc = kc_ref[...]
    vc = vc_ref[...]
    kw_len = 3 * WINDOW
    rows = GQA * WINDOW
    n_blocks = n_seq // WINDOW
    row = lax.broadcasted_iota(jnp.int32, (rows, 1), 0)
    sink = jnp.where(row < WINDOW, sink_ref[hk * GQA],
                     jnp.where(row < 2 * WINDOW, sink_ref[hk * GQA + 1], sink_ref[hk * GQA + 2]))
    rel = (row & (WINDOW - 1)) - lax.broadcasted_iota(jnp.int32, (1, kw_len), 1)
    for i in range(3):
        bias_ref[i] = jnp.where(jnp.abs(rel + i * WINDOW) <= WINDOW, 0.0, -1e30)
    dn = (((1,), (1,)), ((), ()))

    def block(n, back):
        r0 = n * WINDOW if isinstance(n, int) else pl.multiple_of(n * WINDOW, WINDOW)
        start = r0 - back * WINDOW
        if not isinstance(start, int):
            start = pl.multiple_of(start, WINDOW)
        qs = q_ref[pl.ds(r0, WINDOW), :]
        q3 = jnp.concatenate([qs[:, g * HEAD:(g + 1) * HEAD] for g in range(GQA)], axis=0)
        kw = k_ref[pl.ds(start, kw_len), :]
        vw = v_ref[pl.ds(start, kw_len), :]
        s_w = lax.dot_general(q3, kw, dn, preferred_element_type=F32) + bias_ref[back]
        s_c = lax.dot_general(q3, kc, dn, preferred_element_type=F32)
        m = jnp.maximum(jnp.maximum(jnp.max(s_w, axis=1, keepdims=True), jnp.max(s_c, axis=1, keepdims=True)), sink)
        p_w = jnp.exp(s_w - m)
        p_c = jnp.exp(s_c - m)
        den = jnp.sum(p_w, axis=1, keepdims=True) + jnp.sum(p_c, axis=1, keepdims=True) + jnp.exp(sink - m)
        o = (jnp.dot(p_w.astype(BF16), vw, preferred_element_type=F32)
             + jnp.dot(p_c.astype(BF16), vc, preferred_element_type=F32)) / den
        o_ref[pl.ds(r0, WINDOW), :] = jnp.concatenate(
            [o[g * WINDOW:(g + 1) * WINDOW, :] for g in range(GQA)], axis=1).astype(BF16)

    block(0, 0)
    block(n_blocks - 1, 2)

    def pair(i, carry):
        block(1 + 2 * i, 1)
        block(2 + 2 * i, 1)
        return carry

    lax.fori_loop(0, (n_blocks - 2) // 2, pair, 0)


def _attention(p_lat, p_ctx, sink, *, batch, n_seq, n_ctx):
    qw = GQA * HEAD
    assert n_seq % (2 * WINDOW) == 0 and n_seq >= 4 * WINDOW
    return pl.pallas_call(
        functools.partial(_attn_kernel, n_seq=n_seq),
        out_shape=jax.ShapeDtypeStruct((batch * n_seq, Q_END), BF16),
        grid=(batch, N_KV),
        in_specs=[
            pl.BlockSpec(memory_space=pltpu.SMEM),
            pl.BlockSpec((n_seq, qw), lambda b, h: (b, h)),
            pl.BlockSpec((n_seq, HEAD), lambda b, h: (b, Q_END // HEAD + h)),
            pl.BlockSpec((n_seq, HEAD), lambda b, h: (b, K_END // HEAD + h)),
            pl.BlockSpec((n_ctx, HEAD), lambda b, h: (b, h)),
            pl.BlockSpec((n_ctx, HEAD), lambda b, h: (b, N_KV + h)),
        ],
        out_specs=pl.BlockSpec((n_seq, qw), lambda b, h: (b, h)),
        scratch_shapes=[pltpu.VMEM((3, GQA * WINDOW, 3 * WINDOW), F32)],
        compiler_params=_cparams(("arbitrary", "arbitrary")),
        name="banded_gqa",
    )(sink, p_lat, p_lat, p_lat, p_ctx, p_ctx)


DFT_ROWS = 32


def _fourier_kernel(z_ref, cs_ref, ca_ref, sa_ref, cb_ref, sb_ref, o_ref, cn_ref, sn_ref):
    @pl.when(pl.program_id(0) == 0)
    def _():
        cb, sb = cb_ref[...], sb_ref[...]

        def rows(a, carry):
            ca, sa = ca_ref[a], sa_ref[a]
            r = pl.ds(pl.multiple_of(a * DFT_ROWS, DFT_ROWS), DFT_ROWS)
            cn_ref[r, :] = (ca * cb - sa * sb).astype(BF16)
            sn_ref[r, :] = (sa * cb + ca * sb).astype(BF16)
            return carry

        lax.fori_loop(0, ca_ref.shape[0], rows, 0)

    cs = cs_ref[...]
    a_parts, b_parts = [], []
    for g in range(FOUR_G):
        ab = jnp.dot(z_ref[:, g * HEAD:(g + 1) * HEAD], cs, preferred_element_type=F32)
        a_parts.append(ab[:, :HEAD])
        b_parts.append(ab[:, HEAD:])
    a = jnp.concatenate(a_parts, axis=1).astype(BF16)
    b = jnp.concatenate(b_parts, axis=1).astype(BF16)
    out = (jnp.dot(cn_ref[...], a, preferred_element_type=F32)
           - jnp.dot(sn_ref[...], b, preferred_element_type=F32))
    o_ref[...] = (out * (1.0 / (z_ref.shape[0] * HEAD) ** 0.5)).astype(BF16)


def _dft_rows(j, m):
    ang = ((j[:, None] * jnp.arange(m, dtype=jnp.int32)[None, :]) % m).astype(F32) * (2.0 * jnp.pi / m)
    return jnp.cos(ang), jnp.sin(ang)


def _fourier(p_lat, *, batch, n_seq):
    assert n_seq % DFT_ROWS == 0
    n_a = n_seq // DFT_ROWS
    cc, sc = _dft_rows(jnp.arange(HEAD, dtype=jnp.int32), HEAD)
    cs = jnp.concatenate([cc, sc], axis=1).astype(BF16)
    ca, sa = _dft_rows(jnp.arange(n_a, dtype=jnp.int32) * DFT_ROWS, n_seq)
    cb, sb = _dft_rows(jnp.arange(DFT_ROWS, dtype=jnp.int32), n_seq)
    coarse_spec = pl.BlockSpec((n_a, 1, n_seq), lambda b: (0, 0, 0))
    fine_spec = pl.BlockSpec((DFT_ROWS, n_seq), lambda b: (0, 0))
    return pl.pallas_call(
        _fourier_kernel,
        out_shape=jax.ShapeDtypeStruct((batch * n_seq, FOUR_W), BF16),
        grid=(batch,),
        in_specs=[
            pl.BlockSpec((n_seq, FOUR_W), lambda b: (b, V_END // FOUR_W)),
            pl.BlockSpec((HEAD, 2 * HEAD), lambda b: (0, 0)),
            coarse_spec, coarse_spec, fine_spec, fine_spec,
        ],
        out_specs=pl.BlockSpec((n_seq, FOUR_W), lambda b: (b, 0)),
        scratch_shapes=[pltpu.VMEM((n_seq, n_seq), BF16), pltpu.VMEM((n_seq, n_seq), BF16)],
        compiler_params=_cparams(("arbitrary",)),
        name="fourier_mix",
    )(p_lat, cs, ca.reshape(n_a, 1, n_seq), sa.reshape(n_a, 1, n_seq), cb, sb)


def _out_proj_kernel(*refs, n_parts):
    a_refs = refs[:n_parts]
    (w_ref, x_ref, g1_ref, lg_ref, lb_ref, sh_ref, sc_ref, wr_ref, x1_ref, h2_ref, lo_ref) = refs[n_parts:]
    y = None
    k0 = 0
    for a_ref in a_refs:
        kk = a_ref.shape[1]
        t = jnp.dot(a_ref[...], w_ref[k0:k0 + kk, :], preferred_element_type=F32)
        y = t if y is None else y + t
        k0 += kk
    x1 = _ln(ALPHA * x_ref[...] + g1_ref[...] * y) * lg_ref[...] + lb_ref[...]
    x1_ref[...] = x1
    h2 = (_ln(x1) * (1.0 + sc_ref[...]) + sh_ref[...]).astype(BF16)
    h2_ref[...] = h2
    lo_ref[...] = jnp.dot(h2, wr_ref[...], preferred_element_type=F32)


def _out_proj(parts, w_bf, x2d, mod5, layer, ln_g, ln_b, w_router_pad, *, n_seq):
    rows = x2d.shape[0]
    tm = 256
    tiles_per_seq = n_seq // tm
    mod_spec = lambda chunk: pl.BlockSpec((None, None, None, 1, D),
                                          lambda i: (layer, i // tiles_per_seq, chunk, 0, 0))
    vec_spec = pl.BlockSpec((1, D), lambda i: (0, 0))
    in_specs = [pl.BlockSpec((tm, p.shape[1]), lambda i: (i, 0)) for p in parts]
    in_specs += [
        pl.BlockSpec((D, D), lambda i: (0, 0)),
        pl.BlockSpec((tm, D), lambda i: (i, 0)),
        mod_spec(2), vec_spec, vec_spec, mod_spec(3), mod_spec(4),
        pl.BlockSpec((D, LANES), lambda i: (0, 0)),
    ]
    return pl.pallas_call(
        functools.partial(_out_proj_kernel, n_parts=len(parts)),
        out_shape=(jax.ShapeDtypeStruct((rows, D), F32),
                   jax.ShapeDtypeStruct((rows, D), BF16),
                   jax.ShapeDtypeStruct((rows, LANES), F32)),
        grid=(rows // tm,),
        in_specs=in_specs,
        out_specs=(pl.BlockSpec((tm, D), lambda i: (i, 0)),
                   pl.BlockSpec((tm, D), lambda i: (i, 0)),
                   pl.BlockSpec((tm, LANES), lambda i: (i, 0))),
        compiler_params=_cparams(("arbitrary",)),
        name="out_proj_ln",
    )(*parts, w_bf, x2d, mod5, ln_g.reshape(1, D), ln_b.reshape(1, D), mod5, mod5, w_router_pad)


def _route_kernel(lo_ref, slot_row_ref, slot_col_ref, gate_col_ref, off_ref, tri_ref, *, n_seq, cap):
    @pl.when(pl.program_id(0) == 0)
    def _():
        blk = 256
        r = lax.broadcasted_iota(jnp.int32, (blk, n_seq), 0)
        c = lax.broadcasted_iota(jnp.int32, (blk, n_seq), 1)
        for r0 in range(0, n_seq, blk):
            tri_ref[r0:r0 + blk, :] = jnp.where(r + r0 < c, 1.0, 0.0).astype(BF16)

    lt = jnp.transpose(lo_ref[...])[:N_EXP, :]
    ex = jnp.exp(lt - jnp.max(lt, axis=0, keepdims=True))
    aff = ex / jnp.sum(ex, axis=0, keepdims=True)
    def count_ge(t):
        return jnp.sum(jnp.where(aff >= t, 1.0, 0.0), axis=1, keepdims=True)

    def bisect_log(_, lohi):
        lo, hi = lohi
        mid = 0.5 * (lo + hi)
        ok = count_ge(jnp.exp(mid)) >= cap
        return jnp.where(ok, mid, lo), jnp.where(ok, hi, mid)

    def bisect_lin(_, lohi):
        lo, hi = lohi
        mid = 0.5 * (lo + hi)
        ok = count_ge(mid) >= cap
        return jnp.where(ok, mid, lo), jnp.where(ok, hi, mid)

    lo0 = jnp.full((N_EXP, 1), -150.0, F32)
    hi0 = jnp.full((N_EXP, 1), 1.0, F32)
    lo_l, hi_l = lax.fori_loop(0, 18, bisect_log, (lo0, hi0))
    thr, above = lax.fori_loop(0, 34, bisect_lin, (jnp.exp(lo_l), jnp.exp(hi_l)))
    gt = aff >= above
    eq = (aff >= thr) & (aff < above)
    need = cap - jnp.sum(jnp.where(gt, 1.0, 0.0), axis=1, keepdims=True)
    tri = tri_ref[...]
    eq_before = jnp.dot(jnp.where(eq, 1.0, 0.0).astype(BF16), tri, preferred_element_type=F32)
    sel = gt | (eq & (eq_before < need))
    slot = jnp.dot(jnp.where(sel, 1.0, 0.0).astype(BF16), tri, preferred_element_type=F32)
    slot = jnp.where(sel, slot, -1.0)
    gate = jnp.where(sel, aff, 0.0)
    slot_row_ref[...] = slot
    pad = jnp.full((LANES - N_EXP, n_seq), -1.0, F32)
    slot_col_ref[...] = jnp.transpose(jnp.concatenate([slot, pad], axis=0))
    gate_col_ref[...] = jnp.transpose(jnp.concatenate([gate, pad * 0.0], axis=0))
    tt = lax.broadcasted_iota(jnp.int32, (n_seq, LANES), 0)
    ti = lax.broadcasted_iota(jnp.int32, (n_seq, LANES), 1)
    before = jnp.where(tt < ti * MOE_TILE, 1.0, 0.0).astype(BF16)
    off_ref[...] = jnp.dot(jnp.where(sel, 1.0, 0.0).astype(BF16), before, preferred_element_type=F32).astype(jnp.int32)


def _route(logits, *, batch, n_seq, cap):
    return pl.pallas_call(
        functools.partial(_route_kernel, n_seq=n_seq, cap=cap),
        out_shape=(jax.ShapeDtypeStruct((batch, N_EXP, n_seq), F32),
                   jax.ShapeDtypeStruct((batch * n_seq, LANES), F32),
                   jax.ShapeDtypeStruct((batch * n_seq, LANES), F32),
                   jax.ShapeDtypeStruct((batch, N_EXP, LANES), jnp.int32)),
        grid=(batch,),
        in_specs=[pl.BlockSpec((n_seq, LANES), lambda b: (b, 0))],
        out_specs=(pl.BlockSpec((None, N_EXP, n_seq), lambda b: (b, 0, 0)),
                   pl.BlockSpec((n_seq, LANES), lambda b: (b, 0)),
                   pl.BlockSpec((n_seq, LANES), lambda b: (b, 0)),
                   pl.BlockSpec((None, N_EXP, LANES), lambda b: (b, 0, 0))),
        scratch_shapes=[pltpu.VMEM((n_seq, n_seq), BF16)],
        compiler_params=_cparams(("arbitrary",)),
        name="ec_route",
    )(logits)


def _windows(off_ref, row0, n_exp, tile, cap):
    firsts, n_win = [], 0
    for e in range(n_exp):
        lo = off_ref[row0 + e * OFF_STRIDE + tile]
        hi = off_ref[row0 + e * OFF_STRIDE + tile + 1]
        first = (lo >> 4) << 4
        firsts.append(first)
        n_win = jnp.maximum(n_win, (hi - first + MOE_WIN - 1) // MOE_WIN)
    return firsts, n_win


def _dispatch_kernel(off_ref, slot_ref, h_ref, o_ref, *, e_blk, cap):
    b, half = pl.program_id(0), pl.program_id(1)
    o_ref[...] = jnp.zeros(o_ref.shape, o_ref.dtype)
    siota = lax.broadcasted_iota(jnp.int32, (MOE_WIN, 1), 0)
    n_cols = h_ref.shape[1]
    chunk = 512

    def token_tile(tile, carry):
        firsts, n_win = _windows(off_ref, (b * N_EXP + half * e_blk) * OFF_STRIDE, e_blk, tile, cap)
        t0 = pl.multiple_of(tile * MOE_TILE, MOE_TILE)

        def window(w, carry):
            bases, parts = [], []
            for e in range(e_blk):
                start = firsts[e] + w * MOE_WIN
                base = pl.multiple_of(jnp.minimum(start, cap - MOE_WIN), 16)
                srow = slot_ref[e, tile]
                srow = jnp.where(srow >= start.astype(F32), srow, -1.0)
                parts.append(jnp.where(srow == (base + siota).astype(F32), 1.0, 0.0).astype(BF16))
                bases.append(base)
            onehot = jnp.concatenate(parts, axis=0)
            for c0 in range(0, n_cols, chunk):
                res = jnp.dot(onehot, h_ref[pl.ds(t0, MOE_TILE), c0:c0 + chunk],
                              preferred_element_type=F32).astype(BF16)
                for e in range(e_blk):
                    rows = pl.ds(bases[e], MOE_WIN)
                    o_ref[e, rows, c0:c0 + chunk] = (o_ref[e, rows, c0:c0 + chunk]
                                                     + res[e * MOE_WIN:(e + 1) * MOE_WIN, :])
            return carry

        return lax.fori_loop(0, n_win, window, carry)

    lax.fori_loop(0, slot_ref.shape[1], token_tile, 0)


def _dispatch(off, slot_row, h2, *, batch, n_seq, cap):
    e_blk = 8
    tiles = n_seq // MOE_TILE
    return pl.pallas_call(
        functools.partial(_dispatch_kernel, e_blk=e_blk, cap=cap),
        out_shape=jax.ShapeDtypeStruct((N_EXP, batch * cap, D), BF16),
        grid_spec=pltpu.PrefetchScalarGridSpec(
            num_scalar_prefetch=1,
            grid=(batch, N_EXP // e_blk),
            in_specs=[
                pl.BlockSpec((None, e_blk, tiles, 1, MOE_TILE), lambda b, j, off: (b, j, 0, 0, 0)),
                pl.BlockSpec((n_seq, D), lambda b, j, off: (b, 0)),
            ],
            out_specs=pl.BlockSpec((e_blk, cap, D), lambda b, j, off: (j, b, 0)),
        ),
        compiler_params=_cparams(("arbitrary", "arbitrary")),
        name="ec_dispatch",
    )(off, slot_row.reshape(batch, N_EXP, tiles, 1, MOE_TILE), h2)


def _gate_up_kernel(x_ref, wg_ref, wu_ref, o_ref):
    x = x_ref[...]
    g = jnp.dot(x, wg_ref[...].astype(BF16), preferred_element_type=F32)
    u = jnp.dot(x, wu_ref[...].astype(BF16), preferred_element_type=F32)
    o_ref[...] = (jax.nn.silu(g) * u).astype(BF16)


def _down_kernel(h_ref, wd_ref, o_ref):
    y = jnp.dot(h_ref[...], wd_ref[...].astype(BF16), preferred_element_type=F32).astype(BF16)
    cap = o_ref.shape[1]
    for b in range(o_ref.shape[0]):
        o_ref[b] = y[b * cap:(b + 1) * cap, :]


def _experts(xs, w_gate, w_up, w_down, layer, *, batch, cap):
    m = batch * cap
    tn, tn_down = TN_UP, TN_DOWN
    hid = pl.pallas_call(
        _gate_up_kernel,
        out_shape=jax.ShapeDtypeStruct((N_EXP, m, EXP_FF), BF16),
        grid=(N_EXP, EXP_FF // tn),
        in_specs=[
            pl.BlockSpec((None, m, D), lambda e, j: (e, 0, 0)),
            pl.BlockSpec((None, None, D, tn), lambda e, j: (layer, e, 0, j)),
            pl.BlockSpec((None, None, D, tn), lambda e, j: (layer, e, 0, j)),
        ],
        out_specs=pl.BlockSpec((None, m, tn), lambda e, j: (e, 0, j)),
        compiler_params=_cparams(("arbitrary", "arbitrary")),
        name="expert_gate_up",
    )(xs, w_gate, w_up)
    return pl.pallas_call(
        _down_kernel,
        out_shape=jax.ShapeDtypeStruct((batch, N_EXP * cap, D), BF16),
        grid=(N_EXP, D // tn_down),
        in_specs=[
            pl.BlockSpec((None, m, EXP_FF), lambda e, j: (e, 0, 0)),
            pl.BlockSpec((None, None, EXP_FF, tn_down), lambda e, j: (layer, e, 0, j)),
        ],
        out_specs=pl.BlockSpec((batch, cap, tn_down), lambda e, j: (0, e, j)),
        compiler_params=_cparams(("arbitrary", "arbitrary")),
        name="expert_down",
    )(hid, w_down)


def _combine_kernel(off_ref, ye_ref, slot_ref, gate_ref, x_ref, g2_ref, lg_ref, lb_ref, o_ref, *, cap):
    b, tile = pl.program_id(0), pl.program_id(1)
    firsts, n_win = _windows(off_ref, b * N_EXP * OFF_STRIDE, N_EXP, tile, cap)
    lane_e = lax.broadcasted_iota(jnp.int32, (LANES, N_EXP * MOE_WIN), 1) // MOE_WIN
    spread = jnp.where(lax.broadcasted_iota(jnp.int32, (LANES, N_EXP * MOE_WIN), 0) == lane_e, 1.0, 0.0).astype(BF16)
    slot = jnp.dot(slot_ref[...].astype(BF16), spread, preferred_element_type=F32)
    gate = jnp.dot(gate_ref[...].astype(BF16), spread, preferred_element_type=F32).astype(BF16)
    siota = lax.broadcasted_iota(jnp.int32, (1, MOE_WIN), 1)

    def window(w, acc):
        starts, targets, rows = [], [], []
        for e in range(N_EXP):
            start = firsts[e] + w * MOE_WIN
            base = pl.multiple_of(jnp.minimum(start, cap - MOE_WIN), 16)
            starts.append(jnp.full((1, MOE_WIN), start, jnp.int32))
            targets.append(base + siota)
            rows.append(ye_ref[pl.ds(e * cap + base, MOE_WIN), :])
        start_l = jnp.concatenate(starts, axis=1).astype(F32)
        target_l = jnp.concatenate(targets, axis=1).astype(F32)
        comb = jnp.where((slot == target_l) & (slot >= start_l), gate, jnp.zeros_like(gate))
        return acc + jnp.dot(comb, jnp.concatenate(rows, axis=0), preferred_element_type=F32)

    moe = lax.fori_loop(0, n_win, window, jnp.zeros(x_ref.shape, F32))
    o_ref[...] = _ln(ALPHA * x_ref[...] + g2_ref[...] * moe) * lg_ref[...] + lb_ref[...]


def _combine(off, ye, slot_col, gate_col, x1, mod5, layer, ln_g, ln_b, *, batch, n_seq, cap):
    tm = MOE_TILE
    tiles = n_seq // tm
    vec_spec = pl.BlockSpec((1, D), lambda b, i, off: (0, 0))
    row_spec = lambda w: pl.BlockSpec((tm, w), lambda b, i, off: (b * tiles + i, 0))
    return pl.pallas_call(
        functools.partial(_combine_kernel, cap=cap),
        out_shape=jax.ShapeDtypeStruct((batch * n_seq, D), F32),
        grid_spec=pltpu.PrefetchScalarGridSpec(
            num_scalar_prefetch=1,
            grid=(batch, tiles),
            in_specs=[
                pl.BlockSpec((None, N_EXP * cap, D), lambda b, i, off: (b, 0, 0)),
                row_spec(LANES), row_spec(LANES), row_spec(D),
                pl.BlockSpec((None, None, None, 1, D), lambda b, i, off: (layer, b, 5, 0, 0)),
                vec_spec, vec_spec,
            ],
            out_specs=row_spec(D),
        ),
        compiler_params=_cparams(("arbitrary", "arbitrary")),
        name="ec_combine_ln",
    )(off, ye, slot_col, gate_col, x1, mod5, ln_g.reshape(1, D), ln_b.reshape(1, D))


def _moe_block(x1, h2, logits, mod5, layer, w_gate, w_up, w_down, ln_g, ln_b, *, batch, n_seq):
    cap = 2 * n_seq // N_EXP
    assert n_seq % MOE_TILE == 0 and n_seq // MOE_TILE < OFF_STRIDE and cap % MOE_WIN == 0
    assert cap <= 256
    slot_row, slot_col, gate_col, off = _route(logits, batch=batch, n_seq=n_seq, cap=cap)
    off = off[:, :, :OFF_STRIDE].reshape(batch * N_EXP * OFF_STRIDE)
    xs = _dispatch(off, slot_row, h2, batch=batch, n_seq=n_seq, cap=cap)
    ye = _experts(xs, w_gate, w_up, w_down, layer, batch=batch, cap=cap)
    return _combine(off, ye, slot_col, gate_col, x1, mod5, layer, ln_g, ln_b, batch=batch, n_seq=n_seq, cap=cap)


def _cd_in_kernel(x_ref, sh_ref, sc_ref, w_ref, lg_ref, lb_ref, u_ref, vg_ref, xg_ref):
    h = (_ln(x_ref[...]) * (1.0 + sc_ref[...]) + sh_ref[...]).astype(BF16)
    chunk = 512
    for c0 in range(0, SG_W, chunk):
        u_ref[:, c0:c0 + chunk] = jax.nn.gelu(
            jnp.dot(h, w_ref[:, c0:c0 + chunk], preferred_element_type=F32)).astype(BF16)
    for c0 in range(0, SG_W, chunk):
        z = jax.nn.gelu(jnp.dot(h, w_ref[:, SG_W + c0:SG_W + c0 + chunk], preferred_element_type=F32))
        parts = [_ln(z[:, j * HEAD:(j + 1) * HEAD]) for j in range(chunk // HEAD)]
        vg = jnp.concatenate(parts, axis=1) * lg_ref[:, c0:c0 + chunk] + lb_ref[:, c0:c0 + chunk]
        vg_ref[:, c0:c0 + chunk] = vg.astype(BF16)
    for c0 in range(0, CONV_CH, chunk):
        a = jnp.dot(h, w_ref[:, 2 * SG_W + c0:2 * SG_W + c0 + chunk], preferred_element_type=F32)
        gt = jnp.dot(h, w_ref[:, 2 * SG_W + CONV_CH + c0:2 * SG_W + CONV_CH + c0 + chunk],
                     preferred_element_type=F32)
        xg_ref[:, c0:c0 + chunk] = a * jax.nn.sigmoid(gt)


def _cd_in_proj(x2d, mod5, layer, w_bf, sg_ln_g, sg_ln_b, *, n_seq):
    rows = x2d.shape[0]
    tm = 256
    tiles_per_seq = n_seq // tm
    mod_spec = lambda chunk: pl.BlockSpec((None, None, None, 1, D),
                                          lambda i: (layer, i // tiles_per_seq, chunk, 0, 0))
    vec_spec = pl.BlockSpec((1, SG_W), lambda i: (0, 0))
    row_spec = lambda w: pl.BlockSpec((tm, w), lambda i: (i, 0))
    return pl.pallas_call(
        _cd_in_kernel,
        out_shape=(jax.ShapeDtypeStruct((rows, SG_W), BF16),
                   jax.ShapeDtypeStruct((rows, SG_W), BF16),
                   jax.ShapeDtypeStruct((rows, CONV_CH), F32)),
        grid=(rows // tm,),
        in_specs=[row_spec(D), mod_spec(0), mod_spec(1),
                  pl.BlockSpec((D, CD_IN), lambda i: (0, 0)), vec_spec, vec_spec],
        out_specs=(row_spec(SG_W), row_spec(SG_W), row_spec(CONV_CH)),
        compiler_params=_cparams(("arbitrary",)),
        name="cd_in_proj",
    )(x2d, mod5, mod5, w_bf, sg_ln_g.reshape(1, SG_W), sg_ln_b.reshape(1, SG_W))


CONV_HALO = 16
CONV_ROWS = 32


def _cd_mix_kernel(u_ref, vg_ref, xp_ref, xc_ref, xn_ref, sgw_ref, sgb_ref, cw_ref, cb_ref, lg_ref, lb_ref,
                   o_ref, sh_ref, cv_ref, *, tiles_per_seq):
    i = pl.program_id(1)
    tm = xc_ref.shape[0]
    for c in range(tm // CHUNK):
        for g in range(SG_G):
            rs, cs = slice(c * CHUNK, (c + 1) * CHUNK), slice(g * HEAD, (g + 1) * HEAD)
            sp = jnp.dot(sgw_ref[g], vg_ref[rs, cs], preferred_element_type=F32) + sgb_ref[g]
            o_ref[rs, cs] = (u_ref[rs, cs].astype(F32) * sp).astype(BF16)
    sh_ref[0, 0:CONV_HALO, :] = jnp.where(i > 0, xp_ref[...], 0.0)
    sh_ref[0, CONV_HALO:CONV_HALO + tm, :] = xc_ref[...]
    sh_ref[0, CONV_HALO + tm:, :] = jnp.where(i < tiles_per_seq - 1, xn_ref[...], 0.0)
    keep = tm + 2 * CONV_HALO - SUBLANES
    for s in range(1, SUBLANES):
        sh_ref[s, 0:keep, :] = sh_ref[0, s:s + keep, :]
    off = CONV_HALO - CONV_K // 2
    for r0 in range(0, tm, CONV_ROWS):
        for c0 in range(0, CONV_CH, LANES):
            acc = jnp.zeros((CONV_ROWS, LANES), F32) + cb_ref[:, c0:c0 + LANES]
            for k in range(CONV_K):
                a = r0 + off + k
                acc = acc + (sh_ref[a % SUBLANES, a - a % SUBLANES:a - a % SUBLANES + CONV_ROWS, c0:c0 + LANES]
                             * cw_ref[k:k + 1, c0:c0 + LANES])
            cv_ref[r0:r0 + CONV_ROWS, c0:c0 + LANES] = acc
    y = _ln(cv_ref[...]) * lg_ref[...] + lb_ref[...]
    o_ref[:, SG_W:] = jax.nn.silu(y).astype(BF16)


def _cd_mix(u, vg, xg, sg_w, sg_b, conv_w, conv_b, conv_ln_g, conv_ln_b, *, batch, n_seq):
    tm = 128
    tiles = n_seq // tm
    hpt = tm // CONV_HALO
    n_halo = batch * n_seq // CONV_HALO
    row_spec = lambda w: pl.BlockSpec((tm, w), lambda b, i: (b * tiles + i, 0))
    vec_spec = pl.BlockSpec((1, CONV_CH), lambda b, i: (0, 0))
    sgb_full = jnp.broadcast_to(sg_b[:, :, None], (SG_G, CHUNK, HEAD))
    return pl.pallas_call(
        functools.partial(_cd_mix_kernel, tiles_per_seq=tiles),
        out_shape=jax.ShapeDtypeStruct((batch * n_seq, D), BF16),
        grid=(batch, tiles),
        in_specs=[
            row_spec(SG_W), row_spec(SG_W),
            pl.BlockSpec((CONV_HALO, CONV_CH), lambda b, i: (jnp.maximum((b * tiles + i) * hpt - 1, 0), 0)),
            row_spec(CONV_CH),
            pl.BlockSpec((CONV_HALO, CONV_CH), lambda b, i: (jnp.minimum((b * tiles + i + 1) * hpt, n_halo - 1), 0)),
            pl.BlockSpec((SG_G, CHUNK, CHUNK), lambda b, i: (0, 0, 0)),
            pl.BlockSpec((SG_G, CHUNK, HEAD), lambda b, i: (0, 0, 0)),
            pl.BlockSpec((CONV_K + 1, CONV_CH), lambda b, i: (0, 0)),
            vec_spec, vec_spec, vec_spec,
        ],
        out_specs=pl.BlockSpec((tm, D), lambda b, i: (b * tiles + i, 0)),
        scratch_shapes=[pltpu.VMEM((SUBLANES, tm + 2 * CONV_HALO, CONV_CH), F32), pltpu.VMEM((tm, CONV_CH), F32)],
        compiler_params=_cparams(("arbitrary", "arbitrary")),
        name="cd_mix",
    )(u, vg, xg, xg, xg, sg_w.astype(BF16), sgb_full,
      jnp.pad(conv_w.reshape(CONV_K, CONV_CH), ((0, 1), (0, 0))),
      conv_b.reshape(1, CONV_CH), conv_ln_g.reshape(1, CONV_CH), conv_ln_b.reshape(1, CONV_CH))


def kernel(x, c, ctx, c_ctx, w_mod, b_mod, ln1_g, ln1_b, ln2_g, ln2_b, w_router, w_gate, w_up, w_down,
           ab_w_in, ab_w_out, sink, cd_w_in, cd_w_out, sg_ln_g, sg_ln_b, sg_w, sg_b,
           conv_w, conv_b, conv_ln_g, conv_ln_b):
    batch, n_seq, _ = x.shape
    n_ctx = ctx.shape[1]
    assert x.shape[2] == D and batch + 1 <= MOD_ROWS

    c_pad = jnp.concatenate([c, c_ctx[None, :], jnp.zeros((MOD_ROWS - batch - 1, D), F32)], axis=0)
    mod = _modulation(c_pad, w_mod, b_mod)
    mod5 = mod.reshape(DEPTH, MOD_ROWS, 6, 1, D)
    w_router_pad = jnp.pad(w_router, ((0, 0), (0, 0), (0, LANES - N_EXP))).astype(BF16)

    x0 = x.reshape(batch * n_seq, D)
    ctx2d = ctx.reshape(batch * n_ctx, D)

    w_in = ab_w_in[0].astype(BF16)
    tabs = _rope_tables(n_seq)
    p_lat = _ab_in_proj(x0, mod5, 0, lambda b: b, w_in, tabs, n_seq=n_seq,
                        rope_cols=K_END, q_cols=Q_END, name="ab_in_proj")
    p_ctx = _ab_in_proj(ctx2d, mod5, 0, lambda b: batch, w_in[:, Q_END:V_END], tabs, n_seq=n_ctx,
                        rope_cols=0, q_cols=0, name="ab_ctx_proj")
    attn = _attention(p_lat, p_ctx, sink[0], batch=batch, n_seq=n_seq, n_ctx=n_ctx)
    four = _fourier(p_lat, batch=batch, n_seq=n_seq)
    x1, h2, logits = _out_proj([attn, four], ab_w_out[0].astype(BF16), x0, mod5, 0, ln1_g[0], ln1_b[0],
                               w_router_pad[0], n_seq=n_seq)
    x2 = _moe_block(x1, h2, logits, mod5, 0, w_gate, w_up, w_down, ln2_g[0], ln2_b[0], batch=batch, n_seq=n_seq)

    u, vg, xg = _cd_in_proj(x2, mod5, 1, cd_w_in[0].astype(BF16), sg_ln_g[0], sg_ln_b[0], n_seq=n_seq)
    mix = _cd_mix(u, vg, xg, sg_w[0], sg_b[0], conv_w[0], conv_b[0], conv_ln_g[0], conv_ln_b[0],
                  batch=batch, n_seq=n_seq)
    x3, h4, logits1 = _out_proj([mix], cd_w_out[0].astype(BF16), x2, mod5, 1, ln1_g[1], ln1_b[1],
                                w_router_pad[1], n_seq=n_seq)
    x4 = _moe_block(x3, h4, logits1, mod5, 1, w_gate, w_up, w_down, ln2_g[1], ln2_b[1], batch=batch, n_seq=n_seq)
    return x4.reshape(batch, n_seq, D)
```

```python
import functools

import jax
import jax.numpy as jnp
from jax import lax
from jax.experimental import pallas as pl
from jax.experimental.pallas import tpu as pltpu

F32 = jnp.float32
BF16 = jnp.bfloat16

D = 2048
HEAD = 128
N_Q = 12
N_KV = 4
GQA = 3
WINDOW = 128
GRID_W = 64
ROPE_BASE = 10000.0
Q_END = N_Q * HEAD
KV_W = N_KV * HEAD
K_END = Q_END + KV_W
V_END = K_END + KV_W
FOUR_G = 4
FOUR_W = FOUR_G * HEAD
AB_IN = V_END + FOUR_W
SG_G = 8
SG_W = SG_G * HEAD
CHUNK = 128
CONV_CH = 1024
CONV_K = 31
CD_IN = 4096
N_EXP = 16
EXP_FF = 2048
DEPTH = 2
ALPHA = (2 * DEPTH) ** 0.25
LN_EPS = 1e-6
ATTN_SCALE = HEAD ** -0.5
MOD_ROWS = 16
LANES = 128
SUBLANES = 8
VMEM_LIMIT = 56 * 1024 * 1024
MOE_TILE = 256
MOE_WIN = 64
OFF_STRIDE = 16
TN_UP = 512
TN_DOWN = 512


def _cparams(sem):
    return pltpu.CompilerParams(dimension_semantics=sem, vmem_limit_bytes=VMEM_LIMIT)


def _ln(x):
    mu = jnp.mean(x, axis=-1, keepdims=True)
    xc = x - mu
    var = jnp.mean(xc * xc, axis=-1, keepdims=True)
    return xc * lax.rsqrt(var + LN_EPS)


def _mod_kernel(c_ref, w_ref, b_ref, o_ref):
    s = jax.nn.silu(c_ref[...]).astype(BF16)
    o_ref[...] = jnp.dot(s, w_ref[...].astype(BF16), preferred_element_type=F32) + b_ref[...]


def _modulation(c_pad, w_mod, b_mod):
    tn = 1024
    return pl.pallas_call(
        _mod_kernel,
        out_shape=jax.ShapeDtypeStruct((DEPTH, MOD_ROWS, 6 * D), F32),
        grid=(DEPTH, 6 * D // tn),
        in_specs=[
            pl.BlockSpec((MOD_ROWS, D), lambda l, j: (0, 0)),
            pl.BlockSpec((None, D, tn), lambda l, j: (l, 0, j)),
            pl.BlockSpec((None, 1, tn), lambda l, j: (l, 0, j)),
        ],
        out_specs=pl.BlockSpec((None, MOD_ROWS, tn), lambda l, j: (l, 0, j)),
        compiler_params=_cparams(("arbitrary", "arbitrary")),
        name="modulation",
    )(c_pad, w_mod, b_mod.reshape(DEPTH, 1, 6 * D))


def _rope(a, cos, sa, sb):
    return a * cos + pltpu.roll(a, HEAD - 32, 1) * sa + pltpu.roll(a, 32, 1) * sb


def _ab_in_kernel(x_ref, sh_ref, sc_ref, w_ref, cos_ref, sa_ref, sb_ref, o_ref, *, n_cols, rope_cols, q_cols):
    h = (_ln(x_ref[...]) * (1.0 + sc_ref[...]) + sh_ref[...]).astype(BF16)
    chunk = 512
    for c0 in range(0, n_cols, chunk):
        acc = jnp.dot(h, w_ref[:, c0:c0 + chunk], preferred_element_type=F32)
        if c0 < rope_cols:
            cos, sa, sb = cos_ref[...], sa_ref[...], sb_ref[...]
            parts = []
            for j in range(chunk // HEAD):
                r = _rope(acc[:, j * HEAD:(j + 1) * HEAD], cos, sa, sb)
                parts.append(r * ATTN_SCALE if c0 < q_cols else r)
            acc = jnp.concatenate(parts, axis=1)
        o_ref[:, c0:c0 + chunk] = acc.astype(BF16)


def _ab_in_proj(x2d, mod5, layer, row_of_tile, w_bf, rope_tabs, *, n_seq, rope_cols, q_cols, name):
    rows, _ = x2d.shape
    n_cols = w_bf.shape[1]
    tm = 256
    tiles_per_seq = n_seq // tm
    cos, sa, sb = rope_tabs
    kern = functools.partial(_ab_in_kernel, n_cols=n_cols, rope_cols=rope_cols, q_cols=q_cols)
    mod_spec = lambda chunk: pl.BlockSpec((None, None, None, 1, D),
                                          lambda i: (layer, row_of_tile(i // tiles_per_seq), chunk, 0, 0))
    tab_spec = pl.BlockSpec((tm, HEAD), lambda i: (i % tiles_per_seq, 0))
    return pl.pallas_call(
        kern,
        out_shape=jax.ShapeDtypeStruct((rows, n_cols), BF16),
        grid=(rows // tm,),
        in_specs=[
            pl.BlockSpec((tm, D), lambda i: (i, 0)),
            mod_spec(0), mod_spec(1),
            pl.BlockSpec((D, n_cols), lambda i: (0, 0)),
            tab_spec, tab_spec, tab_spec,
        ],
        out_specs=pl.BlockSpec((tm, n_cols), lambda i: (i, 0)),
        compiler_params=_cparams(("arbitrary",)),
        name=name,
    )(x2d, mod5, mod5, w_bf, cos, sa, sb)


def _rope_tables(n):
    quarter = HEAD // 4
    t = jnp.arange(n)
    r = (t // GRID_W).astype(F32)
    col = (t % GRID_W).astype(F32)
    inv = ROPE_BASE ** (-jnp.arange(quarter, dtype=F32) / quarter)
    ang_r, ang_c = r[:, None] * inv, col[:, None] * inv
    zero = jnp.zeros_like(ang_r)
    cos = jnp.concatenate([jnp.cos(ang_r)] * 2 + [jnp.cos(ang_c)] * 2, axis=1)
    sa = jnp.concatenate([-jnp.sin(ang_r), zero, -jnp.sin(ang_c), zero], axis=1)
    sb = jnp.concatenate([zero, jnp.sin(ang_r), zero, jnp.sin(ang_c)], axis=1)
    return cos, sa, sb


def _attn_kernel(sink_ref, q_ref, k_ref, v_ref, kc_ref, vc_ref, o_ref, bias_ref, *, n_seq):
    hk = pl.program_id(1)
    kc = kc_ref[...]
    vc = vc_ref[...]
    kw_len = 3 * WINDOW
    rows = GQA * WINDOW
    n_blocks = n_seq // WINDOW
    row = lax.broadcasted_iota(jnp.int32, (rows, 1), 0)
    sink = jnp.where(row < WINDOW, sink_ref[hk * GQA],
                     jnp.where(row < 2 * WINDOW, sink_ref[hk * GQA + 1], sink_ref[hk * GQA + 2]))
    rel = (row & (WINDOW - 1)) - lax.broadcasted_iota(jnp.int32, (1, kw_len), 1)
    for i in range(3):
        bias_ref[i] = jnp.where(jnp.abs(rel + i * WINDOW) <= WINDOW, 0.0, -1e30)
    dn = (((1,), (1,)), ((), ()))

    def block(n, back):
        r0 = n * WINDOW if isinstance(n, int) else pl.multiple_of(n * WINDOW, WINDOW)
        start = r0 - back * WINDOW
        if not isinstance(start, int):
            start = pl.multiple_of(start, WINDOW)
        qs = q_ref[pl.ds(r0, WINDOW), :]
        q3 = jnp.concatenate([qs[:, g * HEAD:(g + 1) * HEAD] for g in range(GQA)], axis=0)
        kw = k_ref[pl.ds(start, kw_len), :]
        vw = v_ref[pl.ds(start, kw_len), :]
        s_w = lax.dot_general(q3, kw, dn, preferred_element_type=F32) + bias_ref[back]
        s_c = lax.dot_general(q3, kc, dn, preferred_element_type=F32)
        m = jnp.maximum(jnp.maximum(jnp.max(s_w, axis=1, keepdims=True), jnp.max(s_c, axis=1, keepdims=True)), sink)
        p_w = jnp.exp(s_w - m)
        p_c = jnp.exp(s_c - m)
        den = jnp.sum(p_w, axis=1, keepdims=True) + jnp.sum(p_c, axis=1, keepdims=True) + jnp.exp(sink - m)
        o = (jnp.dot(p_w.astype(BF16), vw, preferred_element_type=F32)
             + jnp.dot(p_c.astype(BF16), vc, preferred_element_type=F32)) / den
        o_ref[pl.ds(r0, WINDOW), :] = jnp.concatenate(
            [o[g * WINDOW:(g + 1) * WINDOW, :] for g in range(GQA)], axis=1).astype(BF16)

    block(0, 0)
    block(n_blocks - 1, 2)

    per_iter = max(g for g in (7, 2, 1) if (n_blocks - 2) % g == 0)

    def interior(i, carry):
        for g in range(per_iter):
            block(1 + per_iter * i + g, 1)
        return carry

    lax.fori_loop(0, (n_blocks - 2) // per_iter, interior, 0)


def _attention(p_lat, p_ctx, sink, *, batch, n_seq, n_ctx):
    qw = GQA * HEAD
    assert n_seq % WINDOW == 0 and n_seq >= 3 * WINDOW
    return pl.pallas_call(
        functools.partial(_attn_kernel, n_seq=n_seq),
        out_shape=jax.ShapeDtypeStruct((batch * n_seq, Q_END), BF16),
        grid=(batch, N_KV),
        in_specs=[
            pl.BlockSpec(memory_space=pltpu.SMEM),
            pl.BlockSpec((n_seq, qw), lambda b, h: (b, h)),
            pl.BlockSpec((n_seq, HEAD), lambda b, h: (b, Q_END // HEAD + h)),
            pl.BlockSpec((n_seq, HEAD), lambda b, h: (b, K_END // HEAD + h)),
            pl.BlockSpec((n_ctx, HEAD), lambda b, h: (b, h)),
            pl.BlockSpec((n_ctx, HEAD), lambda b, h: (b, N_KV + h)),
        ],
        out_specs=pl.BlockSpec((n_seq, qw), lambda b, h: (b, h)),
        scratch_shapes=[pltpu.VMEM((3, GQA * WINDOW, 3 * WINDOW), F32)],
        compiler_params=_cparams(("arbitrary", "arbitrary")),
        name="banded_gqa",
    )(sink, p_lat, p_lat, p_lat, p_ctx, p_ctx)


DFT_ROWS = 32


def _fourier_kernel(z_ref, cs_ref, ca_ref, sa_ref, cb_ref, sb_ref, o_ref, cn_ref, sn_ref):
    @pl.when(pl.program_id(0) == 0)
    def _():
        cb, sb = cb_ref[...], sb_ref[...]

        def rows(a, carry):
            ca, sa = ca_ref[a], sa_ref[a]
            r = pl.ds(pl.multiple_of(a * DFT_ROWS, DFT_ROWS), DFT_ROWS)
            cn_ref[r, :] = (ca * cb - sa * sb).astype(BF16)
            sn_ref[r, :] = (sa * cb + ca * sb).astype(BF16)
            return carry

        lax.fori_loop(0, ca_ref.shape[0], rows, 0)

    cs = cs_ref[...]
    a_parts, b_parts = [], []
    for g in range(FOUR_G):
        ab = jnp.dot(z_ref[:, g * HEAD:(g + 1) * HEAD], cs, preferred_element_type=F32)
        a_parts.append(ab[:, :HEAD])
        b_parts.append(ab[:, HEAD:])
    a = jnp.concatenate(a_parts, axis=1).astype(BF16)
    b = jnp.concatenate(b_parts, axis=1).astype(BF16)
    out = (jnp.dot(cn_ref[...], a, preferred_element_type=F32)
           - jnp.dot(sn_ref[...], b, preferred_element_type=F32))
    o_ref[...] = (out * (1.0 / (z_ref.shape[0] * HEAD) ** 0.5)).astype(BF16)


def _dft_rows(j, m):
    ang = ((j[:, None] * jnp.arange(m, dtype=jnp.int32)[None, :]) % m).astype(F32) * (2.0 * jnp.pi / m)
    return jnp.cos(ang), jnp.sin(ang)


def _fourier(p_lat, *, batch, n_seq):
    assert n_seq % DFT_ROWS == 0
    n_a = n_seq // DFT_ROWS
    cc, sc = _dft_rows(jnp.arange(HEAD, dtype=jnp.int32), HEAD)
    cs = jnp.concatenate([cc, sc], axis=1).astype(BF16)
    ca, sa = _dft_rows(jnp.arange(n_a, dtype=jnp.int32) * DFT_ROWS, n_seq)
    cb, sb = _dft_rows(jnp.arange(DFT_ROWS, dtype=jnp.int32), n_seq)
    coarse_spec = pl.BlockSpec((n_a, 1, n_seq), lambda b: (0, 0, 0))
    fine_spec = pl.BlockSpec((DFT_ROWS, n_seq), lambda b: (0, 0))
    return pl.pallas_call(
        _fourier_kernel,
        out_shape=jax.ShapeDtypeStruct((batch * n_seq, FOUR_W), BF16),
        grid=(batch,),
        in_specs=[
            pl.BlockSpec((n_seq, FOUR_W), lambda b: (b, V_END // FOUR_W)),
            pl.BlockSpec((HEAD, 2 * HEAD), lambda b: (0, 0)),
            coarse_spec, coarse_spec, fine_spec, fine_spec,
        ],
        out_specs=pl.BlockSpec((n_seq, FOUR_W), lambda b: (b, 0)),
        scratch_shapes=[pltpu.VMEM((n_seq, n_seq), BF16), pltpu.VMEM((n_seq, n_seq), BF16)],
        compiler_params=_cparams(("arbitrary",)),
        name="fourier_mix",
    )(p_lat, cs, ca.reshape(n_a, 1, n_seq), sa.reshape(n_a, 1, n_seq), cb, sb)


def _out_proj_kernel(*refs, n_parts):
    a_refs = refs[:n_parts]
    (w_ref, x_ref, g1_ref, lg_ref, lb_ref, sh_ref, sc_ref, wr_ref, x1_ref, h2_ref, lo_ref) = refs[n_parts:]
    y = None
    k0 = 0
    for a_ref in a_refs:
        kk = a_ref.shape[1]
        t = jnp.dot(a_ref[...], w_ref[k0:k0 + kk, :], preferred_element_type=F32)
        y = t if y is None else y + t
        k0 += kk
    x1 = _ln(ALPHA * x_ref[...] + g1_ref[...] * y) * lg_ref[...] + lb_ref[...]
    x1_ref[...] = x1
    h2 = (_ln(x1) * (1.0 + sc_ref[...]) + sh_ref[...]).astype(BF16)
    h2_ref[...] = h2
    lo_ref[...] = jnp.dot(h2, wr_ref[...], preferred_element_type=F32)


def _out_proj(parts, w_bf, x2d, mod5, layer, ln_g, ln_b, w_router_pad, *, n_seq):
    rows = x2d.shape[0]
    tm = 256
    tiles_per_seq = n_seq // tm
    mod_spec = lambda chunk: pl.BlockSpec((None, None, None, 1, D),
                                          lambda i: (layer, i // tiles_per_seq, chunk, 0, 0))
    vec_spec = pl.BlockSpec((1, D), lambda i: (0, 0))
    in_specs = [pl.BlockSpec((tm, p.shape[1]), lambda i: (i, 0)) for p in parts]
    in_specs += [
        pl.BlockSpec((D, D), lambda i: (0, 0)),
        pl.BlockSpec((tm, D), lambda i: (i, 0)),
        mod_spec(2), vec_spec, vec_spec, mod_spec(3), mod_spec(4),
        pl.BlockSpec((D, LANES), lambda i: (0, 0)),
    ]
    return pl.pallas_call(
        functools.partial(_out_proj_kernel, n_parts=len(parts)),
        out_shape=(jax.ShapeDtypeStruct((rows, D), F32),
                   jax.ShapeDtypeStruct((rows, D), BF16),
                   jax.ShapeDtypeStruct((rows, LANES), F32)),
        grid=(rows // tm,),
        in_specs=in_specs,
        out_specs=(pl.BlockSpec((tm, D), lambda i: (i, 0)),
                   pl.BlockSpec((tm, D), lambda i: (i, 0)),
                   pl.BlockSpec((tm, LANES), lambda i: (i, 0))),
        compiler_params=_cparams(("arbitrary",)),
        name="out_proj_ln",
    )(*parts, w_bf, x2d, mod5, ln_g.reshape(1, D), ln_b.reshape(1, D), mod5, mod5, w_router_pad)


def _route_kernel(lo_ref, slot_row_ref, slot_col_ref, gate_col_ref, off_ref, tri_ref, *, n_seq, cap, n_b):
    @pl.when(pl.program_id(0) == 0)
    def _():
        blk = 256
        r = lax.broadcasted_iota(jnp.int32, (blk, n_seq), 0)
        c = lax.broadcasted_iota(jnp.int32, (blk, n_seq), 1)
        for r0 in range(0, n_seq, blk):
            tri_ref[r0:r0 + blk, :] = jnp.where(r + r0 < c, 1.0, 0.0).astype(BF16)

    affs = []
    for b in range(n_b):
        lt = jnp.transpose(lo_ref[b * n_seq:(b + 1) * n_seq, :])[:N_EXP, :]
        ex = jnp.exp(lt - jnp.max(lt, axis=0, keepdims=True))
        affs.append(ex / jnp.sum(ex, axis=0, keepdims=True))
    aff = jnp.concatenate(affs, axis=0)
    n_rows = n_b * N_EXP

    def count_ge(t):
        return jnp.sum(jnp.where(aff >= t, 1.0, 0.0), axis=1, keepdims=True)

    def bisect_log(_, lohi):
        lo, hi = lohi
        mid = 0.5 * (lo + hi)
        ok = count_ge(jnp.exp(mid)) >= cap
        return jnp.where(ok, mid, lo), jnp.where(ok, hi, mid)

    def bisect_lin(_, lohi):
        lo, hi = lohi
        mid = 0.5 * (lo + hi)
        ok = count_ge(mid) >= cap
        return jnp.where(ok, mid, lo), jnp.where(ok, hi, mid)

    lo0 = jnp.full((n_rows, 1), -150.0, F32)
    hi0 = jnp.full((n_rows, 1), 1.0, F32)
    lo_l, hi_l = lax.fori_loop(0, 18, bisect_log, (lo0, hi0))
    thr, above = lax.fori_loop(0, 34, bisect_lin, (jnp.exp(lo_l), jnp.exp(hi_l)))
    gt = aff >= above
    eq = (aff >= thr) & (aff < above)
    need = cap - jnp.sum(jnp.where(gt, 1.0, 0.0), axis=1, keepdims=True)
    tri = tri_ref[...]
    eq_before = jnp.dot(jnp.where(eq, 1.0, 0.0).astype(BF16), tri, preferred_element_type=F32)
    sel = gt | (eq & (eq_before < need))
    sel_bf = jnp.where(sel, 1.0, 0.0).astype(BF16)
    slot = jnp.where(sel, jnp.dot(sel_bf, tri, preferred_element_type=F32), -1.0)
    gate = jnp.where(sel, aff, 0.0)
    tt = lax.broadcasted_iota(jnp.int32, (n_seq, LANES), 0)
    ti = lax.broadcasted_iota(jnp.int32, (n_seq, LANES), 1)
    before = jnp.where(tt < ti * MOE_TILE, 1.0, 0.0).astype(BF16)
    off = jnp.dot(sel_bf, before, preferred_element_type=F32).astype(jnp.int32)
    pad = jnp.full((LANES - N_EXP, n_seq), -1.0, F32)
    for b in range(n_b):
        rows = slice(b * N_EXP, (b + 1) * N_EXP)
        slot_row_ref[b] = slot[rows, :]
        off_ref[b] = off[rows, :]
        slot_col_ref[b * n_seq:(b + 1) * n_seq, :] = jnp.transpose(jnp.concatenate([slot[rows, :], pad], axis=0))
        gate_col_ref[b * n_seq:(b + 1) * n_seq, :] = jnp.transpose(jnp.concatenate([gate[rows, :], pad * 0.0], axis=0))


def _route(logits, *, batch, n_seq, cap):
    n_b = 4 if batch % 4 == 0 else 1
    return pl.pallas_call(
        functools.partial(_route_kernel, n_seq=n_seq, cap=cap, n_b=n_b),
        out_shape=(jax.ShapeDtypeStruct((batch, N_EXP, n_seq), F32),
                   jax.ShapeDtypeStruct((batch * n_seq, LANES), F32),
                   jax.ShapeDtypeStruct((batch * n_seq, LANES), F32),
                   jax.ShapeDtypeStruct((batch, N_EXP, LANES), jnp.int32)),
        grid=(batch // n_b,),
        in_specs=[pl.BlockSpec((n_b * n_seq, LANES), lambda b: (b, 0))],
        out_specs=(pl.BlockSpec((n_b, N_EXP, n_seq), lambda b: (b, 0, 0)),
                   pl.BlockSpec((n_b * n_seq, LANES), lambda b: (b, 0)),
                   pl.BlockSpec((n_b * n_seq, LANES), lambda b: (b, 0)),
                   pl.BlockSpec((n_b, N_EXP, LANES), lambda b: (b, 0, 0))),
        scratch_shapes=[pltpu.VMEM((n_seq, n_seq), BF16)],
        compiler_params=_cparams(("arbitrary",)),
        name="ec_route",
    )(logits)


def _windows(off_ref, row0, n_exp, tile, cap):
    firsts, n_win = [], 0
    for e in range(n_exp):
        lo = off_ref[row0 + e * OFF_STRIDE + tile]
        hi = off_ref[row0 + e * OFF_STRIDE + tile + 1]
        first = (lo >> 4) << 4
        firsts.append(first)
        n_win = jnp.maximum(n_win, (hi - first + MOE_WIN - 1) // MOE_WIN)
    return firsts, n_win


def _dispatch_kernel(off_ref, slot_ref, h_ref, o_ref, *, e_blk, cap):
    b, half = pl.program_id(0), pl.program_id(1)
    o_ref[...] = jnp.zeros(o_ref.shape, o_ref.dtype)
    siota = lax.broadcasted_iota(jnp.int32, (MOE_WIN, 1), 0)
    n_cols = h_ref.shape[1]
    chunk = 512

    def token_tile(tile, carry):
        firsts, n_win = _windows(off_ref, (b * N_EXP + half * e_blk) * OFF_STRIDE, e_blk, tile, cap)
        t0 = pl.multiple_of(tile * MOE_TILE, MOE_TILE)

        def window(w, carry):
            bases, parts = [], []
            for e in range(e_blk):
                start = firsts[e] + w * MOE_WIN
                base = pl.multiple_of(jnp.minimum(start, cap - MOE_WIN), 16)
                srow = slot_ref[e, tile]
                srow = jnp.where(srow >= start.astype(F32), srow, -1.0)
                parts.append(jnp.where(srow == (base + siota).astype(F32), 1.0, 0.0).astype(BF16))
                bases.append(base)
            onehot = jnp.concatenate(parts, axis=0)
            for c0 in range(0, n_cols, chunk):
                res = jnp.dot(onehot, h_ref[pl.ds(t0, MOE_TILE), c0:c0 + chunk],
                              preferred_element_type=F32).astype(BF16)
                for e in range(e_blk):
                    rows = pl.ds(bases[e], MOE_WIN)
                    o_ref[e, rows, c0:c0 + chunk] = (o_ref[e, rows, c0:c0 + chunk]
                                                     + res[e * MOE_WIN:(e + 1) * MOE_WIN, :])
            return carry

        return lax.fori_loop(0, n_win, window, carry)

    lax.fori_loop(0, slot_ref.shape[1], token_tile, 0)


def _dispatch(off, slot_row, h2, *, batch, n_seq, cap):
    e_blk = 8
    tiles = n_seq // MOE_TILE
    return pl.pallas_call(
        functools.partial(_dispatch_kernel, e_blk=e_blk, cap=cap),
        out_shape=jax.ShapeDtypeStruct((N_EXP, batch * cap, D), BF16),
        grid_spec=pltpu.PrefetchScalarGridSpec(
            num_scalar_prefetch=1,
            grid=(batch, N_EXP // e_blk),
            in_specs=[
                pl.BlockSpec((None, e_blk, tiles, 1, MOE_TILE), lambda b, j, off: (b, j, 0, 0, 0)),
                pl.BlockSpec((n_seq, D), lambda b, j, off: (b, 0)),
            ],
            out_specs=pl.BlockSpec((e_blk, cap, D), lambda b, j, off: (j, b, 0)),
        ),
        compiler_params=_cparams(("arbitrary", "arbitrary")),
        name="ec_dispatch",
    )(off, slot_row.reshape(batch, N_EXP, tiles, 1, MOE_TILE), h2)


def _gate_up_kernel(x_ref, wg_ref, wu_ref, o_ref):
    x = x_ref[...]
    part = 256
    for c0 in range(0, o_ref.shape[1], part):
        g = jnp.dot(x, wg_ref[:, c0:c0 + part].astype(BF16), preferred_element_type=F32)
        u = jnp.dot(x, wu_ref[:, c0:c0 + part].astype(BF16), preferred_element_type=F32)
        o_ref[:, c0:c0 + part] = (jax.nn.silu(g) * u).astype(BF16)


def _down_kernel(h_ref, wd_ref, o_ref):
    y = jnp.dot(h_ref[...], wd_ref[...].astype(BF16), preferred_element_type=F32).astype(BF16)
    cap = o_ref.shape[1]
    for b in range(o_ref.shape[0]):
        o_ref[b] = y[b * cap:(b + 1) * cap, :]


def _experts(xs, w_gate, w_up, w_down, layer, *, batch, cap):
    m = batch * cap
    tn, tn_down = TN_UP, TN_DOWN
    hid = pl.pallas_call(
        _gate_up_kernel,
        out_shape=jax.ShapeDtypeStruct((N_EXP, m, EXP_FF), BF16),
        grid=(N_EXP, EXP_FF // tn),
        in_specs=[
            pl.BlockSpec((None, m, D), lambda e, j: (e, 0, 0)),
            pl.BlockSpec((None, None, D, tn), lambda e, j: (layer, e, 0, j)),
            pl.BlockSpec((None, None, D, tn), lambda e, j: (layer, e, 0, j)),
        ],
        out_specs=pl.BlockSpec((None, m, tn), lambda e, j: (e, 0, j)),
        compiler_params=_cparams(("arbitrary", "arbitrary")),
        name="expert_gate_up",
    )(xs, w_gate, w_up)
    return pl.pallas_call(
        _down_kernel,
        out_shape=jax.ShapeDtypeStruct((batch, N_EXP * cap, D), BF16),
        grid=(N_EXP, D // tn_down),
        in_specs=[
            pl.BlockSpec((None, m, EXP_FF), lambda e, j: (e, 0, 0)),
            pl.BlockSpec((None, None, EXP_FF, tn_down), lambda e, j: (layer, e, 0, j)),
        ],
        out_specs=pl.BlockSpec((batch, cap, tn_down), lambda e, j: (0, e, j)),
        compiler_params=_cparams(("arbitrary", "arbitrary")),
        name="expert_down",
    )(hid, w_down)


def _combine_kernel(off_ref, ye_ref, slot_ref, gate_ref, x_ref, g2_ref, lg_ref, lb_ref, o_ref, *, cap):
    b, tile = pl.program_id(0), pl.program_id(1)
    firsts, n_win = _windows(off_ref, b * N_EXP * OFF_STRIDE, N_EXP, tile, cap)
    lane_e = lax.broadcasted_iota(jnp.int32, (LANES, N_EXP * MOE_WIN), 1) // MOE_WIN
    spread = jnp.where(lax.broadcasted_iota(jnp.int32, (LANES, N_EXP * MOE_WIN), 0) == lane_e, 1.0, 0.0).astype(BF16)
    slot = jnp.dot(slot_ref[...].astype(BF16), spread, preferred_element_type=F32)
    gate = jnp.dot(gate_ref[...].astype(BF16), spread, preferred_element_type=F32).astype(BF16)
    siota = lax.broadcasted_iota(jnp.int32, (1, MOE_WIN), 1)

    def window(w, acc):
        starts, targets, rows = [], [], []
        for e in range(N_EXP):
            start = firsts[e] + w * MOE_WIN
            base = pl.multiple_of(jnp.minimum(start, cap - MOE_WIN), 16)
            starts.append(jnp.full((1, MOE_WIN), start, jnp.int32))
            targets.append(base + siota)
            rows.append(ye_ref[pl.ds(e * cap + base, MOE_WIN), :])
        start_l = jnp.concatenate(starts, axis=1).astype(F32)
        target_l = jnp.concatenate(targets, axis=1).astype(F32)
        comb = jnp.where((slot == target_l) & (slot >= start_l), gate, jnp.zeros_like(gate))
        return acc + jnp.dot(comb, jnp.concatenate(rows, axis=0), preferred_element_type=F32)

    moe = lax.fori_loop(0, n_win, window, jnp.zeros(x_ref.shape, F32))
    o_ref[...] = _ln(ALPHA * x_ref[...] + g2_ref[...] * moe) * lg_ref[...] + lb_ref[...]


def _combine(off, ye, slot_col, gate_col, x1, mod5, layer, ln_g, ln_b, *, batch, n_seq, cap):
    tm = MOE_TILE
    tiles = n_seq // tm
    vec_spec = pl.BlockSpec((1, D), lambda b, i, off: (0, 0))
    row_spec = lambda w: pl.BlockSpec((tm, w), lambda b, i, off: (b * tiles + i, 0))
    return pl.pallas_call(
        functools.partial(_combine_kernel, cap=cap),
        out_shape=jax.ShapeDtypeStruct((batch * n_seq, D), F32),
        grid_spec=pltpu.PrefetchScalarGridSpec(
            num_scalar_prefetch=1,
            grid=(batch, tiles),
            in_specs=[
                pl.BlockSpec((None, N_EXP * cap, D), lambda b, i, off: (b, 0, 0)),
                row_spec(LANES), row_spec(LANES), row_spec(D),
                pl.BlockSpec((None, None, None, 1, D), lambda b, i, off: (layer, b, 5, 0, 0)),
                vec_spec, vec_spec,
            ],
            out_specs=row_spec(D),
        ),
        compiler_params=_cparams(("arbitrary", "arbitrary")),
        name="ec_combine_ln",
    )(off, ye, slot_col, gate_col, x1, mod5, ln_g.reshape(1, D), ln_b.reshape(1, D))


def _moe_block(x1, h2, logits, mod5, layer, w_gate, w_up, w_down, ln_g, ln_b, *, batch, n_seq):
    cap = 2 * n_seq // N_EXP
    assert n_seq % MOE_TILE == 0 and n_seq // MOE_TILE < OFF_STRIDE and cap % MOE_WIN == 0
    assert cap <= 256
    slot_row, slot_col, gate_col, off = _route(logits, batch=batch, n_seq=n_seq, cap=cap)
    off = off[:, :, :OFF_STRIDE].reshape(batch * N_EXP * OFF_STRIDE)
    xs = _dispatch(off, slot_row, h2, batch=batch, n_seq=n_seq, cap=cap)
    ye = _experts(xs, w_gate, w_up, w_down, layer, batch=batch, cap=cap)
    return _combine(off, ye, slot_col, gate_col, x1, mod5, layer, ln_g, ln_b, batch=batch, n_seq=n_seq, cap=cap)


def _cd_in_kernel(x_ref, sh_ref, sc_ref, w_ref, lg_ref, lb_ref, u_ref, vg_ref, xg_ref):
    h = (_ln(x_ref[...]) * (1.0 + sc_ref[...]) + sh_ref[...]).astype(BF16)
    chunk = 512
    for c0 in range(0, SG_W, chunk):
        u_ref[:, c0:c0 + chunk] = jax.nn.gelu(
            jnp.dot(h, w_ref[:, c0:c0 + chunk], preferred_element_type=F32)).astype(BF16)
    for c0 in range(0, SG_W, chunk):
        z = jax.nn.gelu(jnp.dot(h, w_ref[:, SG_W + c0:SG_W + c0 + chunk], preferred_element_type=F32))
        parts = [_ln(z[:, j * HEAD:(j + 1) * HEAD]) for j in range(chunk // HEAD)]
        vg = jnp.concatenate(parts, axis=1) * lg_ref[:, c0:c0 + chunk] + lb_ref[:, c0:c0 + chunk]
        vg_ref[:, c0:c0 + chunk] = vg.astype(BF16)
    for c0 in range(0, CONV_CH, chunk):
        a = jnp.dot(h, w_ref[:, 2 * SG_W + c0:2 * SG_W + c0 + chunk], preferred_element_type=F32)
        gt = jnp.dot(h, w_ref[:, 2 * SG_W + CONV_CH + c0:2 * SG_W + CONV_CH + c0 + chunk],
                     preferred_element_type=F32)
        xg_ref[:, c0:c0 + chunk] = a * jax.nn.sigmoid(gt)


def _cd_in_proj(x2d, mod5, layer, w_bf, sg_ln_g, sg_ln_b, *, n_seq):
    rows = x2d.shape[0]
    tm = 256
    tiles_per_seq = n_seq // tm
    mod_spec = lambda chunk: pl.BlockSpec((None, None, None, 1, D),
                                          lambda i: (layer, i // tiles_per_seq, chunk, 0, 0))
    vec_spec = pl.BlockSpec((1, SG_W), lambda i: (0, 0))
    row_spec = lambda w: pl.BlockSpec((tm, w), lambda i: (i, 0))
    return pl.pallas_call(
        _cd_in_kernel,
        out_shape=(jax.ShapeDtypeStruct((rows, SG_W), BF16),
                   jax.ShapeDtypeStruct((rows, SG_W), BF16),
                   jax.ShapeDtypeStruct((rows, CONV_CH), F32)),
        grid=(rows // tm,),
        in_specs=[row_spec(D), mod_spec(0), mod_spec(1),
                  pl.BlockSpec((D, CD_IN), lambda i: (0, 0)), vec_spec, vec_spec],
        out_specs=(row_spec(SG_W), row_spec(SG_W), row_spec(CONV_CH)),
        compiler_params=_cparams(("arbitrary",)),
        name="cd_in_proj",
    )(x2d, mod5, mod5, w_bf, sg_ln_g.reshape(1, SG_W), sg_ln_b.reshape(1, SG_W))


CONV_HALO = 16
CONV_ROWS = 32


def _cd_mix_kernel(u_ref, vg_ref, xp_ref, xc_ref, xn_ref, sgw_ref, sgb_ref, cw_ref, cb_ref, lg_ref, lb_ref,
                   o_ref, sh_ref, cv_ref, *, tiles_per_seq):
    i = pl.program_id(1)
    tm = xc_ref.shape[0]
    for c in range(tm // CHUNK):
        for g in range(SG_G):
            rs, cs = slice(c * CHUNK, (c + 1) * CHUNK), slice(g * HEAD, (g + 1) * HEAD)
            sp = jnp.dot(sgw_ref[g], vg_ref[rs, cs], preferred_element_type=F32) + sgb_ref[g]
            o_ref[rs, cs] = (u_ref[rs, cs].astype(F32) * sp).astype(BF16)
    sh_ref[0, 0:CONV_HALO, :] = jnp.where(i > 0, xp_ref[...], 0.0)
    sh_ref[0, CONV_HALO:CONV_HALO + tm, :] = xc_ref[...]
    sh_ref[0, CONV_HALO + tm:, :] = jnp.where(i < tiles_per_seq - 1, xn_ref[...], 0.0)
    keep = tm + 2 * CONV_HALO - SUBLANES
    for s in range(1, SUBLANES):
        sh_ref[s, 0:keep, :] = sh_ref[0, s:s + keep, :]
    off = CONV_HALO - CONV_K // 2
    for r0 in range(0, tm, CONV_ROWS):
        for c0 in range(0, CONV_CH, LANES):
            acc = jnp.zeros((CONV_ROWS, LANES), F32) + cb_ref[:, c0:c0 + LANES]
            for k in range(CONV_K):
                a = r0 + off + k
                acc = acc + (sh_ref[a % SUBLANES, a - a % SUBLANES:a - a % SUBLANES + CONV_ROWS, c0:c0 + LANES]
                             * cw_ref[k:k + 1, c0:c0 + LANES])
            cv_ref[r0:r0 + CONV_ROWS, c0:c0 + LANES] = acc
    y = _ln(cv_ref[...]) * lg_ref[...] + lb_ref[...]
    o_ref[:, SG_W:] = jax.nn.silu(y).astype(BF16)


def _cd_mix(u, vg, xg, sg_w, sg_b, conv_w, conv_b, conv_ln_g, conv_ln_b, *, batch, n_seq):
    tm = 128
    tiles = n_seq // tm
    hpt = tm // CONV_HALO
    n_halo = batch * n_seq // CONV_HALO
    row_spec = lambda w: pl.BlockSpec((tm, w), lambda b, i: (b * tiles + i, 0))
    vec_spec = pl.BlockSpec((1, CONV_CH), lambda b, i: (0, 0))
    sgb_full = jnp.broadcast_to(sg_b[:, :, None], (SG_G, CHUNK, HEAD))
    return pl.pallas_call(
        functools.partial(_cd_mix_kernel, tiles_per_seq=tiles),
        out_shape=jax.ShapeDtypeStruct((batch * n_seq, D), BF16),
        grid=(batch, tiles),
        in_specs=[
            row_spec(SG_W), row_spec(SG_W),
            pl.BlockSpec((CONV_HALO, CONV_CH), lambda b, i: (jnp.maximum((b * tiles + i) * hpt - 1, 0), 0)),
            row_spec(CONV_CH),
            pl.BlockSpec((CONV_HALO, CONV_CH), lambda b, i: (jnp.minimum((b * tiles + i + 1) * hpt, n_halo - 1), 0)),
            pl.BlockSpec((SG_G, CHUNK, CHUNK), lambda b, i: (0, 0, 0)),
            pl.BlockSpec((SG_G, CHUNK, HEAD), lambda b, i: (0, 0, 0)),
            pl.BlockSpec((CONV_K + 1, CONV_CH), lambda b, i: (0, 0)),
            vec_spec, vec_spec, vec_spec,
        ],
        out_specs=pl.BlockSpec((tm, D), lambda b, i: (b * tiles + i, 0)),
        scratch_shapes=[pltpu.VMEM((SUBLANES, tm + 2 * CONV_HALO, CONV_CH), F32), pltpu.VMEM((tm, CONV_CH), F32)],
        compiler_params=_cparams(("arbitrary", "arbitrary")),
        name="cd_mix",
    )(u, vg, xg, xg, xg, sg_w.astype(BF16), sgb_full,
      jnp.pad(conv_w.reshape(CONV_K, CONV_CH), ((0, 1), (0, 0))),
      conv_b.reshape(1, CONV_CH), conv_ln_g.reshape(1, CONV_CH), conv_ln_b.reshape(1, CONV_CH))


def kernel(x, c, ctx, c_ctx, w_mod, b_mod, ln1_g, ln1_b, ln2_g, ln2_b, w_router, w_gate, w_up, w_down,
           ab_w_in, ab_w_out, sink, cd_w_in, cd_w_out, sg_ln_g, sg_ln_b, sg_w, sg_b,
           conv_w, conv_b, conv_ln_g, conv_ln_b):
    batch, n_seq, _ = x.shape
    n_ctx = ctx.shape[1]
    assert x.shape[2] == D and batch + 1 <= MOD_ROWS

    c_pad = jnp.concatenate([c, c_ctx[None, :], jnp.zeros((MOD_ROWS - batch - 1, D), F32)], axis=0)
    mod = _modulation(c_pad, w_mod, b_mod)
    mod5 = mod.reshape(DEPTH, MOD_ROWS, 6, 1, D)
    w_router_pad = jnp.pad(w_router, ((0, 0), (0, 0), (0, LANES - N_EXP))).astype(BF16)

    x0 = x.reshape(batch * n_seq, D)
    ctx2d = ctx.reshape(batch * n_ctx, D)

    w_in = ab_w_in[0].astype(BF16)
    tabs = _rope_tables(n_seq)
    p_lat = _ab_in_proj(x0, mod5, 0, lambda b: b, w_in, tabs, n_seq=n_seq,
                        rope_cols=K_END, q_cols=Q_END, name="ab_in_proj")
    p_ctx = _ab_in_proj(ctx2d, mod5, 0, lambda b: batch, w_in[:, Q_END:V_END], tabs, n_seq=n_ctx,
                        rope_cols=0, q_cols=0, name="ab_ctx_proj")
    attn = _attention(p_lat, p_ctx, sink[0], batch=batch, n_seq=n_seq, n_ctx=n_ctx)
    four = _fourier(p_lat, batch=batch, n_seq=n_seq)
    x1, h2, logits = _out_proj([attn, four], ab_w_out[0].astype(BF16), x0, mod5, 0, ln1_g[0], ln1_b[0],
                               w_router_pad[0], n_seq=n_seq)
    x2 = _moe_block(x1, h2, logits, mod5, 0, w_gate, w_up, w_down, ln2_g[0], ln2_b[0], batch=batch, n_seq=n_seq)

    u, vg, xg = _cd_in_proj(x2, mod5, 1, cd_w_in[0].astype(BF16), sg_ln_g[0], sg_ln_b[0], n_seq=n_seq)
    mix = _cd_mix(u, vg, xg, sg_w[0], sg_b[0], conv_w[0], conv_b[0], conv_ln_g[0], conv_ln_b[0],
                  batch=batch, n_seq=n_seq)
    x3, h4, logits1 = _out_proj([mix], cd_w_out[0].astype(BF16), x2, mod5, 1, ln1_g[1], ln1_b[1],
                                w_router_pad[1], n_seq=n_seq)
    x4 = _moe_block(x3, h4, logits1, mod5, 1, w_gate, w_up, w_down, ln2_g[1], ln2_b[1], batch=batch, n_seq=n_seq)
    return x4.reshape(batch, n_seq, D)
```

```python
import functools

import jax
import jax.numpy as jnp
from jax import lax
from jax.experimental import pallas as pl
from jax.experimental.pallas import tpu as pltpu

F32 = jnp.float32
BF16 = jnp.bfloat16

D = 2048
HEAD = 128
N_Q = 12
N_KV = 4
GQA = 3
WINDOW = 128
GRID_W = 64
ROPE_BASE = 10000.0
Q_END = N_Q * HEAD
KV_W = N_KV * HEAD
K_END = Q_END + KV_W
V_END = K_END + KV_W
FOUR_G = 4
FOUR_W = FOUR_G * HEAD
AB_IN = V_END + FOUR_W
SG_G = 8
SG_W = SG_G * HEAD
CHUNK = 128
CONV_CH = 1024
CONV_K = 31
CD_IN = 4096
N_EXP = 16
EXP_FF = 2048
DEPTH = 2
ALPHA = (2 * DEPTH) ** 0.25
LN_EPS = 1e-6
ATTN_SCALE = HEAD ** -0.5
MOD_ROWS = 16
LANES = 128
SUBLANES = 8
VMEM_LIMIT = 56 * 1024 * 1024
MOE_TILE = 256
MOE_WIN = 64
OFF_STRIDE = 16
TN_UP = 512
TN_DOWN = 512


def _cparams(sem):
    return pltpu.CompilerParams(dimension_semantics=sem, vmem_limit_bytes=VMEM_LIMIT)


def _ln(x):
    mu = jnp.mean(x, axis=-1, keepdims=True)
    xc = x - mu
    var = jnp.mean(xc * xc, axis=-1, keepdims=True)
    return xc * lax.rsqrt(var + LN_EPS)


def _mod_kernel(c_ref, w_ref, b_ref, o_ref):
    s = jax.nn.silu(c_ref[...]).astype(BF16)
    o_ref[...] = jnp.dot(s, w_ref[...].astype(BF16), preferred_element_type=F32) + b_ref[...]


def _modulation(c_pad, w_mod, b_mod):
    tn = 1024
    return pl.pallas_call(
        _mod_kernel,
        out_shape=jax.ShapeDtypeStruct((DEPTH, MOD_ROWS, 6 * D), F32),
        grid=(DEPTH, 6 * D // tn),
        in_specs=[
            pl.BlockSpec((MOD_ROWS, D), lambda l, j: (0, 0)),
            pl.BlockSpec((None, D, tn), lambda l, j: (l, 0, j)),
            pl.BlockSpec((None, 1, tn), lambda l, j: (l, 0, j)),
        ],
        out_specs=pl.BlockSpec((None, MOD_ROWS, tn), lambda l, j: (l, 0, j)),
        compiler_params=_cparams(("arbitrary", "arbitrary")),
        name="modulation",
    )(c_pad, w_mod, b_mod.reshape(DEPTH, 1, 6 * D))


def _rope(a, cos, sa, sb):
    return a * cos + pltpu.roll(a, HEAD - 32, 1) * sa + pltpu.roll(a, 32, 1) * sb


def _ab_in_kernel(x_ref, sh_ref, sc_ref, w_ref, cos_ref, sa_ref, sb_ref, o_ref, *, n_cols, rope_cols, q_cols):
    h = (_ln(x_ref[...]) * (1.0 + sc_ref[...]) + sh_ref[...]).astype(BF16)
    chunk = 512
    for c0 in range(0, n_cols, chunk):
        acc = jnp.dot(h, w_ref[:, c0:c0 + chunk], preferred_element_type=F32)
        if c0 < rope_cols:
            cos, sa, sb = cos_ref[...], sa_ref[...], sb_ref[...]
            parts = []
            for j in range(chunk // HEAD):
                r = _rope(acc[:, j * HEAD:(j + 1) * HEAD], cos, sa, sb)
                parts.append(r * ATTN_SCALE if c0 < q_cols else r)
            acc = jnp.concatenate(parts, axis=1)
        o_ref[:, c0:c0 + chunk] = acc.astype(BF16)


def _ab_in_proj(x2d, mod5, layer, row_of_tile, w_bf, rope_tabs, *, n_seq, rope_cols, q_cols, name):
    rows, _ = x2d.shape
    n_cols = w_bf.shape[1]
    tm = 256
    tiles_per_seq = n_seq // tm
    cos, sa, sb = rope_tabs
    kern = functools.partial(_ab_in_kernel, n_cols=n_cols, rope_cols=rope_cols, q_cols=q_cols)
    mod_spec = lambda chunk: pl.BlockSpec((None, None, None, 1, D),
                                          lambda i: (layer, row_of_tile(i // tiles_per_seq), chunk, 0, 0))
    tab_spec = pl.BlockSpec((tm, HEAD), lambda i: (i % tiles_per_seq, 0))
    return pl.pallas_call(
        kern,
        out_shape=jax.ShapeDtypeStruct((rows, n_cols), BF16),
        grid=(rows // tm,),
        in_specs=[
            pl.BlockSpec((tm, D), lambda i: (i, 0)),
            mod_spec(0), mod_spec(1),
            pl.BlockSpec((D, n_cols), lambda i: (0, 0)),
            tab_spec, tab_spec, tab_spec,
        ],
        out_specs=pl.BlockSpec((tm, n_cols), lambda i: (i, 0)),
        compiler_params=_cparams(("arbitrary",)),
        name=name,
    )(x2d, mod5, mod5, w_bf, cos, sa, sb)


def _rope_tables(n):
    quarter = HEAD // 4
    t = jnp.arange(n)
    r = (t // GRID_W).astype(F32)
    col = (t % GRID_W).astype(F32)
    inv = ROPE_BASE ** (-jnp.arange(quarter, dtype=F32) / quarter)
    ang_r, ang_c = r[:, None] * inv, col[:, None] * inv
    zero = jnp.zeros_like(ang_r)
    cos = jnp.concatenate([jnp.cos(ang_r)] * 2 + [jnp.cos(ang_c)] * 2, axis=1)
    sa = jnp.concatenate([-jnp.sin(ang_r), zero, -jnp.sin(ang_c), zero], axis=1)
    sb = jnp.concatenate([zero, jnp.sin(ang_r), zero, jnp.sin(ang_c)], axis=1)
    return cos, sa, sb


def _attn_kernel(sink_ref, q_ref, k_ref, v_ref, kc_ref, vc_ref, o_ref, bias_ref, *, n_seq):
    hk = pl.program_id(1)
    kc = kc_ref[...]
    vc = vc_ref[...]
    kw_len = 3 * WINDOW
    rows = GQA * WINDOW
    n_blocks = n_seq // WINDOW
    row = lax.broadcasted_iota(jnp.int32, (rows, 1), 0)
    sink = jnp.where(row < WINDOW, sink_ref[hk * GQA],
                     jnp.where(row < 2 * WINDOW, sink_ref[hk * GQA + 1], sink_ref[hk * GQA + 2]))
    rel = (row & (WINDOW - 1)) - lax.broadcasted_iota(jnp.int32, (1, kw_len), 1)
    for i in range(3):
        bias_ref[i] = jnp.where(jnp.abs(rel + i * WINDOW) <= WINDOW, 0.0, -1e30)
    dn = (((1,), (1,)), ((), ()))

    def block(n, back):
        r0 = n * WINDOW if isinstance(n, int) else pl.multiple_of(n * WINDOW, WINDOW)
        start = r0 - back * WINDOW
        if not isinstance(start, int):
            start = pl.multiple_of(start, WINDOW)
        qs = q_ref[pl.ds(r0, WINDOW), :]
        q3 = jnp.concatenate([qs[:, g * HEAD:(g + 1) * HEAD] for g in range(GQA)], axis=0)
        kw = k_ref[pl.ds(start, kw_len), :]
        vw = v_ref[pl.ds(start, kw_len), :]
        s_w = lax.dot_general(q3, kw, dn, preferred_element_type=F32) + bias_ref[back]
        s_c = lax.dot_general(q3, kc, dn, preferred_element_type=F32)
        m = jnp.maximum(jnp.maximum(jnp.max(s_w, axis=1, keepdims=True), jnp.max(s_c, axis=1, keepdims=True)), sink)
        p_w = jnp.exp(s_w - m)
        p_c = jnp.exp(s_c - m)
        den = jnp.sum(p_w, axis=1, keepdims=True) + jnp.sum(p_c, axis=1, keepdims=True) + jnp.exp(sink - m)
        o = (jnp.dot(p_w.astype(BF16), vw, preferred_element_type=F32)
             + jnp.dot(p_c.astype(BF16), vc, preferred_element_type=F32)) / den
        o_ref[pl.ds(r0, WINDOW), :] = jnp.concatenate(
            [o[g * WINDOW:(g + 1) * WINDOW, :] for g in range(GQA)], axis=1).astype(BF16)

    block(0, 0)
    block(n_blocks - 1, 2)

    per_iter = max(g for g in (7, 2, 1) if (n_blocks - 2) % g == 0)

    def interior(i, carry):
        for g in range(per_iter):
            block(1 + per_iter * i + g, 1)
        return carry

    lax.fori_loop(0, (n_blocks - 2) // per_iter, interior, 0)


def _attention(p_lat, p_ctx, sink, *, batch, n_seq, n_ctx):
    qw = GQA * HEAD
    assert n_seq % WINDOW == 0 and n_seq >= 3 * WINDOW
    return pl.pallas_call(
        functools.partial(_attn_kernel, n_seq=n_seq),
        out_shape=jax.ShapeDtypeStruct((batch * n_seq, Q_END), BF16),
        grid=(batch, N_KV),
        in_specs=[
            pl.BlockSpec(memory_space=pltpu.SMEM),
            pl.BlockSpec((n_seq, qw), lambda b, h: (b, h)),
            pl.BlockSpec((n_seq, HEAD), lambda b, h: (b, Q_END // HEAD + h)),
            pl.BlockSpec((n_seq, HEAD), lambda b, h: (b, K_END // HEAD + h)),
            pl.BlockSpec((n_ctx, HEAD), lambda b, h: (b, h)),
            pl.BlockSpec((n_ctx, HEAD), lambda b, h: (b, N_KV + h)),
        ],
        out_specs=pl.BlockSpec((n_seq, qw), lambda b, h: (b, h)),
        scratch_shapes=[pltpu.VMEM((3, GQA * WINDOW, 3 * WINDOW), F32)],
        compiler_params=_cparams(("arbitrary", "arbitrary")),
        name="banded_gqa",
    )(sink, p_lat, p_lat, p_lat, p_ctx, p_ctx)


DFT_ROWS = 32


def _fourier_kernel(z_ref, cs_ref, ca_ref, sa_ref, cb_ref, sb_ref, o_ref, cn_ref, sn_ref):
    @pl.when(pl.program_id(0) == 0)
    def _():
        cb, sb = cb_ref[...], sb_ref[...]

        def rows(a, carry):
            ca, sa = ca_ref[a], sa_ref[a]
            r = pl.ds(pl.multiple_of(a * DFT_ROWS, DFT_ROWS), DFT_ROWS)
            cn_ref[r, :] = (ca * cb - sa * sb).astype(BF16)
            sn_ref[r, :] = (sa * cb + ca * sb).astype(BF16)
            return carry

        lax.fori_loop(0, ca_ref.shape[0], rows, 0)

    cs = cs_ref[...]
    a_parts, b_parts = [], []
    for g in range(FOUR_G):
        ab = jnp.dot(z_ref[:, g * HEAD:(g + 1) * HEAD], cs, preferred_element_type=F32)
        a_parts.append(ab[:, :HEAD])
        b_parts.append(ab[:, HEAD:])
    a = jnp.concatenate(a_parts, axis=1).astype(BF16)
    b = jnp.concatenate(b_parts, axis=1).astype(BF16)
    out = (jnp.dot(cn_ref[...], a, preferred_element_type=F32)
           - jnp.dot(sn_ref[...], b, preferred_element_type=F32))
    o_ref[...] = (out * (1.0 / (z_ref.shape[0] * HEAD) ** 0.5)).astype(BF16)


def _dft_rows(j, m):
    ang = ((j[:, None] * jnp.arange(m, dtype=jnp.int32)[None, :]) % m).astype(F32) * (2.0 * jnp.pi / m)
    return jnp.cos(ang), jnp.sin(ang)


def _fourier(p_lat, *, batch, n_seq):
    assert n_seq % DFT_ROWS == 0
    n_a = n_seq // DFT_ROWS
    cc, sc = _dft_rows(jnp.arange(HEAD, dtype=jnp.int32), HEAD)
    cs = jnp.concatenate([cc, sc], axis=1).astype(BF16)
    ca, sa = _dft_rows(jnp.arange(n_a, dtype=jnp.int32) * DFT_ROWS, n_seq)
    cb, sb = _dft_rows(jnp.arange(DFT_ROWS, dtype=jnp.int32), n_seq)
    coarse_spec = pl.BlockSpec((n_a, 1, n_seq), lambda b: (0, 0, 0))
    fine_spec = pl.BlockSpec((DFT_ROWS, n_seq), lambda b: (0, 0))
    return pl.pallas_call(
        _fourier_kernel,
        out_shape=jax.ShapeDtypeStruct((batch * n_seq, FOUR_W), BF16),
        grid=(batch,),
        in_specs=[
            pl.BlockSpec((n_seq, FOUR_W), lambda b: (b, V_END // FOUR_W)),
            pl.BlockSpec((HEAD, 2 * HEAD), lambda b: (0, 0)),
            coarse_spec, coarse_spec, fine_spec, fine_spec,
        ],
        out_specs=pl.BlockSpec((n_seq, FOUR_W), lambda b: (b, 0)),
        scratch_shapes=[pltpu.VMEM((n_seq, n_seq), BF16), pltpu.VMEM((n_seq, n_seq), BF16)],
        compiler_params=_cparams(("arbitrary",)),
        name="fourier_mix",
    )(p_lat, cs, ca.reshape(n_a, 1, n_seq), sa.reshape(n_a, 1, n_seq), cb, sb)


def _out_proj_kernel(*refs, n_parts):
    a_refs = refs[:n_parts]
    (w_ref, x_ref, g1_ref, lg_ref, lb_ref, sh_ref, sc_ref, wr_ref, x1_ref, h2_ref, lo_ref) = refs[n_parts:]
    y = None
    k0 = 0
    for a_ref in a_refs:
        kk = a_ref.shape[1]
        t = jnp.dot(a_ref[...], w_ref[k0:k0 + kk, :], preferred_element_type=F32)
        y = t if y is None else y + t
        k0 += kk
    x1 = _ln(ALPHA * x_ref[...] + g1_ref[...] * y) * lg_ref[...] + lb_ref[...]
    x1_ref[...] = x1
    h2 = (_ln(x1) * (1.0 + sc_ref[...]) + sh_ref[...]).astype(BF16)
    h2_ref[...] = h2
    lo_ref[...] = jnp.dot(h2, wr_ref[...], preferred_element_type=F32)


def _out_proj(parts, w_bf, x2d, mod5, layer, ln_g, ln_b, w_router_pad, *, n_seq):
    rows = x2d.shape[0]
    tm = 256
    tiles_per_seq = n_seq // tm
    mod_spec = lambda chunk: pl.BlockSpec((None, None, None, 1, D),
                                          lambda i: (layer, i // tiles_per_seq, chunk, 0, 0))
    vec_spec = pl.BlockSpec((1, D), lambda i: (0, 0))
    in_specs = [pl.BlockSpec((tm, p.shape[1]), lambda i: (i, 0)) for p in parts]
    in_specs += [
        pl.BlockSpec((D, D), lambda i: (0, 0)),
        pl.BlockSpec((tm, D), lambda i: (i, 0)),
        mod_spec(2), vec_spec, vec_spec, mod_spec(3), mod_spec(4),
        pl.BlockSpec((D, LANES), lambda i: (0, 0)),
    ]
    return pl.pallas_call(
        functools.partial(_out_proj_kernel, n_parts=len(parts)),
        out_shape=(jax.ShapeDtypeStruct((rows, D), F32),
                   jax.ShapeDtypeStruct((rows, D), BF16),
                   jax.ShapeDtypeStruct((rows, LANES), F32)),
        grid=(rows // tm,),
        in_specs=in_specs,
        out_specs=(pl.BlockSpec((tm, D), lambda i: (i, 0)),
                   pl.BlockSpec((tm, D), lambda i: (i, 0)),
                   pl.BlockSpec((tm, LANES), lambda i: (i, 0))),
        compiler_params=_cparams(("arbitrary",)),
        name="out_proj_ln",
    )(*parts, w_bf, x2d, mod5, ln_g.reshape(1, D), ln_b.reshape(1, D), mod5, mod5, w_router_pad)


def _route_kernel(lo_ref, slot_row_ref, slot_col_ref, gate_col_ref, off_ref, tri_ref, *, n_seq, cap, n_b):
    @pl.when(pl.program_id(0) == 0)
    def _():
        blk = 256
        r = lax.broadcasted_iota(jnp.int32, (blk, n_seq), 0)
        c = lax.broadcasted_iota(jnp.int32, (blk, n_seq), 1)
        for r0 in range(0, n_seq, blk):
            tri_ref[r0:r0 + blk, :] = jnp.where(r + r0 < c, 1.0, 0.0).astype(BF16)

    affs = []
    for b in range(n_b):
        lt = jnp.transpose(lo_ref[b * n_seq:(b + 1) * n_seq, :])[:N_EXP, :]
        ex = jnp.exp(lt - jnp.max(lt, axis=0, keepdims=True))
        affs.append(ex / jnp.sum(ex, axis=0, keepdims=True))
    aff = jnp.concatenate(affs, axis=0)
    n_rows = n_b * N_EXP

    def count_ge(t):
        return jnp.sum(jnp.where(aff >= t, 1.0, 0.0), axis=1, keepdims=True)

    def bisect_log(_, lohi):
        lo, hi = lohi
        mid = 0.5 * (lo + hi)
        ok = count_ge(jnp.exp(mid)) >= cap
        return jnp.where(ok, mid, lo), jnp.where(ok, hi, mid)

    def bisect_lin(_, lohi):
        lo, hi = lohi
        mid = 0.5 * (lo + hi)
        ok = count_ge(mid) >= cap
        return jnp.where(ok, mid, lo), jnp.where(ok, hi, mid)

    lo0 = jnp.full((n_rows, 1), -150.0, F32)
    hi0 = jnp.full((n_rows, 1), 1.0, F32)
    lo_l, hi_l = lax.fori_loop(0, 18, bisect_log, (lo0, hi0))
    thr, above = lax.fori_loop(0, 34, bisect_lin, (jnp.exp(lo_l), jnp.exp(hi_l)))
    gt = aff >= above
    eq = (aff >= thr) & (aff < above)
    need = cap - jnp.sum(jnp.where(gt, 1.0, 0.0), axis=1, keepdims=True)
    tri = tri_ref[...]
    eq_before = jnp.dot(jnp.where(eq, 1.0, 0.0).astype(BF16), tri, preferred_element_type=F32)
    sel = gt | (eq & (eq_before < need))
    sel_bf = jnp.where(sel, 1.0, 0.0).astype(BF16)
    slot = jnp.where(sel, jnp.dot(sel_bf, tri, preferred_element_type=F32), -1.0)
    gate = jnp.where(sel, aff, 0.0)
    tt = lax.broadcasted_iota(jnp.int32, (n_seq, LANES), 0)
    ti = lax.broadcasted_iota(jnp.int32, (n_seq, LANES), 1)
    before = jnp.where(tt < ti * MOE_TILE, 1.0, 0.0).astype(BF16)
    off = jnp.dot(sel_bf, before, preferred_element_type=F32).astype(jnp.int32)
    pad = jnp.full((LANES - N_EXP, n_seq), -1.0, F32)
    for b in range(n_b):
        rows = slice(b * N_EXP, (b + 1) * N_EXP)
        slot_row_ref[b] = slot[rows, :]
        off_ref[b] = off[rows, :]
        slot_col_ref[b * n_seq:(b + 1) * n_seq, :] = jnp.transpose(jnp.concatenate([slot[rows, :], pad], axis=0))
        gate_col_ref[b * n_seq:(b + 1) * n_seq, :] = jnp.transpose(jnp.concatenate([gate[rows, :], pad * 0.0], axis=0))


def _route(logits, *, batch, n_seq, cap):
    n_b = 4 if batch % 4 == 0 else 1
    return pl.pallas_call(
        functools.partial(_route_kernel, n_seq=n_seq, cap=cap, n_b=n_b),
        out_shape=(jax.ShapeDtypeStruct((batch, N_EXP, n_seq), F32),
                   jax.ShapeDtypeStruct((batch * n_seq, LANES), F32),
                   jax.ShapeDtypeStruct((batch * n_seq, LANES), F32),
                   jax.ShapeDtypeStruct((batch, N_EXP, LANES), jnp.int32)),
        grid=(batch // n_b,),
        in_specs=[pl.BlockSpec((n_b * n_seq, LANES), lambda b: (b, 0))],
        out_specs=(pl.BlockSpec((n_b, N_EXP, n_seq), lambda b: (b, 0, 0)),
                   pl.BlockSpec((n_b * n_seq, LANES), lambda b: (b, 0)),
                   pl.BlockSpec((n_b * n_seq, LANES), lambda b: (b, 0)),
                   pl.BlockSpec((n_b, N_EXP, LANES), lambda b: (b, 0, 0))),
        scratch_shapes=[pltpu.VMEM((n_seq, n_seq), BF16)],
        compiler_params=_cparams(("arbitrary",)),
        name="ec_route",
    )(logits)


def _windows(off_ref, row0, n_exp, tile, cap):
    firsts, n_win = [], 0
    for e in range(n_exp):
        lo = off_ref[row0 + e * OFF_STRIDE + tile]
        hi = off_ref[row0 + e * OFF_STRIDE + tile + 1]
        first = (lo >> 4) << 4
        firsts.append(first)
        n_win = jnp.maximum(n_win, (hi - first + MOE_WIN - 1) // MOE_WIN)
    return firsts, n_win


def _dispatch_kernel(off_ref, slot_ref, h_ref, o_ref, *, e_blk, cap):
    b, half = pl.program_id(0), pl.program_id(1)
    o_ref[...] = jnp.zeros(o_ref.shape, o_ref.dtype)
    siota = lax.broadcasted_iota(jnp.int32, (MOE_WIN, 1), 0)
    n_cols = h_ref.shape[1]
    chunk = 512

    def token_tile(tile, carry):
        firsts, n_win = _windows(off_ref, (b * N_EXP + half * e_blk) * OFF_STRIDE, e_blk, tile, cap)
        t0 = pl.multiple_of(tile * MOE_TILE, MOE_TILE)

        def window(w, carry):
            bases, parts = [], []
            for e in range(e_blk):
                start = firsts[e] + w * MOE_WIN
                base = pl.multiple_of(jnp.minimum(start, cap - MOE_WIN), 16)
                srow = slot_ref[e, tile]
                srow = jnp.where(srow >= start.astype(F32), srow, -1.0)
                parts.append(jnp.where(srow == (base + siota).astype(F32), 1.0, 0.0).astype(BF16))
                bases.append(base)
            onehot = jnp.concatenate(parts, axis=0)
            for c0 in range(0, n_cols, chunk):
                res = jnp.dot(onehot, h_ref[pl.ds(t0, MOE_TILE), c0:c0 + chunk],
                              preferred_element_type=F32).astype(BF16)
                for e in range(e_blk):
                    rows = pl.ds(bases[e], MOE_WIN)
                    o_ref[e, rows, c0:c0 + chunk] = (o_ref[e, rows, c0:c0 + chunk]
                                                     + res[e * MOE_WIN:(e + 1) * MOE_WIN, :])
            return carry

        return lax.fori_loop(0, n_win, window, carry)

    lax.fori_loop(0, slot_ref.shape[1], token_tile, 0)


def _dispatch(off, slot_row, h2, *, batch, n_seq, cap):
    e_blk = 8
    tiles = n_seq // MOE_TILE
    return pl.pallas_call(
        functools.partial(_dispatch_kernel, e_blk=e_blk, cap=cap),
        out_shape=jax.ShapeDtypeStruct((N_EXP, batch * cap, D), BF16),
        grid_spec=pltpu.PrefetchScalarGridSpec(
            num_scalar_prefetch=1,
            grid=(batch, N_EXP // e_blk),
            in_specs=[
                pl.BlockSpec((None, e_blk, tiles, 1, MOE_TILE), lambda b, j, off: (b, j, 0, 0, 0)),
                pl.BlockSpec((n_seq, D), lambda b, j, off: (b, 0)),
            ],
            out_specs=pl.BlockSpec((e_blk, cap, D), lambda b, j, off: (j, b, 0)),
        ),
        compiler_params=_cparams(("arbitrary", "arbitrary")),
        name="ec_dispatch",
    )(off, slot_row.reshape(batch, N_EXP, tiles, 1, MOE_TILE), h2)


def _gate_up_kernel(x_ref, wg_ref, wu_ref, o_ref):
    x = x_ref[...]
    part = 256
    for c0 in range(0, o_ref.shape[1], part):
        g = jnp.dot(x, wg_ref[:, c0:c0 + part].astype(BF16), preferred_element_type=F32)
        u = jnp.dot(x, wu_ref[:, c0:c0 + part].astype(BF16), preferred_element_type=F32)
        o_ref[:, c0:c0 + part] = (jax.nn.silu(g) * u).astype(BF16)


def _down_kernel(h_ref, wd_ref, o_ref):
    y = jnp.dot(h_ref[...], wd_ref[...].astype(BF16), preferred_element_type=F32).astype(BF16)
    cap = o_ref.shape[1]
    for b in range(o_ref.shape[0]):
        o_ref[b] = y[b * cap:(b + 1) * cap, :]


def _experts(xs, w_gate, w_up, w_down, layer, *, batch, cap):
    m = batch * cap
    tn, tn_down = TN_UP, TN_DOWN
    hid = pl.pallas_call(
        _gate_up_kernel,
        out_shape=jax.ShapeDtypeStruct((N_EXP, m, EXP_FF), BF16),
        grid=(N_EXP, EXP_FF // tn),
        in_specs=[
            pl.BlockSpec((None, m, D), lambda e, j: (e, 0, 0)),
            pl.BlockSpec((None, None, D, tn), lambda e, j: (layer, e, 0, j)),
            pl.BlockSpec((None, None, D, tn), lambda e, j: (layer, e, 0, j)),
        ],
        out_specs=pl.BlockSpec((None, m, tn), lambda e, j: (e, 0, j)),
        compiler_params=_cparams(("arbitrary", "arbitrary")),
        name="expert_gate_up",
    )(xs, w_gate, w_up)
    return pl.pallas_call(
        _down_kernel,
        out_shape=jax.ShapeDtypeStruct((batch, N_EXP * cap, D), BF16),
        grid=(N_EXP, D // tn_down),
        in_specs=[
            pl.BlockSpec((None, m, EXP_FF), lambda e, j: (e, 0, 0)),
            pl.BlockSpec((None, None, EXP_FF, tn_down), lambda e, j: (layer, e, 0, j)),
        ],
        out_specs=pl.BlockSpec((batch, cap, tn_down), lambda e, j: (0, e, j)),
        compiler_params=_cparams(("arbitrary", "arbitrary")),
        name="expert_down",
    )(hid, w_down)


def _combine_kernel(off_ref, ye_ref, slot_ref, gate_ref, x_ref, g2_ref, lg_ref, lb_ref, o_ref, moe_ref, *, cap):
    b, tile = pl.program_id(0), pl.program_id(1)
    firsts, n_win = _windows(off_ref, b * N_EXP * OFF_STRIDE, N_EXP, tile, cap)
    lane_e = lax.broadcasted_iota(jnp.int32, (LANES, N_EXP * MOE_WIN), 1) // MOE_WIN
    spread = jnp.where(lax.broadcasted_iota(jnp.int32, (LANES, N_EXP * MOE_WIN), 0) == lane_e, 1.0, 0.0).astype(BF16)
    slot = jnp.dot(slot_ref[...].astype(BF16), spread, preferred_element_type=F32)
    gate = jnp.dot(gate_ref[...].astype(BF16), spread, preferred_element_type=F32).astype(BF16)
    siota = lax.broadcasted_iota(jnp.int32, (1, MOE_WIN), 1)

    def window(w):
        starts, targets, rows = [], [], []
        for e in range(N_EXP):
            start = firsts[e] + w * MOE_WIN
            base = pl.multiple_of(jnp.minimum(start, cap - MOE_WIN), 16)
            starts.append(jnp.full((1, MOE_WIN), start, jnp.int32))
            targets.append(base + siota)
            rows.append(ye_ref[pl.ds(e * cap + base, MOE_WIN), :])
        start_l = jnp.concatenate(starts, axis=1).astype(F32)
        target_l = jnp.concatenate(targets, axis=1).astype(F32)
        comb = jnp.where((slot == target_l) & (slot >= start_l), gate, jnp.zeros_like(gate))
        return jnp.dot(comb, jnp.concatenate(rows, axis=0), preferred_element_type=F32)

    moe_ref[...] = window(0)

    @pl.when(n_win > 1)
    def _():
        def more(w, carry):
            moe_ref[...] += window(w)
            return carry
        lax.fori_loop(1, n_win, more, 0)

    o_ref[...] = _ln(ALPHA * x_ref[...] + g2_ref[...] * moe_ref[...]) * lg_ref[...] + lb_ref[...]


def _combine(off, ye, slot_col, gate_col, x1, mod5, layer, ln_g, ln_b, *, batch, n_seq, cap):
    tm = MOE_TILE
    tiles = n_seq // tm
    vec_spec = pl.BlockSpec((1, D), lambda b, i, off: (0, 0))
    row_spec = lambda w: pl.BlockSpec((tm, w), lambda b, i, off: (b * tiles + i, 0))
    return pl.pallas_call(
        functools.partial(_combine_kernel, cap=cap),
        out_shape=jax.ShapeDtypeStruct((batch * n_seq, D), F32),
        grid_spec=pltpu.PrefetchScalarGridSpec(
            num_scalar_prefetch=1,
            grid=(batch, tiles),
            in_specs=[
                pl.BlockSpec((None, N_EXP * cap, D), lambda b, i, off: (b, 0, 0)),
                row_spec(LANES), row_spec(LANES), row_spec(D),
                pl.BlockSpec((None, None, None, 1, D), lambda b, i, off: (layer, b, 5, 0, 0)),
                vec_spec, vec_spec,
            ],
            out_specs=row_spec(D),
            scratch_shapes=[pltpu.VMEM((tm, D), F32)],
        ),
        compiler_params=_cparams(("arbitrary", "arbitrary")),
        name="ec_combine_ln",
    )(off, ye, slot_col, gate_col, x1, mod5, ln_g.reshape(1, D), ln_b.reshape(1, D))


def _moe_block(x1, h2, logits, mod5, layer, w_gate, w_up, w_down, ln_g, ln_b, *, batch, n_seq):
    cap = 2 * n_seq // N_EXP
    assert n_seq % MOE_TILE == 0 and n_seq // MOE_TILE < OFF_STRIDE and cap % MOE_WIN == 0
    assert cap <= 256
    slot_row, slot_col, gate_col, off = _route(logits, batch=batch, n_seq=n_seq, cap=cap)
    off = off[:, :, :OFF_STRIDE].reshape(batch * N_EXP * OFF_STRIDE)
    xs = _dispatch(off, slot_row, h2, batch=batch, n_seq=n_seq, cap=cap)
    ye = _experts(xs, w_gate, w_up, w_down, layer, batch=batch, cap=cap)
    return _combine(off, ye, slot_col, gate_col, x1, mod5, layer, ln_g, ln_b, batch=batch, n_seq=n_seq, cap=cap)


def _cd_in_kernel(x_ref, sh_ref, sc_ref, w_ref, lg_ref, lb_ref, u_ref, vg_ref, xg_ref):
    h = (_ln(x_ref[...]) * (1.0 + sc_ref[...]) + sh_ref[...]).astype(BF16)
    chunk = 512
    for c0 in range(0, SG_W, chunk):
        u_ref[:, c0:c0 + chunk] = jax.nn.gelu(
            jnp.dot(h, w_ref[:, c0:c0 + chunk], preferred_element_type=F32)).astype(BF16)
    for c0 in range(0, SG_W, chunk):
        z = jax.nn.gelu(jnp.dot(h, w_ref[:, SG_W + c0:SG_W + c0 + chunk], preferred_element_type=F32))
        parts = [_ln(z[:, j * HEAD:(j + 1) * HEAD]) for j in range(chunk // HEAD)]
        vg = jnp.concatenate(parts, axis=1) * lg_ref[:, c0:c0 + chunk] + lb_ref[:, c0:c0 + chunk]
        vg_ref[:, c0:c0 + chunk] = vg.astype(BF16)
    for c0 in range(0, CONV_CH, chunk):
        a = jnp.dot(h, w_ref[:, 2 * SG_W + c0:2 * SG_W + c0 + chunk], preferred_element_type=F32)
        gt = jnp.dot(h, w_ref[:, 2 * SG_W + CONV_CH + c0:2 * SG_W + CONV_CH + c0 + chunk],
                     preferred_element_type=F32)
        xg_ref[:, c0:c0 + chunk] = a * jax.nn.sigmoid(gt)


def _cd_in_proj(x2d, mod5, layer, w_bf, sg_ln_g, sg_ln_b, *, n_seq):
    rows = x2d.shape[0]
    tm = 256
    tiles_per_seq = n_seq // tm
    mod_spec = lambda chunk: pl.BlockSpec((None, None, None, 1, D),
                                          lambda i: (layer, i // tiles_per_seq, chunk, 0, 0))
    vec_spec = pl.BlockSpec((1, SG_W), lambda i: (0, 0))
    row_spec = lambda w: pl.BlockSpec((tm, w), lambda i: (i, 0))
    return pl.pallas_call(
        _cd_in_kernel,
        out_shape=(jax.ShapeDtypeStruct((rows, SG_W), BF16),
                   jax.ShapeDtypeStruct((rows, SG_W), BF16),
                   jax.ShapeDtypeStruct((rows, CONV_CH), F32)),
        grid=(rows // tm,),
        in_specs=[row_spec(D), mod_spec(0), mod_spec(1),
                  pl.BlockSpec((D, CD_IN), lambda i: (0, 0)), vec_spec, vec_spec],
        out_specs=(row_spec(SG_W), row_spec(SG_W), row_spec(CONV_CH)),
        compiler_params=_cparams(("arbitrary",)),
        name="cd_in_proj",
    )(x2d, mod5, mod5, w_bf, sg_ln_g.reshape(1, SG_W), sg_ln_b.reshape(1, SG_W))


CONV_HALO = 16
CONV_ROWS = 32


def _cd_mix_kernel(u_ref, vg_ref, xp_ref, xc_ref, xn_ref, sgw_ref, sgb_ref, cw_ref, cb_ref, lg_ref, lb_ref,
                   o_ref, sh_ref, cv_ref, *, tiles_per_seq):
    i = pl.program_id(1)
    tm = xc_ref.shape[0]
    for c in range(tm // CHUNK):
        for g in range(SG_G):
            rs, cs = slice(c * CHUNK, (c + 1) * CHUNK), slice(g * HEAD, (g + 1) * HEAD)
            sp = jnp.dot(sgw_ref[g], vg_ref[rs, cs], preferred_element_type=F32) + sgb_ref[g]
            o_ref[rs, cs] = (u_ref[rs, cs].astype(F32) * sp).astype(BF16)
    sh_ref[0, 0:CONV_HALO, :] = jnp.where(i > 0, xp_ref[...], 0.0)
    sh_ref[0, CONV_HALO:CONV_HALO + tm, :] = xc_ref[...]
    sh_ref[0, CONV_HALO + tm:, :] = jnp.where(i < tiles_per_seq - 1, xn_ref[...], 0.0)
    keep = tm + 2 * CONV_HALO - SUBLANES
    for s in range(1, SUBLANES):
        sh_ref[s, 0:keep, :] = sh_ref[0, s:s + keep, :]
    off = CONV_HALO - CONV_K // 2
    for r0 in range(0, tm, CONV_ROWS):
        for c0 in range(0, CONV_CH, LANES):
            acc = jnp.zeros((CONV_ROWS, LANES), F32) + cb_ref[:, c0:c0 + LANES]
            for k in range(CONV_K):
                a = r0 + off + k
                acc = acc + (sh_ref[a % SUBLANES, a - a % SUBLANES:a - a % SUBLANES + CONV_ROWS, c0:c0 + LANES]
                             * cw_ref[k:k + 1, c0:c0 + LANES])
            cv_ref[r0:r0 + CONV_ROWS, c0:c0 + LANES] = acc
    y = _ln(cv_ref[...]) * lg_ref[...] + lb_ref[...]
    o_ref[:, SG_W:] = jax.nn.silu(y).astype(BF16)


def _cd_mix(u, vg, xg, sg_w, sg_b, conv_w, conv_b, conv_ln_g, conv_ln_b, *, batch, n_seq):
    tm = 256
    tiles = n_seq // tm
    hpt = tm // CONV_HALO
    n_halo = batch * n_seq // CONV_HALO
    row_spec = lambda w: pl.BlockSpec((tm, w), lambda b, i: (b * tiles + i, 0))
    vec_spec = pl.BlockSpec((1, CONV_CH), lambda b, i: (0, 0))
    sgb_full = jnp.broadcast_to(sg_b[:, :, None], (SG_G, CHUNK, HEAD))
    return pl.pallas_call(
        functools.partial(_cd_mix_kernel, tiles_per_seq=tiles),
        out_shape=jax.ShapeDtypeStruct((batch * n_seq, D), BF16),
        grid=(batch, tiles),
        in_specs=[
            row_spec(SG_W), row_spec(SG_W),
            pl.BlockSpec((CONV_HALO, CONV_CH), lambda b, i: (jnp.maximum((b * tiles + i) * hpt - 1, 0), 0)),
            row_spec(CONV_CH),
            pl.BlockSpec((CONV_HALO, CONV_CH), lambda b, i: (jnp.minimum((b * tiles + i + 1) * hpt, n_halo - 1), 0)),
            pl.BlockSpec((SG_G, CHUNK, CHUNK), lambda b, i: (0, 0, 0)),
            pl.BlockSpec((SG_G, CHUNK, HEAD), lambda b, i: (0, 0, 0)),
            pl.BlockSpec((CONV_K + 1, CONV_CH), lambda b, i: (0, 0)),
            vec_spec, vec_spec, vec_spec,
        ],
        out_specs=pl.BlockSpec((tm, D), lambda b, i: (b * tiles + i, 0)),
        scratch_shapes=[pltpu.VMEM((SUBLANES, tm + 2 * CONV_HALO, CONV_CH), F32), pltpu.VMEM((tm, CONV_CH), F32)],
        compiler_params=_cparams(("arbitrary", "arbitrary")),
        name="cd_mix",
    )(u, vg, xg, xg, xg, sg_w.astype(BF16), sgb_full,
      jnp.pad(conv_w.reshape(CONV_K, CONV_CH), ((0, 1), (0, 0))),
      conv_b.reshape(1, CONV_CH), conv_ln_g.reshape(1, CONV_CH), conv_ln_b.reshape(1, CONV_CH))


def kernel(x, c, ctx, c_ctx, w_mod, b_mod, ln1_g, ln1_b, ln2_g, ln2_b, w_router, w_gate, w_up, w_down,
           ab_w_in, ab_w_out, sink, cd_w_in, cd_w_out, sg_ln_g, sg_ln_b, sg_w, sg_b,
           conv_w, conv_b, conv_ln_g, conv_ln_b):
    batch, n_seq, _ = x.shape
    n_ctx = ctx.shape[1]
    assert x.shape[2] == D and batch + 1 <= MOD_ROWS

    c_pad = jnp.concatenate([c, c_ctx[None, :], jnp.zeros((MOD_ROWS - batch - 1, D), F32)], axis=0)
    mod = _modulation(c_pad, w_mod, b_mod)
    mod5 = mod.reshape(DEPTH, MOD_ROWS, 6, 1, D)
    w_router_pad = jnp.pad(w_router, ((0, 0), (0, 0), (0, LANES - N_EXP))).astype(BF16)

    x0 = x.reshape(batch * n_seq, D)
    ctx2d = ctx.reshape(batch * n_ctx, D)

    w_in = ab_w_in[0].astype(BF16)
    tabs = _rope_tables(n_seq)
    p_lat = _ab_in_proj(x0, mod5, 0, lambda b: b, w_in, tabs, n_seq=n_seq,
                        rope_cols=K_END, q_cols=Q_END, name="ab_in_proj")
    p_ctx = _ab_in_proj(ctx2d, mod5, 0, lambda b: batch, w_in[:, Q_END:V_END], tabs, n_seq=n_ctx,
                        rope_cols=0, q_cols=0, name="ab_ctx_proj")
    attn = _attention(p_lat, p_ctx, sink[0], batch=batch, n_seq=n_seq, n_ctx=n_ctx)
    four = _fourier(p_lat, batch=batch, n_seq=n_seq)
    x1, h2, logits = _out_proj([attn, four], ab_w_out[0].astype(BF16), x0, mod5, 0, ln1_g[0], ln1_b[0],
                               w_router_pad[0], n_seq=n_seq)
    x2 = _moe_block(x1, h2, logits, mod5, 0, w_gate, w_up, w_down, ln2_g[0], ln2_b[0], batch=batch, n_seq=n_seq)

    u, vg, xg = _cd_in_proj(x2, mod5, 1, cd_w_in[0].astype(BF16), sg_ln_g[0], sg_ln_b[0], n_seq=n_seq)
    mix = _cd_mix(u, vg, xg, sg_w[0], sg_b[0], conv_w[0], conv_b[0], conv_ln_g[0], conv_ln_b[0],
                  batch=batch, n_seq=n_seq)
    x3, h4, logits1 = _out_proj([mix], cd_w_out[0].astype(BF16), x2, mod5, 1, ln1_g[1], ln1_b[1],
                                w_router_pad[1], n_seq=n_seq)
    x4 = _moe_block(x3, h4, logits1, mod5, 1, w_gate, w_up, w_down, ln2_g[1], ln2_b[1], batch=batch, n_seq=n_seq)
    return x4.reshape(batch, n_seq, D)
```

```python
import functools

import jax
import jax.numpy as jnp
from jax import lax
from jax.experimental import pallas as pl
from jax.experimental.pallas import tpu as pltpu

F32 = jnp.float32
BF16 = jnp.bfloat16

D = 2048
HEAD = 128
N_Q = 12
N_KV = 4
GQA = 3
WINDOW = 128
GRID_W = 64
ROPE_BASE = 10000.0
Q_END = N_Q * HEAD
KV_W = N_KV * HEAD
K_END = Q_END + KV_W
V_END = K_END + KV_W
FOUR_G = 4
FOUR_W = FOUR_G * HEAD
AB_IN = V_END + FOUR_W
SG_G = 8
SG_W = SG_G * HEAD
CHUNK = 128
CONV_CH = 1024
CONV_K = 31
CD_IN = 4096
N_EXP = 16
EXP_FF = 2048
DEPTH = 2
ALPHA = (2 * DEPTH) ** 0.25
LN_EPS = 1e-6
ATTN_SCALE = HEAD ** -0.5
MOD_ROWS = 16
LANES = 128
SUBLANES = 8
VMEM_LIMIT = 56 * 1024 * 1024
MOE_TILE = 256
MOE_WIN = 64
OFF_STRIDE = 16
TN_UP = 512
TN_DOWN = 512


def _cparams(sem):
    return pltpu.CompilerParams(dimension_semantics=sem, vmem_limit_bytes=VMEM_LIMIT)


def _ln(x):
    mu = jnp.mean(x, axis=-1, keepdims=True)
    xc = x - mu
    var = jnp.mean(xc * xc, axis=-1, keepdims=True)
    return xc * lax.rsqrt(var + LN_EPS)


def _mod_kernel(c_ref, w_ref, b_ref, o_ref):
    s = jax.nn.silu(c_ref[...]).astype(BF16)
    o_ref[...] = jnp.dot(s, w_ref[...].astype(BF16), preferred_element_type=F32) + b_ref[...]


def _modulation(c_pad, w_mod, b_mod):
    tn = 1024
    return pl.pallas_call(
        _mod_kernel,
        out_shape=jax.ShapeDtypeStruct((DEPTH, MOD_ROWS, 6 * D), F32),
        grid=(DEPTH, 6 * D // tn),
        in_specs=[
            pl.BlockSpec((MOD_ROWS, D), lambda l, j: (0, 0)),
            pl.BlockSpec((None, D, tn), lambda l, j: (l, 0, j)),
            pl.BlockSpec((None, 1, tn), lambda l, j: (l, 0, j)),
        ],
        out_specs=pl.BlockSpec((None, MOD_ROWS, tn), lambda l, j: (l, 0, j)),
        compiler_params=_cparams(("arbitrary", "arbitrary")),
        name="modulation",
    )(c_pad, w_mod, b_mod.reshape(DEPTH, 1, 6 * D))


def _rope(a, cos, sa, sb):
    return a * cos + pltpu.roll(a, HEAD - 32, 1) * sa + pltpu.roll(a, 32, 1) * sb


def _ab_in_kernel(x_ref, sh_ref, sc_ref, w_ref, cos_ref, sa_ref, sb_ref, o_ref, *, n_cols, rope_cols, q_cols):
    h = (_ln(x_ref[...]) * (1.0 + sc_ref[...]) + sh_ref[...]).astype(BF16)
    chunk = 512
    for c0 in range(0, n_cols, chunk):
        acc = jnp.dot(h, w_ref[:, c0:c0 + chunk], preferred_element_type=F32)
        if c0 < rope_cols:
            cos, sa, sb = cos_ref[...], sa_ref[...], sb_ref[...]
            parts = []
            for j in range(chunk // HEAD):
                r = _rope(acc[:, j * HEAD:(j + 1) * HEAD], cos, sa, sb)
                parts.append(r * ATTN_SCALE if c0 < q_cols else r)
            acc = jnp.concatenate(parts, axis=1)
        o_ref[:, c0:c0 + chunk] = acc.astype(BF16)


def _ab_in_proj(x2d, mod5, layer, row_of_tile, w_bf, rope_tabs, *, n_seq, rope_cols, q_cols, name):
    rows, _ = x2d.shape
    n_cols = w_bf.shape[1]
    tm = 256
    tiles_per_seq = n_seq // tm
    cos, sa, sb = rope_tabs
    kern = functools.partial(_ab_in_kernel, n_cols=n_cols, rope_cols=rope_cols, q_cols=q_cols)
    mod_spec = lambda chunk: pl.BlockSpec((None, None, None, 1, D),
                                          lambda i: (layer, row_of_tile(i // tiles_per_seq), chunk, 0, 0))
    tab_spec = pl.BlockSpec((tm, HEAD), lambda i: (i % tiles_per_seq, 0))
    return pl.pallas_call(
        kern,
        out_shape=jax.ShapeDtypeStruct((rows, n_cols), BF16),
        grid=(rows // tm,),
        in_specs=[
            pl.BlockSpec((tm, D), lambda i: (i, 0)),
            mod_spec(0), mod_spec(1),
            pl.BlockSpec((D, n_cols), lambda i: (0, 0)),
            tab_spec, tab_spec, tab_spec,
        ],
        out_specs=pl.BlockSpec((tm, n_cols), lambda i: (i, 0)),
        compiler_params=_cparams(("arbitrary",)),
        name=name,
    )(x2d, mod5, mod5, w_bf, cos, sa, sb)


def _rope_tables(n):
    quarter = HEAD // 4
    t = jnp.arange(n)
    r = (t // GRID_W).astype(F32)
    col = (t % GRID_W).astype(F32)
    inv = ROPE_BASE ** (-jnp.arange(quarter, dtype=F32) / quarter)
    ang_r, ang_c = r[:, None] * inv, col[:, None] * inv
    zero = jnp.zeros_like(ang_r)
    cos = jnp.concatenate([jnp.cos(ang_r)] * 2 + [jnp.cos(ang_c)] * 2, axis=1)
    sa = jnp.concatenate([-jnp.sin(ang_r), zero, -jnp.sin(ang_c), zero], axis=1)
    sb = jnp.concatenate([zero, jnp.sin(ang_r), zero, jnp.sin(ang_c)], axis=1)
    return cos, sa, sb


def _attn_kernel(sink_ref, q_ref, k_ref, v_ref, kc_ref, vc_ref, o_ref, bias_ref, *, n_seq):
    hk = pl.program_id(1)
    kc = kc_ref[...]
    vc = vc_ref[...]
    kw_len = 3 * WINDOW
    rows = GQA * WINDOW
    n_blocks = n_seq // WINDOW
    row = lax.broadcasted_iota(jnp.int32, (rows, 1), 0)
    sink = jnp.where(row < WINDOW, sink_ref[hk * GQA],
                     jnp.where(row < 2 * WINDOW, sink_ref[hk * GQA + 1], sink_ref[hk * GQA + 2]))
    rel = (row & (WINDOW - 1)) - lax.broadcasted_iota(jnp.int32, (1, kw_len), 1)
    for i in range(3):
        bias_ref[i] = jnp.where(jnp.abs(rel + i * WINDOW) <= WINDOW, 0.0, -1e30)
    dn = (((1,), (1,)), ((), ()))

    def block(n, back):
        r0 = n * WINDOW if isinstance(n, int) else pl.multiple_of(n * WINDOW, WINDOW)
        start = r0 - back * WINDOW
        if not isinstance(start, int):
            start = pl.multiple_of(start, WINDOW)
        qs = q_ref[pl.ds(r0, WINDOW), :]
        q3 = jnp.concatenate([qs[:, g * HEAD:(g + 1) * HEAD] for g in range(GQA)], axis=0)
        kw = k_ref[pl.ds(start, kw_len), :]
        vw = v_ref[pl.ds(start, kw_len), :]
        s_w = lax.dot_general(q3, kw, dn, preferred_element_type=F32) + bias_ref[back]
        s_c = lax.dot_general(q3, kc, dn, preferred_element_type=F32)
        m = jnp.maximum(jnp.maximum(jnp.max(s_w, axis=1, keepdims=True), jnp.max(s_c, axis=1, keepdims=True)), sink)
        p_w = jnp.exp(s_w - m)
        p_c = jnp.exp(s_c - m)
        den = jnp.sum(p_w, axis=1, keepdims=True) + jnp.sum(p_c, axis=1, keepdims=True) + jnp.exp(sink - m)
        o = (jnp.dot(p_w.astype(BF16), vw, preferred_element_type=F32)
             + jnp.dot(p_c.astype(BF16), vc, preferred_element_type=F32)) / den
        o_ref[pl.ds(r0, WINDOW), :] = jnp.concatenate(
            [o[g * WINDOW:(g + 1) * WINDOW, :] for g in range(GQA)], axis=1).astype(BF16)

    block(0, 0)
    block(n_blocks - 1, 2)

    per_iter = max(g for g in (7, 2, 1) if (n_blocks - 2) % g == 0)

    def interior(i, carry):
        for g in range(per_iter):
            block(1 + per_iter * i + g, 1)
        return carry

    lax.fori_loop(0, (n_blocks - 2) // per_iter, interior, 0)


def _attention(p_lat, p_ctx, sink, *, batch, n_seq, n_ctx):
    qw = GQA * HEAD
    assert n_seq % WINDOW == 0 and n_seq >= 3 * WINDOW
    return pl.pallas_call(
        functools.partial(_attn_kernel, n_seq=n_seq),
        out_shape=jax.ShapeDtypeStruct((batch * n_seq, Q_END), BF16),
        grid=(batch, N_KV),
        in_specs=[
            pl.BlockSpec(memory_space=pltpu.SMEM),
            pl.BlockSpec((n_seq, qw), lambda b, h: (b, h)),
            pl.BlockSpec((n_seq, HEAD), lambda b, h: (b, Q_END // HEAD + h)),
            pl.BlockSpec((n_seq, HEAD), lambda b, h: (b, K_END // HEAD + h)),
            pl.BlockSpec((n_ctx, HEAD), lambda b, h: (b, h)),
            pl.BlockSpec((n_ctx, HEAD), lambda b, h: (b, N_KV + h)),
        ],
        out_specs=pl.BlockSpec((n_seq, qw), lambda b, h: (b, h)),
        scratch_shapes=[pltpu.VMEM((3, GQA * WINDOW, 3 * WINDOW), F32)],
        compiler_params=_cparams(("arbitrary", "arbitrary")),
        name="banded_gqa",
    )(sink, p_lat, p_lat, p_lat, p_ctx, p_ctx)


DFT_ROWS = 32


def _fourier_kernel(z_ref, cs_ref, ca_ref, sa_ref, cb_ref, sb_ref, o_ref, cn_ref, sn_ref):
    n = z_ref.shape[0]
    half = n // 2

    @pl.when(pl.program_id(0) == 0)
    def _():
        cb, sb = cb_ref[...], sb_ref[...]

        def rows(a, carry):
            ca, sa = ca_ref[a], sa_ref[a]
            r = pl.ds(pl.multiple_of(a * DFT_ROWS, DFT_ROWS), DFT_ROWS)
            cn_ref[r, :] = (ca * cb - sa * sb).astype(BF16)
            sn_ref[r, :] = (sa * cb + ca * sb).astype(BF16)
            return carry

        lax.fori_loop(0, ca_ref.shape[0], rows, 0)

    cs = cs_ref[...]
    a_parts, b_parts = [], []
    for g in range(FOUR_G):
        ab = jnp.dot(z_ref[:, g * HEAD:(g + 1) * HEAD], cs, preferred_element_type=F32)
        a_parts.append(ab[:, :HEAD])
        b_parts.append(ab[:, HEAD:])
    a = jnp.concatenate(a_parts, axis=1).astype(BF16)
    b = jnp.concatenate(b_parts, axis=1).astype(BF16)
    p = jnp.dot(cn_ref[...], a, preferred_element_type=F32)
    q = jnp.dot(sn_ref[...], b, preferred_element_type=F32)
    scale = 1.0 / (n * HEAD) ** 0.5
    o_ref[0:half, :] = ((p[0:half, :] - q[0:half, :]) * scale).astype(BF16)
    o_ref[half:n, :] = ((p[1:half + 1, :] + q[1:half + 1, :]) * scale).astype(BF16)


def _dft_rows(j, m):
    ang = ((j[:, None] * jnp.arange(m, dtype=jnp.int32)[None, :]) % m).astype(F32) * (2.0 * jnp.pi / m)
    return jnp.cos(ang), jnp.sin(ang)


def _fourier(p_lat, *, batch, n_seq):
    assert n_seq % (2 * DFT_ROWS) == 0
    n_a = n_seq // (2 * DFT_ROWS) + 1
    cc, sc = _dft_rows(jnp.arange(HEAD, dtype=jnp.int32), HEAD)
    cs = jnp.concatenate([cc, sc], axis=1).astype(BF16)
    ca, sa = _dft_rows(jnp.arange(n_a, dtype=jnp.int32) * DFT_ROWS, n_seq)
    cb, sb = _dft_rows(jnp.arange(DFT_ROWS, dtype=jnp.int32), n_seq)
    coarse_spec = pl.BlockSpec((n_a, 1, n_seq), lambda b: (0, 0, 0))
    fine_spec = pl.BlockSpec((DFT_ROWS, n_seq), lambda b: (0, 0))
    return pl.pallas_call(
        _fourier_kernel,
        out_shape=jax.ShapeDtypeStruct((batch * n_seq, FOUR_W), BF16),
        grid=(batch,),
        in_specs=[
            pl.BlockSpec((n_seq, FOUR_W), lambda b: (b, V_END // FOUR_W)),
            pl.BlockSpec((HEAD, 2 * HEAD), lambda b: (0, 0)),
            coarse_spec, coarse_spec, fine_spec, fine_spec,
        ],
        out_specs=pl.BlockSpec((n_seq, FOUR_W), lambda b: (b, 0)),
        scratch_shapes=[pltpu.VMEM((n_a * DFT_ROWS, n_seq), BF16), pltpu.VMEM((n_a * DFT_ROWS, n_seq), BF16)],
        compiler_params=_cparams(("arbitrary",)),
        name="fourier_mix",
    )(p_lat, cs, ca.reshape(n_a, 1, n_seq), sa.reshape(n_a, 1, n_seq), cb, sb)


def _out_proj_kernel(*refs, n_parts, folded):
    a_refs = refs[:n_parts]
    rest = refs[n_parts:]
    if folded:
        flip_ref, rest = rest[0], rest[1:]
    (w_ref, x_ref, g1_ref, lg_ref, lb_ref, sh_ref, sc_ref, wr_ref, x1_ref, h2_ref, lo_ref) = rest
    y = None
    k0 = 0
    for p, a_ref in enumerate(a_refs):
        kk = a_ref.shape[1]
        a = a_ref[...]
        if folded and p == n_parts - 1:
            a = jnp.dot(flip_ref[...], a, preferred_element_type=F32).astype(BF16)
        t = jnp.dot(a, w_ref[k0:k0 + kk, :], preferred_element_type=F32)
        y = t if y is None else y + t
        k0 += kk
    x1 = _ln(ALPHA * x_ref[...] + g1_ref[...] * y) * lg_ref[...] + lb_ref[...]
    x1_ref[...] = x1
    h2 = (_ln(x1) * (1.0 + sc_ref[...]) + sh_ref[...]).astype(BF16)
    h2_ref[...] = h2
    lo_ref[...] = jnp.dot(h2, wr_ref[...], preferred_element_type=F32)


def _out_proj(parts, w_bf, x2d, mod5, layer, ln_g, ln_b, w_router_pad, *, n_seq, folded_last=False):
    rows = x2d.shape[0]
    tm = 256
    tiles_per_seq = n_seq // tm
    half_tiles = tiles_per_seq // 2
    mod_spec = lambda chunk: pl.BlockSpec((None, None, None, 1, D),
                                          lambda i: (layer, i // tiles_per_seq, chunk, 0, 0))
    vec_spec = pl.BlockSpec((1, D), lambda i: (0, 0))
    in_specs = [pl.BlockSpec((tm, p.shape[1]), lambda i: (i, 0)) for p in parts]
    extra = []
    if folded_last:
        assert tiles_per_seq % 2 == 0

        def folded_tile(i):
            li = i % tiles_per_seq
            return i - li + jnp.where(li < half_tiles, li, half_tiles + tiles_per_seq - 1 - li)

        in_specs[-1] = pl.BlockSpec((tm, parts[-1].shape[1]), lambda i: (folded_tile(i), 0))
        in_specs.append(pl.BlockSpec((None, tm, tm), lambda i: ((i % tiles_per_seq) // half_tiles, 0, 0)))
        eye = jnp.eye(tm, dtype=BF16)
        extra = [jnp.stack([eye, eye[::-1]])]
    in_specs += [
        pl.BlockSpec((D, D), lambda i: (0, 0)),
        pl.BlockSpec((tm, D), lambda i: (i, 0)),
        mod_spec(2), vec_spec, vec_spec, mod_spec(3), mod_spec(4),
        pl.BlockSpec((D, LANES), lambda i: (0, 0)),
    ]
    return pl.pallas_call(
        functools.partial(_out_proj_kernel, n_parts=len(parts), folded=folded_last),
        out_shape=(jax.ShapeDtypeStruct((rows, D), F32),
                   jax.ShapeDtypeStruct((rows, D), BF16),
                   jax.ShapeDtypeStruct((rows, LANES), F32)),
        grid=(rows // tm,),
        in_specs=in_specs,
        out_specs=(pl.BlockSpec((tm, D), lambda i: (i, 0)),
                   pl.BlockSpec((tm, D), lambda i: (i, 0)),
                   pl.BlockSpec((tm, LANES), lambda i: (i, 0))),
        compiler_params=_cparams(("arbitrary",)),
        name="out_proj_ln",
    )(*parts, *extra, w_bf, x2d, mod5, ln_g.reshape(1, D), ln_b.reshape(1, D), mod5, mod5, w_router_pad)


def _route_kernel(lo_ref, slot_row_ref, slot_col_ref, gate_col_ref, off_ref, tri_ref, *, n_seq, cap, n_b):
    @pl.when(pl.program_id(0) == 0)
    def _():
        blk = 256
        r = lax.broadcasted_iota(jnp.int32, (blk, n_seq), 0)
        c = lax.broadcasted_iota(jnp.int32, (blk, n_seq), 1)
        for r0 in range(0, n_seq, blk):
            tri_ref[r0:r0 + blk, :] = jnp.where(r + r0 < c, 1.0, 0.0).astype(BF16)

    affs = []
    for b in range(n_b):
        lt = jnp.transpose(lo_ref[b * n_seq:(b + 1) * n_seq, :])[:N_EXP, :]
        ex = jnp.exp(lt - jnp.max(lt, axis=0, keepdims=True))
        affs.append(ex / jnp.sum(ex, axis=0, keepdims=True))
    aff = jnp.concatenate(affs, axis=0)
    n_rows = n_b * N_EXP

    def count_ge(t):
        return jnp.sum(jnp.where(aff >= t, 1.0, 0.0), axis=1, keepdims=True)

    def bisect_log(_, lohi):
        lo, hi = lohi
        mid = 0.5 * (lo + hi)
        ok = count_ge(jnp.exp(mid)) >= cap
        return jnp.where(ok, mid, lo), jnp.where(ok, hi, mid)

    def bisect_lin(_, lohi):
        lo, hi = lohi
        mid = 0.5 * (lo + hi)
        ok = count_ge(mid) >= cap
        return jnp.where(ok, mid, lo), jnp.where(ok, hi, mid)

    lo0 = jnp.full((n_rows, 1), -150.0, F32)
    hi0 = jnp.full((n_rows, 1), 1.0, F32)
    lo_l, hi_l = lax.fori_loop(0, 18, bisect_log, (lo0, hi0))
    thr, above = lax.fori_loop(0, 34, bisect_lin, (jnp.exp(lo_l), jnp.exp(hi_l)))
    gt = aff >= above
    eq = (aff >= thr) & (aff < above)
    need = cap - jnp.sum(jnp.where(gt, 1.0, 0.0), axis=1, keepdims=True)
    tri = tri_ref[...]
    eq_before = jnp.dot(jnp.where(eq, 1.0, 0.0).astype(BF16), tri, preferred_element_type=F32)
    sel = gt | (eq & (eq_before < need))
    sel_bf = jnp.where(sel, 1.0, 0.0).astype(BF16)
    slot = jnp.where(sel, jnp.dot(sel_bf, tri, preferred_element_type=F32), -1.0)
    gate = jnp.where(sel, aff, 0.0)
    tt = lax.broadcasted_iota(jnp.int32, (n_seq, LANES), 0)
    ti = lax.broadcasted_iota(jnp.int32, (n_seq, LANES), 1)
    before = jnp.where(tt < ti * MOE_TILE, 1.0, 0.0).astype(BF16)
    off = jnp.dot(sel_bf, before, preferred_element_type=F32).astype(jnp.int32)
    pad = jnp.full((LANES - N_EXP, n_seq), -1.0, F32)
    for b in range(n_b):
        rows = slice(b * N_EXP, (b + 1) * N_EXP)
        slot_row_ref[b] = slot[rows, :]
        off_ref[b] = off[rows, :]
        slot_col_ref[b * n_seq:(b + 1) * n_seq, :] = jnp.transpose(jnp.concatenate([slot[rows, :], pad], axis=0))
        gate_col_ref[b * n_seq:(b + 1) * n_seq, :] = jnp.transpose(jnp.concatenate([gate[rows, :], pad * 0.0], axis=0))


def _route(logits, *, batch, n_seq, cap):
    n_b = 4 if batch % 4 == 0 else 1
    return pl.pallas_call(
        functools.partial(_route_kernel, n_seq=n_seq, cap=cap, n_b=n_b),
        out_shape=(jax.ShapeDtypeStruct((batch, N_EXP, n_seq), F32),
                   jax.ShapeDtypeStruct((batch * n_seq, LANES), F32),
                   jax.ShapeDtypeStruct((batch * n_seq, LANES), F32),
                   jax.ShapeDtypeStruct((batch, N_EXP, LANES), jnp.int32)),
        grid=(batch // n_b,),
        in_specs=[pl.BlockSpec((n_b * n_seq, LANES), lambda b: (b, 0))],
        out_specs=(pl.BlockSpec((n_b, N_EXP, n_seq), lambda b: (b, 0, 0)),
                   pl.BlockSpec((n_b * n_seq, LANES), lambda b: (b, 0)),
                   pl.BlockSpec((n_b * n_seq, LANES), lambda b: (b, 0)),
                   pl.BlockSpec((n_b, N_EXP, LANES), lambda b: (b, 0, 0))),
        scratch_shapes=[pltpu.VMEM((n_seq, n_seq), BF16)],
        compiler_params=_cparams(("arbitrary",)),
        name="ec_route",
    )(logits)


def _windows(off_ref, row0, n_exp, tile, cap):
    firsts, n_win = [], 0
    for e in range(n_exp):
        lo = off_ref[row0 + e * OFF_STRIDE + tile]
        hi = off_ref[row0 + e * OFF_STRIDE + tile + 1]
        first = (lo >> 4) << 4
        firsts.append(first)
        n_win = jnp.maximum(n_win, (hi - first + MOE_WIN - 1) // MOE_WIN)
    return firsts, n_win


def _dispatch_kernel(off_ref, slot_ref, h_ref, o_ref, *, e_blk, cap):
    b, half = pl.program_id(0), pl.program_id(1)
    o_ref[...] = jnp.zeros(o_ref.shape, o_ref.dtype)
    siota = lax.broadcasted_iota(jnp.int32, (MOE_WIN, 1), 0)
    n_cols = h_ref.shape[1]
    chunk = 512

    def token_tile(tile, carry):
        firsts, n_win = _windows(off_ref, (b * N_EXP + half * e_blk) * OFF_STRIDE, e_blk, tile, cap)
        t0 = pl.multiple_of(tile * MOE_TILE, MOE_TILE)

        def window(w, carry):
            bases, parts = [], []
            for e in range(e_blk):
                start = firsts[e] + w * MOE_WIN
                base = pl.multiple_of(jnp.minimum(start, cap - MOE_WIN), 16)
                srow = slot_ref[e, tile]
                srow = jnp.where(srow >= start.astype(F32), srow, -1.0)
                parts.append(jnp.where(srow == (base + siota).astype(F32), 1.0, 0.0).astype(BF16))
                bases.append(base)
            onehot = jnp.concatenate(parts, axis=0)
            for c0 in range(0, n_cols, chunk):
                res = jnp.dot(onehot, h_ref[pl.ds(t0, MOE_TILE), c0:c0 + chunk],
                              preferred_element_type=F32).astype(BF16)
                for e in range(e_blk):
                    rows = pl.ds(bases[e], MOE_WIN)
                    o_ref[e, rows, c0:c0 + chunk] = (o_ref[e, rows, c0:c0 + chunk]
                                                     + res[e * MOE_WIN:(e + 1) * MOE_WIN, :])
            return carry

        return lax.fori_loop(0, n_win, window, carry)

    lax.fori_loop(0, slot_ref.shape[1], token_tile, 0)


def _dispatch(off, slot_row, h2, *, batch, n_seq, cap):
    e_blk = 8
    tiles = n_seq // MOE_TILE
    return pl.pallas_call(
        functools.partial(_dispatch_kernel, e_blk=e_blk, cap=cap),
        out_shape=jax.ShapeDtypeStruct((N_EXP, batch * cap, D), BF16),
        grid_spec=pltpu.PrefetchScalarGridSpec(
            num_scalar_prefetch=1,
            grid=(batch, N_EXP // e_blk),
            in_specs=[
                pl.BlockSpec((None, e_blk, tiles, 1, MOE_TILE), lambda b, j, off: (b, j, 0, 0, 0)),
                pl.BlockSpec((n_seq, D), lambda b, j, off: (b, 0)),
            ],
            out_specs=pl.BlockSpec((e_blk, cap, D), lambda b, j, off: (j, b, 0)),
        ),
        compiler_params=_cparams(("arbitrary", "arbitrary")),
        name="ec_dispatch",
    )(off, slot_row.reshape(batch, N_EXP, tiles, 1, MOE_TILE), h2)


def _gate_up_kernel(x_ref, wg_ref, wu_ref, o_ref):
    x = x_ref[...]
    part = 256
    for c0 in range(0, o_ref.shape[1], part):
        g = jnp.dot(x, wg_ref[:, c0:c0 + part].astype(BF16), preferred_element_type=F32)
        u = jnp.dot(x, wu_ref[:, c0:c0 + part].astype(BF16), preferred_element_type=F32)
        o_ref[:, c0:c0 + part] = (jax.nn.silu(g) * u).astype(BF16)


def _down_kernel(h_ref, wd_ref, o_ref):
    y = jnp.dot(h_ref[...], wd_ref[...].astype(BF16), preferred_element_type=F32).astype(BF16)
    cap = o_ref.shape[1]
    for b in range(o_ref.shape[0]):
        o_ref[b] = y[b * cap:(b + 1) * cap, :]


def _experts(xs, w_gate, w_up, w_down, layer, *, batch, cap):
    m = batch * cap
    tn, tn_down = TN_UP, TN_DOWN
    hid = pl.pallas_call(
        _gate_up_kernel,
        out_shape=jax.ShapeDtypeStruct((N_EXP, m, EXP_FF), BF16),
        grid=(N_EXP, EXP_FF // tn),
        in_specs=[
            pl.BlockSpec((None, m, D), lambda e, j: (e, 0, 0)),
            pl.BlockSpec((None, None, D, tn), lambda e, j: (layer, e, 0, j)),
            pl.BlockSpec((None, None, D, tn), lambda e, j: (layer, e, 0, j)),
        ],
        out_specs=pl.BlockSpec((None, m, tn), lambda e, j: (e, 0, j)),
        compiler_params=_cparams(("arbitrary", "arbitrary")),
        name="expert_gate_up",
    )(xs, w_gate, w_up)
    return pl.pallas_call(
        _down_kernel,
        out_shape=jax.ShapeDtypeStruct((batch, N_EXP * cap, D), BF16),
        grid=(N_EXP, D // tn_down),
        in_specs=[
            pl.BlockSpec((None, m, EXP_FF), lambda e, j: (e, 0, 0)),
            pl.BlockSpec((None, None, EXP_FF, tn_down), lambda e, j: (layer, e, 0, j)),
        ],
        out_specs=pl.BlockSpec((batch, cap, tn_down), lambda e, j: (0, e, j)),
        compiler_params=_cparams(("arbitrary", "arbitrary")),
        name="expert_down",
    )(hid, w_down)


def _combine_kernel(off_ref, ye_ref, slot_ref, gate_ref, x_ref, g2_ref, lg_ref, lb_ref, o_ref, moe_ref, *, cap):
    b, tile = pl.program_id(0), pl.program_id(1)
    firsts, n_win = _windows(off_ref, b * N_EXP * OFF_STRIDE, N_EXP, tile, cap)
    lane_e = lax.broadcasted_iota(jnp.int32, (LANES, N_EXP * MOE_WIN), 1) // MOE_WIN
    spread = jnp.where(lax.broadcasted_iota(jnp.int32, (LANES, N_EXP * MOE_WIN), 0) == lane_e, 1.0, 0.0).astype(BF16)
    slot = jnp.dot(slot_ref[...].astype(BF16), spread, preferred_element_type=F32)
    gate = jnp.dot(gate_ref[...].astype(BF16), spread, preferred_element_type=F32).astype(BF16)
    siota = lax.broadcasted_iota(jnp.int32, (1, MOE_WIN), 1)

    def window(w):
        starts, targets, rows = [], [], []
        for e in range(N_EXP):
            start = firsts[e] + w * MOE_WIN
            base = pl.multiple_of(jnp.minimum(start, cap - MOE_WIN), 16)
            starts.append(jnp.full((1, MOE_WIN), start, jnp.int32))
            targets.append(base + siota)
            rows.append(ye_ref[pl.ds(e * cap + base, MOE_WIN), :])
        start_l = jnp.concatenate(starts, axis=1).astype(F32)
        target_l = jnp.concatenate(targets, axis=1).astype(F32)
        comb = jnp.where((slot == target_l) & (slot >= start_l), gate, jnp.zeros_like(gate))
        return jnp.dot(comb, jnp.concatenate(rows, axis=0), preferred_element_type=F32)

    moe_ref[...] = window(0)

    @pl.when(n_win > 1)
    def _():
        def more(w, carry):
            moe_ref[...] += window(w)
            return carry
        lax.fori_loop(1, n_win, more, 0)

    o_ref[...] = _ln(ALPHA * x_ref[...] + g2_ref[...] * moe_ref[...]) * lg_ref[...] + lb_ref[...]


def _combine(off, ye, slot_col, gate_col, x1, mod5, layer, ln_g, ln_b, *, batch, n_seq, cap):
    tm = MOE_TILE
    tiles = n_seq // tm
    vec_spec = pl.BlockSpec((1, D), lambda b, i, off: (0, 0))
    row_spec = lambda w: pl.BlockSpec((tm, w), lambda b, i, off: (b * tiles + i, 0))
    return pl.pallas_call(
        functools.partial(_combine_kernel, cap=cap),
        out_shape=jax.ShapeDtypeStruct((batch * n_seq, D), F32),
        grid_spec=pltpu.PrefetchScalarGridSpec(
            num_scalar_prefetch=1,
            grid=(batch, tiles),
            in_specs=[
                pl.BlockSpec((None, N_EXP * cap, D), lambda b, i, off: (b, 0, 0)),
                row_spec(LANES), row_spec(LANES), row_spec(D),
                pl.BlockSpec((None, None, None, 1, D), lambda b, i, off: (layer, b, 5, 0, 0)),
                vec_spec, vec_spec,
            ],
            out_specs=row_spec(D),
            scratch_shapes=[pltpu.VMEM((tm, D), F32)],
        ),
        compiler_params=_cparams(("arbitrary", "arbitrary")),
        name="ec_combine_ln",
    )(off, ye, slot_col, gate_col, x1, mod5, ln_g.reshape(1, D), ln_b.reshape(1, D))


def _moe_block(x1, h2, logits, mod5, layer, w_gate, w_up, w_down, ln_g, ln_b, *, batch, n_seq):
    cap = 2 * n_seq // N_EXP
    assert n_seq % MOE_TILE == 0 and n_seq // MOE_TILE < OFF_STRIDE and cap % MOE_WIN == 0
    assert cap <= 256
    slot_row, slot_col, gate_col, off = _route(logits, batch=batch, n_seq=n_seq, cap=cap)
    off = off[:, :, :OFF_STRIDE].reshape(batch * N_EXP * OFF_STRIDE)
    xs = _dispatch(off, slot_row, h2, batch=batch, n_seq=n_seq, cap=cap)
    ye = _experts(xs, w_gate, w_up, w_down, layer, batch=batch, cap=cap)
    return _combine(off, ye, slot_col, gate_col, x1, mod5, layer, ln_g, ln_b, batch=batch, n_seq=n_seq, cap=cap)


def _cd_in_kernel(x_ref, sh_ref, sc_ref, w_ref, lg_ref, lb_ref, u_ref, vg_ref, xg_ref):
    h = (_ln(x_ref[...]) * (1.0 + sc_ref[...]) + sh_ref[...]).astype(BF16)
    chunk = 512
    for c0 in range(0, SG_W, chunk):
        u_ref[:, c0:c0 + chunk] = jax.nn.gelu(
            jnp.dot(h, w_ref[:, c0:c0 + chunk], preferred_element_type=F32)).astype(BF16)
    for c0 in range(0, SG_W, chunk):
        z = jax.nn.gelu(jnp.dot(h, w_ref[:, SG_W + c0:SG_W + c0 + chunk], preferred_element_type=F32))
        parts = [_ln(z[:, j * HEAD:(j + 1) * HEAD]) for j in range(chunk // HEAD)]
        vg = jnp.concatenate(parts, axis=1) * lg_ref[:, c0:c0 + chunk] + lb_ref[:, c0:c0 + chunk]
        vg_ref[:, c0:c0 + chunk] = vg.astype(BF16)
    for c0 in range(0, CONV_CH, chunk):
        a = jnp.dot(h, w_ref[:, 2 * SG_W + c0:2 * SG_W + c0 + chunk], preferred_element_type=F32)
        gt = jnp.dot(h, w_ref[:, 2 * SG_W + CONV_CH + c0:2 * SG_W + CONV_CH + c0 + chunk],
                     preferred_element_type=F32)
        xg_ref[:, c0:c0 + chunk] = a * jax.nn.sigmoid(gt)


def _cd_in_proj(x2d, mod5, layer, w_bf, sg_ln_g, sg_ln_b, *, n_seq):
    rows = x2d.shape[0]
    tm = 256
    tiles_per_seq = n_seq // tm
    mod_spec = lambda chunk: pl.BlockSpec((None, None, None, 1, D),
                                          lambda i: (layer, i // tiles_per_seq, chunk, 0, 0))
    vec_spec = pl.BlockSpec((1, SG_W), lambda i: (0, 0))
    row_spec = lambda w: pl.BlockSpec((tm, w), lambda i: (i, 0))
    return pl.pallas_call(
        _cd_in_kernel,
        out_shape=(jax.ShapeDtypeStruct((rows, SG_W), BF16),
                   jax.ShapeDtypeStruct((rows, SG_W), BF16),
                   jax.ShapeDtypeStruct((rows, CONV_CH), F32)),
        grid=(rows // tm,),
        in_specs=[row_spec(D), mod_spec(0), mod_spec(1),
                  pl.BlockSpec((D, CD_IN), lambda i: (0, 0)), vec_spec, vec_spec],
        out_specs=(row_spec(SG_W), row_spec(SG_W), row_spec(CONV_CH)),
        compiler_params=_cparams(("arbitrary",)),
        name="cd_in_proj",
    )(x2d, mod5, mod5, w_bf, sg_ln_g.reshape(1, SG_W), sg_ln_b.reshape(1, SG_W))


CONV_HALO = 16
CONV_ROWS = 32


def _cd_mix_kernel(u_ref, vg_ref, xp_ref, xc_ref, xn_ref, sgw_ref, sgb_ref, cw_ref, cb_ref, lg_ref, lb_ref,
                   o_ref, sh_ref, cv_ref, *, tiles_per_seq):
    i = pl.program_id(1)
    tm = xc_ref.shape[0]
    for c in range(tm // CHUNK):
        for g in range(SG_G):
            rs, cs = slice(c * CHUNK, (c + 1) * CHUNK), slice(g * HEAD, (g + 1) * HEAD)
            sp = jnp.dot(sgw_ref[g], vg_ref[rs, cs], preferred_element_type=F32) + sgb_ref[g]
            o_ref[rs, cs] = (u_ref[rs, cs].astype(F32) * sp).astype(BF16)
    sh_ref[0, 0:CONV_HALO, :] = jnp.where(i > 0, xp_ref[...], 0.0)
    sh_ref[0, CONV_HALO:CONV_HALO + tm, :] = xc_ref[...]
    sh_ref[0, CONV_HALO + tm:, :] = jnp.where(i < tiles_per_seq - 1, xn_ref[...], 0.0)
    keep = tm + 2 * CONV_HALO - SUBLANES
    for s in range(1, SUBLANES):
        sh_ref[s, 0:keep, :] = sh_ref[0, s:s + keep, :]
    off = CONV_HALO - CONV_K // 2
    for r0 in range(0, tm, CONV_ROWS):
        for c0 in range(0, CONV_CH, LANES):
            acc = jnp.zeros((CONV_ROWS, LANES), F32) + cb_ref[:, c0:c0 + LANES]
            for k in range(CONV_K):
                a = r0 + off + k
                acc = acc + (sh_ref[a % SUBLANES, a - a % SUBLANES:a - a % SUBLANES + CONV_ROWS, c0:c0 + LANES]
                             * cw_ref[k:k + 1, c0:c0 + LANES])
            cv_ref[r0:r0 + CONV_ROWS, c0:c0 + LANES] = acc
    y = _ln(cv_ref[...]) * lg_ref[...] + lb_ref[...]
    o_ref[:, SG_W:] = jax.nn.silu(y).astype(BF16)


def _cd_mix(u, vg, xg, sg_w, sg_b, conv_w, conv_b, conv_ln_g, conv_ln_b, *, batch, n_seq):
    tm = 256
    tiles = n_seq // tm
    hpt = tm // CONV_HALO
    n_halo = batch * n_seq // CONV_HALO
    row_spec = lambda w: pl.BlockSpec((tm, w), lambda b, i: (b * tiles + i, 0))
    vec_spec = pl.BlockSpec((1, CONV_CH), lambda b, i: (0, 0))
    sgb_full = jnp.broadcast_to(sg_b[:, :, None], (SG_G, CHUNK, HEAD))
    return pl.pallas_call(
        functools.partial(_cd_mix_kernel, tiles_per_seq=tiles),
        out_shape=jax.ShapeDtypeStruct((batch * n_seq, D), BF16),
        grid=(batch, tiles),
        in_specs=[
            row_spec(SG_W), row_spec(SG_W),
            pl.BlockSpec((CONV_HALO, CONV_CH), lambda b, i: (jnp.maximum((b * tiles + i) * hpt - 1, 0), 0)),
            row_spec(CONV_CH),
            pl.BlockSpec((CONV_HALO, CONV_CH), lambda b, i: (jnp.minimum((b * tiles + i + 1) * hpt, n_halo - 1), 0)),
            pl.BlockSpec((SG_G, CHUNK, CHUNK), lambda b, i: (0, 0, 0)),
            pl.BlockSpec((SG_G, CHUNK, HEAD), lambda b, i: (0, 0, 0)),
            pl.BlockSpec((CONV_K + 1, CONV_CH), lambda b, i: (0, 0)),
            vec_spec, vec_spec, vec_spec,
        ],
        out_specs=pl.BlockSpec((tm, D), lambda b, i: (b * tiles + i, 0)),
        scratch_shapes=[pltpu.VMEM((SUBLANES, tm + 2 * CONV_HALO, CONV_CH), F32), pltpu.VMEM((tm, CONV_CH), F32)],
        compiler_params=_cparams(("arbitrary", "arbitrary")),
        name="cd_mix",
    )(u, vg, xg, xg, xg, sg_w.astype(BF16), sgb_full,
      jnp.pad(conv_w.reshape(CONV_K, CONV_CH), ((0, 1), (0, 0))),
      conv_b.reshape(1, CONV_CH), conv_ln_g.reshape(1, CONV_CH), conv_ln_b.reshape(1, CONV_CH))


def kernel(x, c, ctx, c_ctx, w_mod, b_mod, ln1_g, ln1_b, ln2_g, ln2_b, w_router, w_gate, w_up, w_down,
           ab_w_in, ab_w_out, sink, cd_w_in, cd_w_out, sg_ln_g, sg_ln_b, sg_w, sg_b,
           conv_w, conv_b, conv_ln_g, conv_ln_b):
    batch, n_seq, _ = x.shape
    n_ctx = ctx.shape[1]
    assert x.shape[2] == D and batch + 1 <= MOD_ROWS

    c_pad = jnp.concatenate([c, c_ctx[None, :], jnp.zeros((MOD_ROWS - batch - 1, D), F32)], axis=0)
    mod = _modulation(c_pad, w_mod, b_mod)
    mod5 = mod.reshape(DEPTH, MOD_ROWS, 6, 1, D)
    w_router_pad = jnp.pad(w_router, ((0, 0), (0, 0), (0, LANES - N_EXP))).astype(BF16)

    x0 = x.reshape(batch * n_seq, D)
    ctx2d = ctx.reshape(batch * n_ctx, D)

    w_in = ab_w_in[0].astype(BF16)
    tabs = _rope_tables(n_seq)
    p_lat = _ab_in_proj(x0, mod5, 0, lambda b: b, w_in, tabs, n_seq=n_seq,
                        rope_cols=K_END, q_cols=Q_END, name="ab_in_proj")
    p_ctx = _ab_in_proj(ctx2d, mod5, 0, lambda b: batch, w_in[:, Q_END:V_END], tabs, n_seq=n_ctx,
                        rope_cols=0, q_cols=0, name="ab_ctx_proj")
    attn = _attention(p_lat, p_ctx, sink[0], batch=batch, n_seq=n_seq, n_ctx=n_ctx)
    four = _fourier(p_lat, batch=batch, n_seq=n_seq)
    x1, h2, logits = _out_proj([attn, four], ab_w_out[0].astype(BF16), x0, mod5, 0, ln1_g[0], ln1_b[0],
                               w_router_pad[0], n_seq=n_seq, folded_last=True)
    x2 = _moe_block(x1, h2, logits, mod5, 0, w_gate, w_up, w_down, ln2_g[0], ln2_b[0], batch=batch, n_seq=n_seq)

    u, vg, xg = _cd_in_proj(x2, mod5, 1, cd_w_in[0].astype(BF16), sg_ln_g[0], sg_ln_b[0], n_seq=n_seq)
    mix = _cd_mix(u, vg, xg, sg_w[0], sg_b[0], conv_w[0], conv_b[0], conv_ln_g[0], conv_ln_b[0],
                  batch=batch, n_seq=n_seq)
    x3, h4, logits1 = _out_proj([mix], cd_w_out[0].astype(BF16), x2, mod5, 1, ln1_g[1], ln1_b[1],
                                w_router_pad[1], n_seq=n_seq)
    x4 = _moe_block(x3, h4, logits1, mod5, 1, w_gate, w_up, w_down, ln2_g[1], ln2_b[1], batch=batch, n_seq=n_seq)
    return x4.reshape(batch, n_seq, D)
```

```python
import functools

import jax
import jax.numpy as jnp
from jax import lax
from jax.experimental import pallas as pl
from jax.experimental.pallas import tpu as pltpu

F32 = jnp.float32
BF16 = jnp.bfloat16

D = 2048
HEAD = 128
N_Q = 12
N_KV = 4
GQA = 3
WINDOW = 128
GRID_W = 64
ROPE_BASE = 10000.0
Q_END = N_Q * HEAD
KV_W = N_KV * HEAD
K_END = Q_END + KV_W
V_END = K_END + KV_W
FOUR_G = 4
FOUR_W = FOUR_G * HEAD
AB_IN = V_END + FOUR_W
SG_G = 8
SG_W = SG_G * HEAD
CHUNK = 128
CONV_CH = 1024
CONV_K = 31
CD_IN = 4096
N_EXP = 16
EXP_FF = 2048
DEPTH = 2
ALPHA = (2 * DEPTH) ** 0.25
LN_EPS = 1e-6
ATTN_SCALE = HEAD ** -0.5
MOD_ROWS = 16
LANES = 128
SUBLANES = 8
VMEM_LIMIT = 56 * 1024 * 1024
MOE_TILE = 256
MOE_WIN = 64
OFF_STRIDE = 16
TN_UP = 512
TN_DOWN = 512


def _cparams(sem):
    return pltpu.CompilerParams(dimension_semantics=sem, vmem_limit_bytes=VMEM_LIMIT)


def _ln(x):
    mu = jnp.mean(x, axis=-1, keepdims=True)
    xc = x - mu
    var = jnp.mean(xc * xc, axis=-1, keepdims=True)
    return xc * lax.rsqrt(var + LN_EPS)


def _mod_kernel(c_ref, w_ref, b_ref, o_ref):
    s = jax.nn.silu(c_ref[...]).astype(BF16)
    o_ref[...] = jnp.dot(s, w_ref[...].astype(BF16), preferred_element_type=F32) + b_ref[...]


def _modulation(c_pad, w_mod, b_mod):
    tn = 1024
    return pl.pallas_call(
        _mod_kernel,
        out_shape=jax.ShapeDtypeStruct((DEPTH, MOD_ROWS, 6 * D), F32),
        grid=(DEPTH, 6 * D // tn),
        in_specs=[
            pl.BlockSpec((MOD_ROWS, D), lambda l, j: (0, 0)),
            pl.BlockSpec((None, D, tn), lambda l, j: (l, 0, j)),
            pl.BlockSpec((None, 1, tn), lambda l, j: (l, 0, j)),
        ],
        out_specs=pl.BlockSpec((None, MOD_ROWS, tn), lambda l, j: (l, 0, j)),
        compiler_params=_cparams(("arbitrary", "arbitrary")),
        name="modulation",
    )(c_pad, w_mod, b_mod.reshape(DEPTH, 1, 6 * D))


def _rope(a, cos, sa, sb):
    return a * cos + pltpu.roll(a, HEAD - 32, 1) * sa + pltpu.roll(a, 32, 1) * sb


def _ab_in_kernel(x_ref, sh_ref, sc_ref, wf_ref, cos_ref, sa_ref, sb_ref, o_ref, w_ref, *, n_cols, rope_cols, q_cols):
    @pl.when(pl.program_id(0) == 0)
    def _():
        for r0 in range(0, D, 256):
            w_ref[r0:r0 + 256, :] = wf_ref[r0:r0 + 256, :].astype(BF16)

    h = (_ln(x_ref[...]) * (1.0 + sc_ref[...]) + sh_ref[...]).astype(BF16)
    chunk = 512
    for c0 in range(0, n_cols, chunk):
        acc = jnp.dot(h, w_ref[:, c0:c0 + chunk], preferred_element_type=F32)
        if c0 < rope_cols:
            cos, sa, sb = cos_ref[...], sa_ref[...], sb_ref[...]
            parts = []
            for j in range(chunk // HEAD):
                r = _rope(acc[:, j * HEAD:(j + 1) * HEAD], cos, sa, sb)
                parts.append(r * ATTN_SCALE if c0 < q_cols else r)
            acc = jnp.concatenate(parts, axis=1)
        o_ref[:, c0:c0 + chunk] = acc.astype(BF16)


def _ab_in_proj(x2d, mod5, layer, row_of_tile, w, rope_tabs, *, n_seq, rope_cols, q_cols, name):
    rows, _ = x2d.shape
    n_cols = w.shape[1]
    tm = 256
    tiles_per_seq = n_seq // tm
    cos, sa, sb = rope_tabs
    kern = functools.partial(_ab_in_kernel, n_cols=n_cols, rope_cols=rope_cols, q_cols=q_cols)
    mod_spec = lambda chunk: pl.BlockSpec((None, None, None, 1, D),
                                          lambda i: (layer, row_of_tile(i // tiles_per_seq), chunk, 0, 0))
    tab_spec = pl.BlockSpec((tm, HEAD), lambda i: (i % tiles_per_seq, 0))
    return pl.pallas_call(
        kern,
        out_shape=jax.ShapeDtypeStruct((rows, n_cols), BF16),
        grid=(rows // tm,),
        in_specs=[
            pl.BlockSpec((tm, D), lambda i: (i, 0)),
            mod_spec(0), mod_spec(1),
            pl.BlockSpec((D, n_cols), lambda i: (0, 0), pipeline_mode=pl.Buffered(1)),
            tab_spec, tab_spec, tab_spec,
        ],
        out_specs=pl.BlockSpec((tm, n_cols), lambda i: (i, 0)),
        scratch_shapes=[pltpu.VMEM((D, n_cols), BF16)],
        compiler_params=_cparams(("arbitrary",)),
        name=name,
    )(x2d, mod5, mod5, w, cos, sa, sb)


def _rope_tables(n):
    quarter = HEAD // 4
    t = jnp.arange(n)
    r = (t // GRID_W).astype(F32)
    col = (t % GRID_W).astype(F32)
    inv = ROPE_BASE ** (-jnp.arange(quarter, dtype=F32) / quarter)
    ang_r, ang_c = r[:, None] * inv, col[:, None] * inv
    zero = jnp.zeros_like(ang_r)
    cos = jnp.concatenate([jnp.cos(ang_r)] * 2 + [jnp.cos(ang_c)] * 2, axis=1)
    sa = jnp.concatenate([-jnp.sin(ang_r), zero, -jnp.sin(ang_c), zero], axis=1)
    sb = jnp.concatenate([zero, jnp.sin(ang_r), zero, jnp.sin(ang_c)], axis=1)
    return cos, sa, sb


def _attn_kernel(sink_ref, q_ref, k_ref, v_ref, kc_ref, vc_ref, o_ref, bias_ref, *, n_seq):
    hk = pl.program_id(1)
    kc = kc_ref[...]
    vc = vc_ref[...]
    kw_len = 3 * WINDOW
    rows = GQA * WINDOW
    n_blocks = n_seq // WINDOW
    row = lax.broadcasted_iota(jnp.int32, (rows, 1), 0)
    sink = jnp.where(row < WINDOW, sink_ref[hk * GQA],
                     jnp.where(row < 2 * WINDOW, sink_ref[hk * GQA + 1], sink_ref[hk * GQA + 2]))
    rel = (row & (WINDOW - 1)) - lax.broadcasted_iota(jnp.int32, (1, kw_len), 1)
    for i in range(3):
        bias_ref[i] = jnp.where(jnp.abs(rel + i * WINDOW) <= WINDOW, 0.0, -1e30)
    dn = (((1,), (1,)), ((), ()))

    def block(n, back):
        r0 = n * WINDOW if isinstance(n, int) else pl.multiple_of(n * WINDOW, WINDOW)
        start = r0 - back * WINDOW
        if not isinstance(start, int):
            start = pl.multiple_of(start, WINDOW)
        qs = q_ref[pl.ds(r0, WINDOW), :]
        q3 = jnp.concatenate([qs[:, g * HEAD:(g + 1) * HEAD] for g in range(GQA)], axis=0)
        kw = k_ref[pl.ds(start, kw_len), :]
        vw = v_ref[pl.ds(start, kw_len), :]
        s_w = lax.dot_general(q3, kw, dn, preferred_element_type=F32) + bias_ref[back]
        s_c = lax.dot_general(q3, kc, dn, preferred_element_type=F32)
        m = jnp.maximum(jnp.maximum(jnp.max(s_w, axis=1, keepdims=True), jnp.max(s_c, axis=1, keepdims=True)), sink)
        p_w = jnp.exp(s_w - m)
        p_c = jnp.exp(s_c - m)
        den = jnp.sum(p_w, axis=1, keepdims=True) + jnp.sum(p_c, axis=1, keepdims=True) + jnp.exp(sink - m)
        o = (jnp.dot(p_w.astype(BF16), vw, preferred_element_type=F32)
             + jnp.dot(p_c.astype(BF16), vc, preferred_element_type=F32)) / den
        o_ref[pl.ds(r0, WINDOW), :] = jnp.concatenate(
            [o[g * WINDOW:(g + 1) * WINDOW, :] for g in range(GQA)], axis=1).astype(BF16)

    block(0, 0)
    block(n_blocks - 1, 2)

    per_iter = max(g for g in (7, 2, 1) if (n_blocks - 2) % g == 0)

    def interior(i, carry):
        for g in range(per_iter):
            block(1 + per_iter * i + g, 1)
        return carry

    lax.fori_loop(0, (n_blocks - 2) // per_iter, interior, 0)


def _attention(p_lat, p_ctx, sink, *, batch, n_seq, n_ctx):
    qw = GQA * HEAD
    assert n_seq % WINDOW == 0 and n_seq >= 3 * WINDOW
    return pl.pallas_call(
        functools.partial(_attn_kernel, n_seq=n_seq),
        out_shape=jax.ShapeDtypeStruct((batch * n_seq, Q_END), BF16),
        grid=(batch, N_KV),
        in_specs=[
            pl.BlockSpec(memory_space=pltpu.SMEM),
            pl.BlockSpec((n_seq, qw), lambda b, h: (b, h)),
            pl.BlockSpec((n_seq, HEAD), lambda b, h: (b, Q_END // HEAD + h)),
            pl.BlockSpec((n_seq, HEAD), lambda b, h: (b, K_END // HEAD + h)),
            pl.BlockSpec((n_ctx, HEAD), lambda b, h: (b, h)),
            pl.BlockSpec((n_ctx, HEAD), lambda b, h: (b, N_KV + h)),
        ],
        out_specs=pl.BlockSpec((n_seq, qw), lambda b, h: (b, h)),
        scratch_shapes=[pltpu.VMEM((3, GQA * WINDOW, 3 * WINDOW), F32)],
        compiler_params=_cparams(("arbitrary", "arbitrary")),
        name="banded_gqa",
    )(sink, p_lat, p_lat, p_lat, p_ctx, p_ctx)


DFT_ROWS = 32


def _fourier_kernel(z_ref, cs_ref, ca_ref, sa_ref, cb_ref, sb_ref, o_ref, cn_ref, sn_ref):
    n = z_ref.shape[0]
    half = n // 2

    @pl.when(pl.program_id(0) == 0)
    def _():
        cb, sb = cb_ref[...], sb_ref[...]

        def rows(a, carry):
            ca, sa = ca_ref[a], sa_ref[a]
            r = pl.ds(pl.multiple_of(a * DFT_ROWS, DFT_ROWS), DFT_ROWS)
            cn_ref[r, :] = (ca * cb - sa * sb).astype(BF16)
            sn_ref[r, :] = (sa * cb + ca * sb).astype(BF16)
            return carry

        lax.fori_loop(0, ca_ref.shape[0], rows, 0)

    cs = cs_ref[...]
    a_parts, b_parts = [], []
    for g in range(FOUR_G):
        ab = jnp.dot(z_ref[:, g * HEAD:(g + 1) * HEAD], cs, preferred_element_type=F32)
        a_parts.append(ab[:, :HEAD])
        b_parts.append(ab[:, HEAD:])
    a = jnp.concatenate(a_parts, axis=1).astype(BF16)
    b = jnp.concatenate(b_parts, axis=1).astype(BF16)
    p = jnp.dot(cn_ref[...], a, preferred_element_type=F32)
    q = jnp.dot(sn_ref[...], b, preferred_element_type=F32)
    scale = 1.0 / (n * HEAD) ** 0.5
    o_ref[0:half, :] = ((p[0:half, :] - q[0:half, :]) * scale).astype(BF16)
    o_ref[half:n, :] = ((p[1:half + 1, :] + q[1:half + 1, :]) * scale).astype(BF16)


def _dft_rows(j, m):
    ang = ((j[:, None] * jnp.arange(m, dtype=jnp.int32)[None, :]) % m).astype(F32) * (2.0 * jnp.pi / m)
    return jnp.cos(ang), jnp.sin(ang)


def _fourier(p_lat, *, batch, n_seq):
    assert n_seq % (2 * DFT_ROWS) == 0
    n_a = n_seq // (2 * DFT_ROWS) + 1
    cc, sc = _dft_rows(jnp.arange(HEAD, dtype=jnp.int32), HEAD)
    cs = jnp.concatenate([cc, sc], axis=1).astype(BF16)
    ca, sa = _dft_rows(jnp.arange(n_a, dtype=jnp.int32) * DFT_ROWS, n_seq)
    cb, sb = _dft_rows(jnp.arange(DFT_ROWS, dtype=jnp.int32), n_seq)
    coarse_spec = pl.BlockSpec((n_a, 1, n_seq), lambda b: (0, 0, 0))
    fine_spec = pl.BlockSpec((DFT_ROWS, n_seq), lambda b: (0, 0))
    return pl.pallas_call(
        _fourier_kernel,
        out_shape=jax.ShapeDtypeStruct((batch * n_seq, FOUR_W), BF16),
        grid=(batch,),
        in_specs=[
            pl.BlockSpec((n_seq, FOUR_W), lambda b: (b, V_END // FOUR_W)),
            pl.BlockSpec((HEAD, 2 * HEAD), lambda b: (0, 0)),
            coarse_spec, coarse_spec, fine_spec, fine_spec,
        ],
        out_specs=pl.BlockSpec((n_seq, FOUR_W), lambda b: (b, 0)),
        scratch_shapes=[pltpu.VMEM((n_a * DFT_ROWS, n_seq), BF16), pltpu.VMEM((n_a * DFT_ROWS, n_seq), BF16)],
        compiler_params=_cparams(("arbitrary",)),
        name="fourier_mix",
    )(p_lat, cs, ca.reshape(n_a, 1, n_seq), sa.reshape(n_a, 1, n_seq), cb, sb)


def _out_proj_kernel(*refs, n_parts, folded):
    a_refs = refs[:n_parts]
    rest = refs[n_parts:]
    if folded:
        flip_ref, rest = rest[0], rest[1:]
    (wf_ref, x_ref, g1_ref, lg_ref, lb_ref, sh_ref, sc_ref, wr_ref, x1_ref, h2_ref, lo_ref, w_ref) = rest

    @pl.when(pl.program_id(0) == 0)
    def _():
        for r0 in range(0, D, 256):
            w_ref[r0:r0 + 256, :] = wf_ref[r0:r0 + 256, :].astype(BF16)

    y = None
    k0 = 0
    for p, a_ref in enumerate(a_refs):
        kk = a_ref.shape[1]
        a = a_ref[...]
        if folded and p == n_parts - 1:
            a = jnp.dot(flip_ref[...], a, preferred_element_type=F32).astype(BF16)
        t = jnp.dot(a, w_ref[k0:k0 + kk, :], preferred_element_type=F32)
        y = t if y is None else y + t
        k0 += kk
    x1 = _ln(ALPHA * x_ref[...] + g1_ref[...] * y) * lg_ref[...] + lb_ref[...]
    x1_ref[...] = x1
    h2 = (_ln(x1) * (1.0 + sc_ref[...]) + sh_ref[...]).astype(BF16)
    h2_ref[...] = h2
    lo_ref[...] = jnp.dot(h2, wr_ref[...], preferred_element_type=F32)


def _out_proj(parts, w, x2d, mod5, layer, ln_g, ln_b, w_router_pad, *, n_seq, folded_last=False):
    rows = x2d.shape[0]
    tm = 256
    tiles_per_seq = n_seq // tm
    half_tiles = tiles_per_seq // 2
    mod_spec = lambda chunk: pl.BlockSpec((None, None, None, 1, D),
                                          lambda i: (layer, i // tiles_per_seq, chunk, 0, 0))
    vec_spec = pl.BlockSpec((1, D), lambda i: (0, 0))
    in_specs = [pl.BlockSpec((tm, p.shape[1]), lambda i: (i, 0)) for p in parts]
    extra = []
    if folded_last:
        assert tiles_per_seq % 2 == 0

        def folded_tile(i):
            li = i % tiles_per_seq
            return i - li + jnp.where(li < half_tiles, li, half_tiles + tiles_per_seq - 1 - li)

        in_specs[-1] = pl.BlockSpec((tm, parts[-1].shape[1]), lambda i: (folded_tile(i), 0))
        in_specs.append(pl.BlockSpec((None, tm, tm), lambda i: ((i % tiles_per_seq) // half_tiles, 0, 0)))
        eye = jnp.eye(tm, dtype=BF16)
        extra = [jnp.stack([eye, eye[::-1]])]
    in_specs += [
        pl.BlockSpec((D, D), lambda i: (0, 0), pipeline_mode=pl.Buffered(1)),
        pl.BlockSpec((tm, D), lambda i: (i, 0)),
        mod_spec(2), vec_spec, vec_spec, mod_spec(3), mod_spec(4),
        pl.BlockSpec((D, LANES), lambda i: (0, 0)),
    ]
    return pl.pallas_call(
        functools.partial(_out_proj_kernel, n_parts=len(parts), folded=folded_last),
        out_shape=(jax.ShapeDtypeStruct((rows, D), F32),
                   jax.ShapeDtypeStruct((rows, D), BF16),
                   jax.ShapeDtypeStruct((rows, LANES), F32)),
        grid=(rows // tm,),
        in_specs=in_specs,
        out_specs=(pl.BlockSpec((tm, D), lambda i: (i, 0)),
                   pl.BlockSpec((tm, D), lambda i: (i, 0)),
                   pl.BlockSpec((tm, LANES), lambda i: (i, 0))),
        scratch_shapes=[pltpu.VMEM((D, D), BF16)],
        compiler_params=_cparams(("arbitrary",)),
        name="out_proj_ln",
    )(*parts, *extra, w, x2d, mod5, ln_g.reshape(1, D), ln_b.reshape(1, D), mod5, mod5, w_router_pad)


def _route_kernel(lo_ref, slot_row_ref, slot_col_ref, gate_col_ref, off_ref, tri_ref, *, n_seq, cap, n_b):
    @pl.when(pl.program_id(0) == 0)
    def _():
        blk = 256
        r = lax.broadcasted_iota(jnp.int32, (blk, n_seq), 0)
        c = lax.broadcasted_iota(jnp.int32, (blk, n_seq), 1)
        for r0 in range(0, n_seq, blk):
            tri_ref[r0:r0 + blk, :] = jnp.where(r + r0 < c, 1.0, 0.0).astype(BF16)

    affs = []
    for b in range(n_b):
        lt = jnp.transpose(lo_ref[b * n_seq:(b + 1) * n_seq, :])[:N_EXP, :]
        ex = jnp.exp(lt - jnp.max(lt, axis=0, keepdims=True))
        affs.append(ex / jnp.sum(ex, axis=0, keepdims=True))
    aff = jnp.concatenate(affs, axis=0)
    n_rows = n_b * N_EXP

    def count_ge(t):
        return jnp.sum(jnp.where(aff >= t, 1.0, 0.0), axis=1, keepdims=True)

    def bisect_log(_, lohi):
        lo, hi = lohi
        mid = 0.5 * (lo + hi)
        ok = count_ge(jnp.exp(mid)) >= cap
        return jnp.where(ok, mid, lo), jnp.where(ok, hi, mid)

    def bisect_lin(_, lohi):
        lo, hi = lohi
        mid = 0.5 * (lo + hi)
        ok = count_ge(mid) >= cap
        return jnp.where(ok, mid, lo), jnp.where(ok, hi, mid)

    lo0 = jnp.full((n_rows, 1), -150.0, F32)
    hi0 = jnp.full((n_rows, 1), 1.0, F32)
    lo_l, hi_l = lax.fori_loop(0, 18, bisect_log, (lo0, hi0))
    thr, above = lax.fori_loop(0, 34, bisect_lin, (jnp.exp(lo_l), jnp.exp(hi_l)))
    gt = aff >= above
    eq = (aff >= thr) & (aff < above)
    need = cap - jnp.sum(jnp.where(gt, 1.0, 0.0), axis=1, keepdims=True)
    tri = tri_ref[...]
    eq_before = jnp.dot(jnp.where(eq, 1.0, 0.0).astype(BF16), tri, preferred_element_type=F32)
    sel = gt | (eq & (eq_before < need))
    sel_bf = jnp.where(sel, 1.0, 0.0).astype(BF16)
    slot = jnp.where(sel, jnp.dot(sel_bf, tri, preferred_element_type=F32), -1.0)
    gate = jnp.where(sel, aff, 0.0)
    tt = lax.broadcasted_iota(jnp.int32, (n_seq, LANES), 0)
    ti = lax.broadcasted_iota(jnp.int32, (n_seq, LANES), 1)
    before = jnp.where(tt < ti * MOE_TILE, 1.0, 0.0).astype(BF16)
    off = jnp.dot(sel_bf, before, preferred_element_type=F32).astype(jnp.int32)
    pad = jnp.full((LANES - N_EXP, n_seq), -1.0, F32)
    for b in range(n_b):
        rows = slice(b * N_EXP, (b + 1) * N_EXP)
        slot_row_ref[b] = slot[rows, :]
        off_ref[b] = off[rows, :]
        slot_col_ref[b * n_seq:(b + 1) * n_seq, :] = jnp.transpose(jnp.concatenate([slot[rows, :], pad], axis=0))
        gate_col_ref[b * n_seq:(b + 1) * n_seq, :] = jnp.transpose(jnp.concatenate([gate[rows, :], pad * 0.0], axis=0))


def _route(logits, *, batch, n_seq, cap):
    n_b = 4 if batch % 4 == 0 else 1
    return pl.pallas_call(
        functools.partial(_route_kernel, n_seq=n_seq, cap=cap, n_b=n_b),
        out_shape=(jax.ShapeDtypeStruct((batch, N_EXP, n_seq), F32),
                   jax.ShapeDtypeStruct((batch * n_seq, LANES), F32),
                   jax.ShapeDtypeStruct((batch * n_seq, LANES), F32),
                   jax.ShapeDtypeStruct((batch, N_EXP, LANES), jnp.int32)),
        grid=(batch // n_b,),
        in_specs=[pl.BlockSpec((n_b * n_seq, LANES), lambda b: (b, 0))],
        out_specs=(pl.BlockSpec((n_b, N_EXP, n_seq), lambda b: (b, 0, 0)),
                   pl.BlockSpec((n_b * n_seq, LANES), lambda b: (b, 0)),
                   pl.BlockSpec((n_b * n_seq, LANES), lambda b: (b, 0)),
                   pl.BlockSpec((n_b, N_EXP, LANES), lambda b: (b, 0, 0))),
        scratch_shapes=[pltpu.VMEM((n_seq, n_seq), BF16)],
        compiler_params=_cparams(("arbitrary",)),
        name="ec_route",
    )(logits)


def _windows(off_ref, row0, n_exp, tile, cap):
    firsts, n_win = [], 0
    for e in range(n_exp):
        lo = off_ref[row0 + e * OFF_STRIDE + tile]
        hi = off_ref[row0 + e * OFF_STRIDE + tile + 1]
        first = (lo >> 4) << 4
        firsts.append(first)
        n_win = jnp.maximum(n_win, (hi - first + MOE_WIN - 1) // MOE_WIN)
    return firsts, n_win


def _dispatch_kernel(off_ref, slot_ref, h_ref, o_ref, *, e_blk, cap):
    b, half = pl.program_id(0), pl.program_id(1)
    o_ref[...] = jnp.zeros(o_ref.shape, o_ref.dtype)
    siota = lax.broadcasted_iota(jnp.int32, (MOE_WIN, 1), 0)
    n_cols = h_ref.shape[1]
    chunk = 512

    def token_tile(tile, carry):
        firsts, n_win = _windows(off_ref, (b * N_EXP + half * e_blk) * OFF_STRIDE, e_blk, tile, cap)
        t0 = pl.multiple_of(tile * MOE_TILE, MOE_TILE)

        def window(w, carry):
            bases, parts = [], []
            for e in range(e_blk):
                start = firsts[e] + w * MOE_WIN
                base = pl.multiple_of(jnp.minimum(start, cap - MOE_WIN), 16)
                srow = slot_ref[e, tile]
                srow = jnp.where(srow >= start.astype(F32), srow, -1.0)
                parts.append(jnp.where(srow == (base + siota).astype(F32), 1.0, 0.0).astype(BF16))
                bases.append(base)
            onehot = jnp.concatenate(parts, axis=0)
            for c0 in range(0, n_cols, chunk):
                res = jnp.dot(onehot, h_ref[pl.ds(t0, MOE_TILE), c0:c0 + chunk],
                              preferred_element_type=F32).astype(BF16)
                for e in range(e_blk):
                    rows = pl.ds(bases[e], MOE_WIN)
                    o_ref[e, rows, c0:c0 + chunk] = (o_ref[e, rows, c0:c0 + chunk]
                                                     + res[e * MOE_WIN:(e + 1) * MOE_WIN, :])
            return carry

        return lax.fori_loop(0, n_win, window, carry)

    lax.fori_loop(0, slot_ref.shape[1], token_tile, 0)


def _dispatch(off, slot_row, h2, *, batch, n_seq, cap):
    e_blk = 8
    tiles = n_seq // MOE_TILE
    return pl.pallas_call(
        functools.partial(_dispatch_kernel, e_blk=e_blk, cap=cap),
        out_shape=jax.ShapeDtypeStruct((N_EXP, batch * cap, D), BF16),
        grid_spec=pltpu.PrefetchScalarGridSpec(
            num_scalar_prefetch=1,
            grid=(batch, N_EXP // e_blk),
            in_specs=[
                pl.BlockSpec((None, e_blk, tiles, 1, MOE_TILE), lambda b, j, off: (b, j, 0, 0, 0)),
                pl.BlockSpec((n_seq, D), lambda b, j, off: (b, 0)),
            ],
            out_specs=pl.BlockSpec((e_blk, cap, D), lambda b, j, off: (j, b, 0)),
        ),
        compiler_params=_cparams(("arbitrary", "arbitrary")),
        name="ec_dispatch",
    )(off, slot_row.reshape(batch, N_EXP, tiles, 1, MOE_TILE), h2)


def _gate_up_kernel(x_ref, wg_ref, wu_ref, o_ref):
    x = x_ref[...]
    part = 256
    for c0 in range(0, o_ref.shape[1], part):
        g = jnp.dot(x, wg_ref[:, c0:c0 + part].astype(BF16), preferred_element_type=F32)
        u = jnp.dot(x, wu_ref[:, c0:c0 + part].astype(BF16), preferred_element_type=F32)
        o_ref[:, c0:c0 + part] = (jax.nn.silu(g) * u).astype(BF16)


def _down_kernel(h_ref, wd_ref, o_ref):
    y = jnp.dot(h_ref[...], wd_ref[...].astype(BF16), preferred_element_type=F32).astype(BF16)
    cap = o_ref.shape[1]
    for b in range(o_ref.shape[0]):
        o_ref[b] = y[b * cap:(b + 1) * cap, :]


def _experts(xs, w_gate, w_up, w_down, layer, *, batch, cap):
    m = batch * cap
    tn, tn_down = TN_UP, TN_DOWN
    hid = pl.pallas_call(
        _gate_up_kernel,
        out_shape=jax.ShapeDtypeStruct((N_EXP, m, EXP_FF), BF16),
        grid=(N_EXP, EXP_FF // tn),
        in_specs=[
            pl.BlockSpec((None, m, D), lambda e, j: (e, 0, 0)),
            pl.BlockSpec((None, None, D, tn), lambda e, j: (layer, e, 0, j)),
            pl.BlockSpec((None, None, D, tn), lambda e, j: (layer, e, 0, j)),
        ],
        out_specs=pl.BlockSpec((None, m, tn), lambda e, j: (e, 0, j)),
        compiler_params=_cparams(("arbitrary", "arbitrary")),
        name="expert_gate_up",
    )(xs, w_gate, w_up)
    return pl.pallas_call(
        _down_kernel,
        out_shape=jax.ShapeDtypeStruct((batch, N_EXP * cap, D), BF16),
        grid=(N_EXP, D // tn_down),
        in_specs=[
            pl.BlockSpec((None, m, EXP_FF), lambda e, j: (e, 0, 0)),
            pl.BlockSpec((None, None, EXP_FF, tn_down), lambda e, j: (layer, e, 0, j)),
        ],
        out_specs=pl.BlockSpec((batch, cap, tn_down), lambda e, j: (0, e, j)),
        compiler_params=_cparams(("arbitrary", "arbitrary")),
        name="expert_down",
    )(hid, w_down)


def _combine_kernel(off_ref, ye_ref, slot_ref, gate_ref, x_ref, g2_ref, lg_ref, lb_ref, o_ref, moe_ref, *, cap):
    b, tile = pl.program_id(0), pl.program_id(1)
    firsts, n_win = _windows(off_ref, b * N_EXP * OFF_STRIDE, N_EXP, tile, cap)
    lane_e = lax.broadcasted_iota(jnp.int32, (LANES, N_EXP * MOE_WIN), 1) // MOE_WIN
    spread = jnp.where(lax.broadcasted_iota(jnp.int32, (LANES, N_EXP * MOE_WIN), 0) == lane_e, 1.0, 0.0).astype(BF16)
    slot = jnp.dot(slot_ref[...].astype(BF16), spread, preferred_element_type=F32)
    gate = jnp.dot(gate_ref[...].astype(BF16), spread, preferred_element_type=F32).astype(BF16)
    siota = lax.broadcasted_iota(jnp.int32, (1, MOE_WIN), 1)

    def window(w):
        starts, targets, rows = [], [], []
        for e in range(N_EXP):
            start = firsts[e] + w * MOE_WIN
            base = pl.multiple_of(jnp.minimum(start, cap - MOE_WIN), 16)
            starts.append(jnp.full((1, MOE_WIN), start, jnp.int32))
            targets.append(base + siota)
            rows.append(ye_ref[pl.ds(e * cap + base, MOE_WIN), :])
        start_l = jnp.concatenate(starts, axis=1).astype(F32)
        target_l = jnp.concatenate(targets, axis=1).astype(F32)
        comb = jnp.where((slot == target_l) & (slot >= start_l), gate, jnp.zeros_like(gate))
        return jnp.dot(comb, jnp.concatenate(rows, axis=0), preferred_element_type=F32)

    moe_ref[...] = window(0)

    @pl.when(n_win > 1)
    def _():
        def more(w, carry):
            moe_ref[...] += window(w)
            return carry
        lax.fori_loop(1, n_win, more, 0)

    o_ref[...] = _ln(ALPHA * x_ref[...] + g2_ref[...] * moe_ref[...]) * lg_ref[...] + lb_ref[...]


def _combine(off, ye, slot_col, gate_col, x1, mod5, layer, ln_g, ln_b, *, batch, n_seq, cap):
    tm = MOE_TILE
    tiles = n_seq // tm
    vec_spec = pl.BlockSpec((1, D), lambda b, i, off: (0, 0))
    row_spec = lambda w: pl.BlockSpec((tm, w), lambda b, i, off: (b * tiles + i, 0))
    return pl.pallas_call(
        functools.partial(_combine_kernel, cap=cap),
        out_shape=jax.ShapeDtypeStruct((batch * n_seq, D), F32),
        grid_spec=pltpu.PrefetchScalarGridSpec(
            num_scalar_prefetch=1,
            grid=(batch, tiles),
            in_specs=[
                pl.BlockSpec((None, N_EXP * cap, D), lambda b, i, off: (b, 0, 0)),
                row_spec(LANES), row_spec(LANES), row_spec(D),
                pl.BlockSpec((None, None, None, 1, D), lambda b, i, off: (layer, b, 5, 0, 0)),
                vec_spec, vec_spec,
            ],
            out_specs=row_spec(D),
            scratch_shapes=[pltpu.VMEM((tm, D), F32)],
        ),
        compiler_params=_cparams(("arbitrary", "arbitrary")),
        name="ec_combine_ln",
    )(off, ye, slot_col, gate_col, x1, mod5, ln_g.reshape(1, D), ln_b.reshape(1, D))


def _moe_block(x1, h2, logits, mod5, layer, w_gate, w_up, w_down, ln_g, ln_b, *, batch, n_seq):
    cap = 2 * n_seq // N_EXP
    assert n_seq % MOE_TILE == 0 and n_seq // MOE_TILE < OFF_STRIDE and cap % MOE_WIN == 0
    assert cap <= 256
    slot_row, slot_col, gate_col, off = _route(logits, batch=batch, n_seq=n_seq, cap=cap)
    off = off[:, :, :OFF_STRIDE].reshape(batch * N_EXP * OFF_STRIDE)
    xs = _dispatch(off, slot_row, h2, batch=batch, n_seq=n_seq, cap=cap)
    ye = _experts(xs, w_gate, w_up, w_down, layer, batch=batch, cap=cap)
    return _combine(off, ye, slot_col, gate_col, x1, mod5, layer, ln_g, ln_b, batch=batch, n_seq=n_seq, cap=cap)


def _cd_in_kernel(x_ref, sh_ref, sc_ref, w_ref, lg_ref, lb_ref, u_ref, vg_ref, xg_ref):
    h = (_ln(x_ref[...]) * (1.0 + sc_ref[...]) + sh_ref[...]).astype(BF16)
    chunk = 512
    for c0 in range(0, SG_W, chunk):
        u_ref[:, c0:c0 + chunk] = jax.nn.gelu(
            jnp.dot(h, w_ref[:, c0:c0 + chunk], preferred_element_type=F32)).astype(BF16)
    for c0 in range(0, SG_W, chunk):
        z = jax.nn.gelu(jnp.dot(h, w_ref[:, SG_W + c0:SG_W + c0 + chunk], preferred_element_type=F32))
        parts = [_ln(z[:, j * HEAD:(j + 1) * HEAD]) for j in range(chunk // HEAD)]
        vg = jnp.concatenate(parts, axis=1) * lg_ref[:, c0:c0 + chunk] + lb_ref[:, c0:c0 + chunk]
        vg_ref[:, c0:c0 + chunk] = vg.astype(BF16)
    for c0 in range(0, CONV_CH, chunk):
        a = jnp.dot(h, w_ref[:, 2 * SG_W + c0:2 * SG_W + c0 + chunk], preferred_element_type=F32)
        gt = jnp.dot(h, w_ref[:, 2 * SG_W + CONV_CH + c0:2 * SG_W + CONV_CH + c0 + chunk],
                     preferred_element_type=F32)
        xg_ref[:, c0:c0 + chunk] = a * jax.nn.sigmoid(gt)


def _cd_in_proj(x2d, mod5, layer, w_bf, sg_ln_g, sg_ln_b, *, n_seq):
    rows = x2d.shape[0]
    tm = 256
    tiles_per_seq = n_seq // tm
    mod_spec = lambda chunk: pl.BlockSpec((None, None, None, 1, D),
                                          lambda i: (layer, i // tiles_per_seq, chunk, 0, 0))
    vec_spec = pl.BlockSpec((1, SG_W), lambda i: (0, 0))
    row_spec = lambda w: pl.BlockSpec((tm, w), lambda i: (i, 0))
    return pl.pallas_call(
        _cd_in_kernel,
        out_shape=(jax.ShapeDtypeStruct((rows, SG_W), BF16),
                   jax.ShapeDtypeStruct((rows, SG_W), BF16),
                   jax.ShapeDtypeStruct((rows, CONV_CH), F32)),
        grid=(rows // tm,),
        in_specs=[row_spec(D), mod_spec(0), mod_spec(1),
                  pl.BlockSpec((D, CD_IN), lambda i: (0, 0)), vec_spec, vec_spec],
        out_specs=(row_spec(SG_W), row_spec(SG_W), row_spec(CONV_CH)),
        compiler_params=_cparams(("arbitrary",)),
        name="cd_in_proj",
    )(x2d, mod5, mod5, w_bf, sg_ln_g.reshape(1, SG_W), sg_ln_b.reshape(1, SG_W))


CONV_HALO = 16
CONV_ROWS = 32


def _cd_mix_kernel(u_ref, vg_ref, xp_ref, xc_ref, xn_ref, sgw_ref, sgb_ref, cw_ref, cb_ref, lg_ref, lb_ref,
                   o_ref, sh_ref, cv_ref, *, tiles_per_seq):
    i = pl.program_id(1)
    tm = xc_ref.shape[0]
    for c in range(tm // CHUNK):
        for g in range(SG_G):
            rs, cs = slice(c * CHUNK, (c + 1) * CHUNK), slice(g * HEAD, (g + 1) * HEAD)
            sp = jnp.dot(sgw_ref[g], vg_ref[rs, cs], preferred_element_type=F32) + sgb_ref[g]
            o_ref[rs, cs] = (u_ref[rs, cs].astype(F32) * sp).astype(BF16)
    sh_ref[0, 0:CONV_HALO, :] = jnp.where(i > 0, xp_ref[...], 0.0)
    sh_ref[0, CONV_HALO:CONV_HALO + tm, :] = xc_ref[...]
    sh_ref[0, CONV_HALO + tm:, :] = jnp.where(i < tiles_per_seq - 1, xn_ref[...], 0.0)
    keep = tm + 2 * CONV_HALO - SUBLANES
    for s in range(1, SUBLANES):
        sh_ref[s, 0:keep, :] = sh_ref[0, s:s + keep, :]
    off = CONV_HALO - CONV_K // 2
    for r0 in range(0, tm, CONV_ROWS):
        for c0 in range(0, CONV_CH, LANES):
            acc = jnp.zeros((CONV_ROWS, LANES), F32) + cb_ref[:, c0:c0 + LANES]
            for k in range(CONV_K):
                a = r0 + off + k
                acc = acc + (sh_ref[a % SUBLANES, a - a % SUBLANES:a - a % SUBLANES + CONV_ROWS, c0:c0 + LANES]
                             * cw_ref[k:k + 1, c0:c0 + LANES])
            cv_ref[r0:r0 + CONV_ROWS, c0:c0 + LANES] = acc
    y = _ln(cv_ref[...]) * lg_ref[...] + lb_ref[...]
    o_ref[:, SG_W:] = jax.nn.silu(y).astype(BF16)


def _cd_mix(u, vg, xg, sg_w, sg_b, conv_w, conv_b, conv_ln_g, conv_ln_b, *, batch, n_seq):
    tm = 256
    tiles = n_seq // tm
    hpt = tm // CONV_HALO
    n_halo = batch * n_seq // CONV_HALO
    row_spec = lambda w: pl.BlockSpec((tm, w), lambda b, i: (b * tiles + i, 0))
    vec_spec = pl.BlockSpec((1, CONV_CH), lambda b, i: (0, 0))
    sgb_full = jnp.broadcast_to(sg_b[:, :, None], (SG_G, CHUNK, HEAD))
    return pl.pallas_call(
        functools.partial(_cd_mix_kernel, tiles_per_seq=tiles),
        out_shape=jax.ShapeDtypeStruct((batch * n_seq, D), BF16),
        grid=(batch, tiles),
        in_specs=[
            row_spec(SG_W), row_spec(SG_W),
            pl.BlockSpec((CONV_HALO, CONV_CH), lambda b, i: (jnp.maximum((b * tiles + i) * hpt - 1, 0), 0)),
            row_spec(CONV_CH),
            pl.BlockSpec((CONV_HALO, CONV_CH), lambda b, i: (jnp.minimum((b * tiles + i + 1) * hpt, n_halo - 1), 0)),
            pl.BlockSpec((SG_G, CHUNK, CHUNK), lambda b, i: (0, 0, 0)),
            pl.BlockSpec((SG_G, CHUNK, HEAD), lambda b, i: (0, 0, 0)),
            pl.BlockSpec((CONV_K + 1, CONV_CH), lambda b, i: (0, 0)),
            vec_spec, vec_spec, vec_spec,
        ],
        out_specs=pl.BlockSpec((tm, D), lambda b, i: (b * tiles + i, 0)),
        scratch_shapes=[pltpu.VMEM((SUBLANES, tm + 2 * CONV_HALO, CONV_CH), F32), pltpu.VMEM((tm, CONV_CH), F32)],
        compiler_params=_cparams(("arbitrary", "arbitrary")),
        name="cd_mix",
    )(u, vg, xg, xg, xg, sg_w.astype(BF16), sgb_full,
      jnp.pad(conv_w.reshape(CONV_K, CONV_CH), ((0, 1), (0, 0))),
      conv_b.reshape(1, CONV_CH), conv_ln_g.reshape(1, CONV_CH), conv_ln_b.reshape(1, CONV_CH))


def kernel(x, c, ctx, c_ctx, w_mod, b_mod, ln1_g, ln1_b, ln2_g, ln2_b, w_router, w_gate, w_up, w_down,
           ab_w_in, ab_w_out, sink, cd_w_in, cd_w_out, sg_ln_g, sg_ln_b, sg_w, sg_b,
           conv_w, conv_b, conv_ln_g, conv_ln_b):
    batch, n_seq, _ = x.shape
    n_ctx = ctx.shape[1]
    assert x.shape[2] == D and batch + 1 <= MOD_ROWS

    c_pad = jnp.concatenate([c, c_ctx[None, :], jnp.zeros((MOD_ROWS - batch - 1, D), F32)], axis=0)
    mod = _modulation(c_pad, w_mod, b_mod)
    mod5 = mod.reshape(DEPTH, MOD_ROWS, 6, 1, D)
    w_router_pad = jnp.pad(w_router, ((0, 0), (0, 0), (0, LANES - N_EXP))).astype(BF16)

    x0 = x.reshape(batch * n_seq, D)
    ctx2d = ctx.reshape(batch * n_ctx, D)

    w_in = ab_w_in[0]
    tabs = _rope_tables(n_seq)
    p_lat = _ab_in_proj(x0, mod5, 0, lambda b: b, w_in, tabs, n_seq=n_seq,
                        rope_cols=K_END, q_cols=Q_END, name="ab_in_proj")
    p_ctx = _ab_in_proj(ctx2d, mod5, 0, lambda b: batch, w_in[:, Q_END:V_END], tabs, n_seq=n_ctx,
                        rope_cols=0, q_cols=0, name="ab_ctx_proj")
    attn = _attention(p_lat, p_ctx, sink[0], batch=batch, n_seq=n_seq, n_ctx=n_ctx)
    four = _fourier(p_lat, batch=batch, n_seq=n_seq)
    x1, h2, logits = _out_proj([attn, four], ab_w_out[0], x0, mod5, 0, ln1_g[0], ln1_b[0],
                               w_router_pad[0], n_seq=n_seq, folded_last=True)
    x2 = _moe_block(x1, h2, logits, mod5, 0, w_gate, w_up, w_down, ln2_g[0], ln2_b[0], batch=batch, n_seq=n_seq)

    u, vg, xg = _cd_in_proj(x2, mod5, 1, cd_w_in[0].astype(BF16), sg_ln_g[0], sg_ln_b[0], n_seq=n_seq)
    mix = _cd_mix(u, vg, xg, sg_w[0], sg_b[0], conv_w[0], conv_b[0], conv_ln_g[0], conv_ln_b[0],
                  batch=batch, n_seq=n_seq)
    x3, h4, logits1 = _out_proj([mix], cd_w_out[0], x2, mod5, 1, ln1_g[1], ln1_b[1],
                                w_router_pad[1], n_seq=n_seq)
    x4 = _moe_block(x3, h4, logits1, mod5, 1, w_gate, w_up, w_down, ln2_g[1], ln2_b[1], batch=batch, n_seq=n_seq)
    return x4.reshape(batch, n_seq, D)
```

```python
import functools

import jax
import jax.numpy as jnp
from jax import lax
from jax.experimental import pallas as pl
from jax.experimental.pallas import tpu as pltpu

F32 = jnp.float32
BF16 = jnp.bfloat16

D = 2048
HEAD = 128
N_Q = 12
N_KV = 4
GQA = 3
WINDOW = 128
GRID_W = 64
ROPE_BASE = 10000.0
Q_END = N_Q * HEAD
KV_W = N_KV * HEAD
K_END = Q_END + KV_W
V_END = K_END + KV_W
FOUR_G = 4
FOUR_W = FOUR_G * HEAD
AB_IN = V_END + FOUR_W
SG_G = 8
SG_W = SG_G * HEAD
CHUNK = 128
CONV_CH = 1024
CONV_K = 31
CD_IN = 4096
N_EXP = 16
EXP_FF = 2048
DEPTH = 2
ALPHA = (2 * DEPTH) ** 0.25
LN_EPS = 1e-6
ATTN_SCALE = HEAD ** -0.5
MOD_ROWS = 16
LANES = 128
SUBLANES = 8
VMEM_LIMIT = 56 * 1024 * 1024
MOE_TILE = 256
MOE_WIN = 64
OFF_STRIDE = 16
TN_UP = 512
TN_DOWN = 512


def _cparams(sem):
    return pltpu.CompilerParams(dimension_semantics=sem, vmem_limit_bytes=VMEM_LIMIT)


def _ln(x):
    mu = jnp.mean(x, axis=-1, keepdims=True)
    xc = x - mu
    var = jnp.mean(xc * xc, axis=-1, keepdims=True)
    return xc * lax.rsqrt(var + LN_EPS)


def _mod_kernel(c_ref, w_ref, b_ref, o_ref):
    s = jax.nn.silu(c_ref[...]).astype(BF16)
    o_ref[...] = jnp.dot(s, w_ref[...].astype(BF16), preferred_element_type=F32) + b_ref[...]


def _modulation(c_pad, w_mod, b_mod, layer):
    tn = 1024
    return pl.pallas_call(
        _mod_kernel,
        out_shape=jax.ShapeDtypeStruct((MOD_ROWS, 6 * D), F32),
        grid=(6 * D // tn,),
        in_specs=[
            pl.BlockSpec((MOD_ROWS, D), lambda j: (0, 0)),
            pl.BlockSpec((None, D, tn), lambda j: (layer, 0, j)),
            pl.BlockSpec((None, 1, tn), lambda j: (layer, 0, j)),
        ],
        out_specs=pl.BlockSpec((MOD_ROWS, tn), lambda j: (0, j)),
        compiler_params=_cparams(("arbitrary",)),
        name="modulation",
    )(c_pad, w_mod, b_mod.reshape(DEPTH, 1, 6 * D))


def _rope(a, cos, sa, sb):
    return a * cos + pltpu.roll(a, HEAD - 32, 1) * sa + pltpu.roll(a, 32, 1) * sb


def _ab_in_kernel(x_ref, sh_ref, sc_ref, wf_ref, cos_ref, sa_ref, sb_ref, o_ref, w_ref, *, n_cols, rope_cols, q_cols):
    @pl.when(pl.program_id(0) == 0)
    def _():
        for r0 in range(0, D, 256):
            w_ref[r0:r0 + 256, :] = wf_ref[r0:r0 + 256, :].astype(BF16)

    h = (_ln(x_ref[...]) * (1.0 + sc_ref[...]) + sh_ref[...]).astype(BF16)
    chunk = 512
    for c0 in range(0, n_cols, chunk):
        acc = jnp.dot(h, w_ref[:, c0:c0 + chunk], preferred_element_type=F32)
        if c0 < rope_cols:
            cos, sa, sb = cos_ref[...], sa_ref[...], sb_ref[...]
            parts = []
            for j in range(chunk // HEAD):
                r = _rope(acc[:, j * HEAD:(j + 1) * HEAD], cos, sa, sb)
                parts.append(r * ATTN_SCALE if c0 < q_cols else r)
            acc = jnp.concatenate(parts, axis=1)
        o_ref[:, c0:c0 + chunk] = acc.astype(BF16)


def _ab_in_proj(x2d, mod5, layer, row_of_tile, w, rope_tabs, *, n_seq, rope_cols, q_cols, name):
    rows, _ = x2d.shape
    n_cols = w.shape[1]
    tm = 256
    tiles_per_seq = n_seq // tm
    cos, sa, sb = rope_tabs
    kern = functools.partial(_ab_in_kernel, n_cols=n_cols, rope_cols=rope_cols, q_cols=q_cols)
    mod_spec = lambda chunk: pl.BlockSpec((None, None, None, 1, D),
                                          lambda i: (layer, row_of_tile(i // tiles_per_seq), chunk, 0, 0))
    tab_spec = pl.BlockSpec((tm, HEAD), lambda i: (i % tiles_per_seq, 0))
    return pl.pallas_call(
        kern,
        out_shape=jax.ShapeDtypeStruct((rows, n_cols), BF16),
        grid=(rows // tm,),
        in_specs=[
            pl.BlockSpec((tm, D), lambda i: (i, 0)),
            mod_spec(0), mod_spec(1),
            pl.BlockSpec((D, n_cols), lambda i: (0, 0), pipeline_mode=pl.Buffered(1)),
            tab_spec, tab_spec, tab_spec,
        ],
        out_specs=pl.BlockSpec((tm, n_cols), lambda i: (i, 0)),
        scratch_shapes=[pltpu.VMEM((D, n_cols), BF16)],
        compiler_params=_cparams(("arbitrary",)),
        name=name,
    )(x2d, mod5, mod5, w, cos, sa, sb)


def _rope_tables(n):
    quarter = HEAD // 4
    t = jnp.arange(n)
    r = (t // GRID_W).astype(F32)
    col = (t % GRID_W).astype(F32)
    inv = ROPE_BASE ** (-jnp.arange(quarter, dtype=F32) / quarter)
    ang_r, ang_c = r[:, None] * inv, col[:, None] * inv
    zero = jnp.zeros_like(ang_r)
    cos = jnp.concatenate([jnp.cos(ang_r)] * 2 + [jnp.cos(ang_c)] * 2, axis=1)
    sa = jnp.concatenate([-jnp.sin(ang_r), zero, -jnp.sin(ang_c), zero], axis=1)
    sb = jnp.concatenate([zero, jnp.sin(ang_r), zero, jnp.sin(ang_c)], axis=1)
    return cos, sa, sb


def _attn_kernel(sink_ref, q_ref, k_ref, v_ref, kc_ref, vc_ref, o_ref, bias_ref, *, n_seq):
    hk = pl.program_id(1)
    kc = kc_ref[...]
    vc = vc_ref[...]
    kw_len = 3 * WINDOW
    rows = GQA * WINDOW
    n_blocks = n_seq // WINDOW
    row = lax.broadcasted_iota(jnp.int32, (rows, 1), 0)
    sink = jnp.where(row < WINDOW, sink_ref[hk * GQA],
                     jnp.where(row < 2 * WINDOW, sink_ref[hk * GQA + 1], sink_ref[hk * GQA + 2]))
    rel = (row & (WINDOW - 1)) - lax.broadcasted_iota(jnp.int32, (1, kw_len), 1)
    for i in range(3):
        bias_ref[i] = jnp.where(jnp.abs(rel + i * WINDOW) <= WINDOW, 0.0, -1e30)
    dn = (((1,), (1,)), ((), ()))

    def block(n, back):
        r0 = n * WINDOW if isinstance(n, int) else pl.multiple_of(n * WINDOW, WINDOW)
        start = r0 - back * WINDOW
        if not isinstance(start, int):
            start = pl.multiple_of(start, WINDOW)
        qs = q_ref[pl.ds(r0, WINDOW), :]
        q3 = jnp.concatenate([qs[:, g * HEAD:(g + 1) * HEAD] for g in range(GQA)], axis=0)
        kw = k_ref[pl.ds(start, kw_len), :]
        vw = v_ref[pl.ds(start, kw_len), :]
        s_w = lax.dot_general(q3, kw, dn, preferred_element_type=F32) + bias_ref[back]
        s_c = lax.dot_general(q3, kc, dn, preferred_element_type=F32)
        m = jnp.maximum(jnp.maximum(jnp.max(s_w, axis=1, keepdims=True), jnp.max(s_c, axis=1, keepdims=True)), sink)
        p_w = jnp.exp(s_w - m)
        p_c = jnp.exp(s_c - m)
        den = jnp.sum(p_w, axis=1, keepdims=True) + jnp.sum(p_c, axis=1, keepdims=True) + jnp.exp(sink - m)
        o = (jnp.dot(p_w.astype(BF16), vw, preferred_element_type=F32)
             + jnp.dot(p_c.astype(BF16), vc, preferred_element_type=F32)) / den
        o_ref[pl.ds(r0, WINDOW), :] = jnp.concatenate(
            [o[g * WINDOW:(g + 1) * WINDOW, :] for g in range(GQA)], axis=1).astype(BF16)

    block(0, 0)
    block(n_blocks - 1, 2)

    per_iter = max(g for g in (7, 2, 1) if (n_blocks - 2) % g == 0)

    def interior(i, carry):
        for g in range(per_iter):
            block(1 + per_iter * i + g, 1)
        return carry

    lax.fori_loop(0, (n_blocks - 2) // per_iter, interior, 0)


def _attention(p_lat, p_ctx, sink, *, batch, n_seq, n_ctx):
    qw = GQA * HEAD
    assert n_seq % WINDOW == 0 and n_seq >= 3 * WINDOW
    return pl.pallas_call(
        functools.partial(_attn_kernel, n_seq=n_seq),
        out_shape=jax.ShapeDtypeStruct((batch * n_seq, Q_END), BF16),
        grid=(batch, N_KV),
        in_specs=[
            pl.BlockSpec(memory_space=pltpu.SMEM),
            pl.BlockSpec((n_seq, qw), lambda b, h: (b, h)),
            pl.BlockSpec((n_seq, HEAD), lambda b, h: (b, Q_END // HEAD + h)),
            pl.BlockSpec((n_seq, HEAD), lambda b, h: (b, K_END // HEAD + h)),
            pl.BlockSpec((n_ctx, HEAD), lambda b, h: (b, h)),
            pl.BlockSpec((n_ctx, HEAD), lambda b, h: (b, N_KV + h)),
        ],
        out_specs=pl.BlockSpec((n_seq, qw), lambda b, h: (b, h)),
        scratch_shapes=[pltpu.VMEM((3, GQA * WINDOW, 3 * WINDOW), F32)],
        compiler_params=_cparams(("arbitrary", "arbitrary")),
        name="banded_gqa",
    )(sink, p_lat, p_lat, p_lat, p_ctx, p_ctx)


DFT_ROWS = 32


def _fourier_kernel(z_ref, cs_ref, ca_ref, sa_ref, cb_ref, sb_ref, o_ref, cn_ref, sn_ref):
    n = z_ref.shape[0]
    half = n // 2

    @pl.when(pl.program_id(0) == 0)
    def _():
        cb, sb = cb_ref[...], sb_ref[...]

        def rows(a, carry):
            ca, sa = ca_ref[a], sa_ref[a]
            r = pl.ds(pl.multiple_of(a * DFT_ROWS, DFT_ROWS), DFT_ROWS)
            cn_ref[r, :] = (ca * cb - sa * sb).astype(BF16)
            sn_ref[r, :] = (sa * cb + ca * sb).astype(BF16)
            return carry

        lax.fori_loop(0, ca_ref.shape[0], rows, 0)

    cs = cs_ref[...]
    a_parts, b_parts = [], []
    for g in range(FOUR_G):
        ab = jnp.dot(z_ref[:, g * HEAD:(g + 1) * HEAD], cs, preferred_element_type=F32)
        a_parts.append(ab[:, :HEAD])
        b_parts.append(ab[:, HEAD:])
    a = jnp.concatenate(a_parts, axis=1).astype(BF16)
    b = jnp.concatenate(b_parts, axis=1).astype(BF16)
    p = jnp.dot(cn_ref[...], a, preferred_element_type=F32)
    q = jnp.dot(sn_ref[...], b, preferred_element_type=F32)
    scale = 1.0 / (n * HEAD) ** 0.5
    o_ref[0:half, :] = ((p[0:half, :] - q[0:half, :]) * scale).astype(BF16)
    o_ref[half:n, :] = ((p[1:half + 1, :] + q[1:half + 1, :]) * scale).astype(BF16)


def _dft_rows(j, m):
    ang = ((j[:, None] * jnp.arange(m, dtype=jnp.int32)[None, :]) % m).astype(F32) * (2.0 * jnp.pi / m)
    return jnp.cos(ang), jnp.sin(ang)


def _fourier(p_lat, *, batch, n_seq):
    assert n_seq % (2 * DFT_ROWS) == 0
    n_a = n_seq // (2 * DFT_ROWS) + 1
    cc, sc = _dft_rows(jnp.arange(HEAD, dtype=jnp.int32), HEAD)
    cs = jnp.concatenate([cc, sc], axis=1).astype(BF16)
    ca, sa = _dft_rows(jnp.arange(n_a, dtype=jnp.int32) * DFT_ROWS, n_seq)
    cb, sb = _dft_rows(jnp.arange(DFT_ROWS, dtype=jnp.int32), n_seq)
    coarse_spec = pl.BlockSpec((n_a, 1, n_seq), lambda b: (0, 0, 0))
    fine_spec = pl.BlockSpec((DFT_ROWS, n_seq), lambda b: (0, 0))
    return pl.pallas_call(
        _fourier_kernel,
        out_shape=jax.ShapeDtypeStruct((batch * n_seq, FOUR_W), BF16),
        grid=(batch,),
        in_specs=[
            pl.BlockSpec((n_seq, FOUR_W), lambda b: (b, V_END // FOUR_W)),
            pl.BlockSpec((HEAD, 2 * HEAD), lambda b: (0, 0)),
            coarse_spec, coarse_spec, fine_spec, fine_spec,
        ],
        out_specs=pl.BlockSpec((n_seq, FOUR_W), lambda b: (b, 0)),
        scratch_shapes=[pltpu.VMEM((n_a * DFT_ROWS, n_seq), BF16), pltpu.VMEM((n_a * DFT_ROWS, n_seq), BF16)],
        compiler_params=_cparams(("arbitrary",)),
        name="fourier_mix",
    )(p_lat, cs, ca.reshape(n_a, 1, n_seq), sa.reshape(n_a, 1, n_seq), cb, sb)


def _out_proj_kernel(*refs, n_parts, folded):
    a_refs = refs[:n_parts]
    rest = refs[n_parts:]
    if folded:
        flip_ref, rest = rest[0], rest[1:]
    (wf_ref, x_ref, g1_ref, lg_ref, lb_ref, sh_ref, sc_ref, wr_ref, x1_ref, h2_ref, lo_ref, w_ref) = rest

    @pl.when(pl.program_id(0) == 0)
    def _():
        for r0 in range(0, D, 256):
            w_ref[r0:r0 + 256, :] = wf_ref[r0:r0 + 256, :].astype(BF16)

    y = None
    k0 = 0
    for p, a_ref in enumerate(a_refs):
        kk = a_ref.shape[1]
        a = a_ref[...]
        if folded and p == n_parts - 1:
            a = jnp.dot(flip_ref[...], a, preferred_element_type=F32).astype(BF16)
        t = jnp.dot(a, w_ref[k0:k0 + kk, :], preferred_element_type=F32)
        y = t if y is None else y + t
        k0 += kk
    x1 = _ln(ALPHA * x_ref[...] + g1_ref[...] * y) * lg_ref[...] + lb_ref[...]
    x1_ref[...] = x1
    h2 = (_ln(x1) * (1.0 + sc_ref[...]) + sh_ref[...]).astype(BF16)
    h2_ref[...] = h2
    lo_ref[...] = jnp.dot(h2, wr_ref[...], preferred_element_type=F32)


def _out_proj(parts, w, x2d, mod5, layer, ln_g, ln_b, w_router_pad, *, n_seq, folded_last=False):
    rows = x2d.shape[0]
    tm = 256
    tiles_per_seq = n_seq // tm
    half_tiles = tiles_per_seq // 2
    mod_spec = lambda chunk: pl.BlockSpec((None, None, None, 1, D),
                                          lambda i: (layer, i // tiles_per_seq, chunk, 0, 0))
    vec_spec = pl.BlockSpec((1, D), lambda i: (0, 0))
    in_specs = [pl.BlockSpec((tm, p.shape[1]), lambda i: (i, 0)) for p in parts]
    extra = []
    if folded_last:
        assert tiles_per_seq % 2 == 0

        def folded_tile(i):
            li = i % tiles_per_seq
            return i - li + jnp.where(li < half_tiles, li, half_tiles + tiles_per_seq - 1 - li)

        in_specs[-1] = pl.BlockSpec((tm, parts[-1].shape[1]), lambda i: (folded_tile(i), 0))
        in_specs.append(pl.BlockSpec((None, tm, tm), lambda i: ((i % tiles_per_seq) // half_tiles, 0, 0)))
        eye = jnp.eye(tm, dtype=BF16)
        extra = [jnp.stack([eye, eye[::-1]])]
    in_specs += [
        pl.BlockSpec((D, D), lambda i: (0, 0), pipeline_mode=pl.Buffered(1)),
        pl.BlockSpec((tm, D), lambda i: (i, 0)),
        mod_spec(2), vec_spec, vec_spec, mod_spec(3), mod_spec(4),
        pl.BlockSpec((D, LANES), lambda i: (0, 0)),
    ]
    return pl.pallas_call(
        functools.partial(_out_proj_kernel, n_parts=len(parts), folded=folded_last),
        out_shape=(jax.ShapeDtypeStruct((rows, D), F32),
                   jax.ShapeDtypeStruct((rows, D), BF16),
                   jax.ShapeDtypeStruct((rows, LANES), F32)),
        grid=(rows // tm,),
        in_specs=in_specs,
        out_specs=(pl.BlockSpec((tm, D), lambda i: (i, 0)),
                   pl.BlockSpec((tm, D), lambda i: (i, 0)),
                   pl.BlockSpec((tm, LANES), lambda i: (i, 0))),
        scratch_shapes=[pltpu.VMEM((D, D), BF16)],
        compiler_params=_cparams(("arbitrary",)),
        name="out_proj_ln",
    )(*parts, *extra, w, x2d, mod5, ln_g.reshape(1, D), ln_b.reshape(1, D), mod5, mod5, w_router_pad)


def _route_kernel(lo_ref, slot_row_ref, slot_col_ref, gate_col_ref, off_ref, tri_ref, *, n_seq, cap, n_b):
    @pl.when(pl.program_id(0) == 0)
    def _():
        blk = 256
        r = lax.broadcasted_iota(jnp.int32, (blk, n_seq), 0)
        c = lax.broadcasted_iota(jnp.int32, (blk, n_seq), 1)
        for r0 in range(0, n_seq, blk):
            tri_ref[r0:r0 + blk, :] = jnp.where(r + r0 < c, 1.0, 0.0).astype(BF16)

    affs = []
    for b in range(n_b):
        lt = jnp.transpose(lo_ref[b * n_seq:(b + 1) * n_seq, :])[:N_EXP, :]
        ex = jnp.exp(lt - jnp.max(lt, axis=0, keepdims=True))
        affs.append(ex / jnp.sum(ex, axis=0, keepdims=True))
    aff = jnp.concatenate(affs, axis=0)
    n_rows = n_b * N_EXP

    def count_ge(t):
        return jnp.sum(jnp.where(aff >= t, 1.0, 0.0), axis=1, keepdims=True)

    def bisect_log(_, lohi):
        lo, hi = lohi
        mid = 0.5 * (lo + hi)
        ok = count_ge(jnp.exp(mid)) >= cap
        return jnp.where(ok, mid, lo), jnp.where(ok, hi, mid)

    def bisect_lin(_, lohi):
        lo, hi = lohi
        mid = 0.5 * (lo + hi)
        ok = count_ge(mid) >= cap
        return jnp.where(ok, mid, lo), jnp.where(ok, hi, mid)

    lo0 = jnp.full((n_rows, 1), -150.0, F32)
    hi0 = jnp.full((n_rows, 1), 1.0, F32)
    lo_l, hi_l = lax.fori_loop(0, 18, bisect_log, (lo0, hi0))
    thr, above = lax.fori_loop(0, 34, bisect_lin, (jnp.exp(lo_l), jnp.exp(hi_l)))
    gt = aff >= above
    eq = (aff >= thr) & (aff < above)
    need = cap - jnp.sum(jnp.where(gt, 1.0, 0.0), axis=1, keepdims=True)
    tri = tri_ref[...]
    eq_before = jnp.dot(jnp.where(eq, 1.0, 0.0).astype(BF16), tri, preferred_element_type=F32)
    sel = gt | (eq & (eq_before < need))
    sel_bf = jnp.where(sel, 1.0, 0.0).astype(BF16)
    slot = jnp.where(sel, jnp.dot(sel_bf, tri, preferred_element_type=F32), -1.0)
    gate = jnp.where(sel, aff, 0.0)
    tt = lax.broadcasted_iota(jnp.int32, (n_seq, LANES), 0)
    ti = lax.broadcasted_iota(jnp.int32, (n_seq, LANES), 1)
    before = jnp.where(tt < ti * MOE_TILE, 1.0, 0.0).astype(BF16)
    off = jnp.dot(sel_bf, before, preferred_element_type=F32).astype(jnp.int32)
    pad = jnp.full((LANES - N_EXP, n_seq), -1.0, F32)
    for b in range(n_b):
        rows = slice(b * N_EXP, (b + 1) * N_EXP)
        slot_row_ref[b] = slot[rows, :]
        off_ref[b] = off[rows, :]
        slot_col_ref[b * n_seq:(b + 1) * n_seq, :] = jnp.transpose(jnp.concatenate([slot[rows, :], pad], axis=0))
        gate_col_ref[b * n_seq:(b + 1) * n_seq, :] = jnp.transpose(jnp.concatenate([gate[rows, :], pad * 0.0], axis=0))


def _route(logits, *, batch, n_seq, cap):
    n_b = 4 if batch % 4 == 0 else 1
    return pl.pallas_call(
        functools.partial(_route_kernel, n_seq=n_seq, cap=cap, n_b=n_b),
        out_shape=(jax.ShapeDtypeStruct((batch, N_EXP, n_seq), F32),
                   jax.ShapeDtypeStruct((batch * n_seq, LANES), F32),
                   jax.ShapeDtypeStruct((batch * n_seq, LANES), F32),
                   jax.ShapeDtypeStruct((batch, N_EXP, LANES), jnp.int32)),
        grid=(batch // n_b,),
        in_specs=[pl.BlockSpec((n_b * n_seq, LANES), lambda b: (b, 0))],
        out_specs=(pl.BlockSpec((n_b, N_EXP, n_seq), lambda b: (b, 0, 0)),
                   pl.BlockSpec((n_b * n_seq, LANES), lambda b: (b, 0)),
                   pl.BlockSpec((n_b * n_seq, LANES), lambda b: (b, 0)),
                   pl.BlockSpec((n_b, N_EXP, LANES), lambda b: (b, 0, 0))),
        scratch_shapes=[pltpu.VMEM((n_seq, n_seq), BF16)],
        compiler_params=_cparams(("arbitrary",)),
        name="ec_route",
    )(logits)


def _windows(off_ref, row0, n_exp, tile, cap):
    firsts, n_win = [], 0
    for e in range(n_exp):
        lo = off_ref[row0 + e * OFF_STRIDE + tile]
        hi = off_ref[row0 + e * OFF_STRIDE + tile + 1]
        first = (lo >> 4) << 4
        firsts.append(first)
        n_win = jnp.maximum(n_win, (hi - first + MOE_WIN - 1) // MOE_WIN)
    return firsts, n_win


def _dispatch_kernel(off_ref, slot_ref, h_ref, o_ref, *, e_blk, cap):
    b, half = pl.program_id(0), pl.program_id(1)
    o_ref[...] = jnp.zeros(o_ref.shape, o_ref.dtype)
    siota = lax.broadcasted_iota(jnp.int32, (MOE_WIN, 1), 0)
    n_cols = h_ref.shape[1]
    chunk = 512

    def token_tile(tile, carry):
        firsts, n_win = _windows(off_ref, (b * N_EXP + half * e_blk) * OFF_STRIDE, e_blk, tile, cap)
        t0 = pl.multiple_of(tile * MOE_TILE, MOE_TILE)

        def window(w, carry):
            bases, parts = [], []
            for e in range(e_blk):
                start = firsts[e] + w * MOE_WIN
                base = pl.multiple_of(jnp.minimum(start, cap - MOE_WIN), 16)
                srow = slot_ref[e, tile]
                srow = jnp.where(srow >= start.astype(F32), srow, -1.0)
                parts.append(jnp.where(srow == (base + siota).astype(F32), 1.0, 0.0).astype(BF16))
                bases.append(base)
            onehot = jnp.concatenate(parts, axis=0)
            for c0 in range(0, n_cols, chunk):
                res = jnp.dot(onehot, h_ref[pl.ds(t0, MOE_TILE), c0:c0 + chunk],
                              preferred_element_type=F32).astype(BF16)
                for e in range(e_blk):
                    rows = pl.ds(bases[e], MOE_WIN)
                    o_ref[e, rows, c0:c0 + chunk] = (o_ref[e, rows, c0:c0 + chunk]
                                                     + res[e * MOE_WIN:(e + 1) * MOE_WIN, :])
            return carry

        return lax.fori_loop(0, n_win, window, carry)

    lax.fori_loop(0, slot_ref.shape[1], token_tile, 0)


def _dispatch(off, slot_row, h2, *, batch, n_seq, cap):
    e_blk = 8
    tiles = n_seq // MOE_TILE
    return pl.pallas_call(
        functools.partial(_dispatch_kernel, e_blk=e_blk, cap=cap),
        out_shape=jax.ShapeDtypeStruct((N_EXP, batch * cap, D), BF16),
        grid_spec=pltpu.PrefetchScalarGridSpec(
            num_scalar_prefetch=1,
            grid=(batch, N_EXP // e_blk),
            in_specs=[
                pl.BlockSpec((None, e_blk, tiles, 1, MOE_TILE), lambda b, j, off: (b, j, 0, 0, 0)),
                pl.BlockSpec((n_seq, D), lambda b, j, off: (b, 0)),
            ],
            out_specs=pl.BlockSpec((e_blk, cap, D), lambda b, j, off: (j, b, 0)),
        ),
        compiler_params=_cparams(("arbitrary", "arbitrary")),
        name="ec_dispatch",
    )(off, slot_row.reshape(batch, N_EXP, tiles, 1, MOE_TILE), h2)


def _gate_up_kernel(x_ref, wg_ref, wu_ref, o_ref):
    x = x_ref[...]
    part = 256
    for c0 in range(0, o_ref.shape[1], part):
        g = jnp.dot(x, wg_ref[:, c0:c0 + part].astype(BF16), preferred_element_type=F32)
        u = jnp.dot(x, wu_ref[:, c0:c0 + part].astype(BF16), preferred_element_type=F32)
        o_ref[:, c0:c0 + part] = (jax.nn.silu(g) * u).astype(BF16)


def _down_kernel(h_ref, wd_ref, *rest):
    if len(rest) > 1:
        c_ref, wm_ref, bm_ref, o_ref, mod_ref = rest
        _mod_kernel(c_ref, wm_ref, bm_ref, mod_ref)
    else:
        o_ref, = rest
    y = jnp.dot(h_ref[...], wd_ref[...].astype(BF16), preferred_element_type=F32).astype(BF16)
    cap = o_ref.shape[1]
    for b in range(o_ref.shape[0]):
        o_ref[b] = y[b * cap:(b + 1) * cap, :]


MOD_SIDE_TN = 256


def _experts(xs, w_gate, w_up, w_down, layer, *, batch, cap, mod_side=None):
    m = batch * cap
    tn, tn_down = TN_UP, TN_DOWN
    hid = pl.pallas_call(
        _gate_up_kernel,
        out_shape=jax.ShapeDtypeStruct((N_EXP, m, EXP_FF), BF16),
        grid=(N_EXP, EXP_FF // tn),
        in_specs=[
            pl.BlockSpec((None, m, D), lambda e, j: (e, 0, 0)),
            pl.BlockSpec((None, None, D, tn), lambda e, j: (layer, e, 0, j)),
            pl.BlockSpec((None, None, D, tn), lambda e, j: (layer, e, 0, j)),
        ],
        out_specs=pl.BlockSpec((None, m, tn), lambda e, j: (e, 0, j)),
        compiler_params=_cparams(("arbitrary", "arbitrary")),
        name="expert_gate_up",
    )(xs, w_gate, w_up)
    steps = D // tn_down
    in_specs = [
        pl.BlockSpec((None, m, EXP_FF), lambda e, j: (e, 0, 0)),
        pl.BlockSpec((None, None, EXP_FF, tn_down), lambda e, j: (layer, e, 0, j)),
    ]
    out_shape = [jax.ShapeDtypeStruct((batch, N_EXP * cap, D), BF16)]
    out_specs = [pl.BlockSpec((batch, cap, tn_down), lambda e, j: (0, e, j))]
    args = [hid, w_down]
    if mod_side is not None:
        c_pad, w_mod, b_mod, l = mod_side
        n_blk = 6 * D // MOD_SIDE_TN
        assert n_blk <= N_EXP * steps
        blk = lambda e, j: jnp.minimum(e * steps + j, n_blk - 1)
        in_specs += [
            pl.BlockSpec((MOD_ROWS, D), lambda e, j: (0, 0)),
            pl.BlockSpec((None, D, MOD_SIDE_TN), lambda e, j: (l, 0, blk(e, j))),
            pl.BlockSpec((None, 1, MOD_SIDE_TN), lambda e, j: (l, 0, blk(e, j))),
        ]
        out_shape.append(jax.ShapeDtypeStruct((MOD_ROWS, 6 * D), F32))
        out_specs.append(pl.BlockSpec((MOD_ROWS, MOD_SIDE_TN), lambda e, j: (0, blk(e, j))))
        args += [c_pad, w_mod, b_mod.reshape(DEPTH, 1, 6 * D)]
    out = pl.pallas_call(
        _down_kernel,
        out_shape=tuple(out_shape),
        grid=(N_EXP, steps),
        in_specs=in_specs,
        out_specs=tuple(out_specs),
        compiler_params=_cparams(("arbitrary", "arbitrary")),
        name="expert_down",
    )(*args)
    return out if mod_side is not None else out[0]


def _combine_kernel(off_ref, ye_ref, slot_ref, gate_ref, x_ref, g2_ref, lg_ref, lb_ref, o_ref, moe_ref, *, cap):
    b, tile = pl.program_id(0), pl.program_id(1)
    firsts, n_win = _windows(off_ref, b * N_EXP * OFF_STRIDE, N_EXP, tile, cap)
    lane_e = lax.broadcasted_iota(jnp.int32, (LANES, N_EXP * MOE_WIN), 1) // MOE_WIN
    spread = jnp.where(lax.broadcasted_iota(jnp.int32, (LANES, N_EXP * MOE_WIN), 0) == lane_e, 1.0, 0.0).astype(BF16)
    slot = jnp.dot(slot_ref[...].astype(BF16), spread, preferred_element_type=F32)
    gate = jnp.dot(gate_ref[...].astype(BF16), spread, preferred_element_type=F32).astype(BF16)
    siota = lax.broadcasted_iota(jnp.int32, (1, MOE_WIN), 1)

    def window(w):
        starts, targets, rows = [], [], []
        for e in range(N_EXP):
            start = firsts[e] + w * MOE_WIN
            base = pl.multiple_of(jnp.minimum(start, cap - MOE_WIN), 16)
            starts.append(jnp.full((1, MOE_WIN), start, jnp.int32))
            targets.append(base + siota)
            rows.append(ye_ref[pl.ds(e * cap + base, MOE_WIN), :])
        start_l = jnp.concatenate(starts, axis=1).astype(F32)
        target_l = jnp.concatenate(targets, axis=1).astype(F32)
        comb = jnp.where((slot == target_l) & (slot >= start_l), gate, jnp.zeros_like(gate))
        return jnp.dot(comb, jnp.concatenate(rows, axis=0), preferred_element_type=F32)

    moe_ref[...] = window(0)

    @pl.when(n_win > 1)
    def _():
        def more(w, carry):
            moe_ref[...] += window(w)
            return carry
        lax.fori_loop(1, n_win, more, 0)

    o_ref[...] = _ln(ALPHA * x_ref[...] + g2_ref[...] * moe_ref[...]) * lg_ref[...] + lb_ref[...]


def _combine(off, ye, slot_col, gate_col, x1, mod5, layer, ln_g, ln_b, *, batch, n_seq, cap):
    tm = MOE_TILE
    tiles = n_seq // tm
    vec_spec = pl.BlockSpec((1, D), lambda b, i, off: (0, 0))
    row_spec = lambda w: pl.BlockSpec((tm, w), lambda b, i, off: (b * tiles + i, 0))
    return pl.pallas_call(
        functools.partial(_combine_kernel, cap=cap),
        out_shape=jax.ShapeDtypeStruct((batch * n_seq, D), F32),
        grid_spec=pltpu.PrefetchScalarGridSpec(
            num_scalar_prefetch=1,
            grid=(batch, tiles),
            in_specs=[
                pl.BlockSpec((None, N_EXP * cap, D), lambda b, i, off: (b, 0, 0)),
                row_spec(LANES), row_spec(LANES), row_spec(D),
                pl.BlockSpec((None, None, None, 1, D), lambda b, i, off: (layer, b, 5, 0, 0)),
                vec_spec, vec_spec,
            ],
            out_specs=row_spec(D),
            scratch_shapes=[pltpu.VMEM((tm, D), F32)],
        ),
        compiler_params=_cparams(("arbitrary", "arbitrary")),
        name="ec_combine_ln",
    )(off, ye, slot_col, gate_col, x1, mod5, ln_g.reshape(1, D), ln_b.reshape(1, D))


def _moe_block(x1, h2, logits, mod5, w_gate, w_up, w_down, layer, ln_g, ln_b, *, batch, n_seq, mod_side=None):
    cap = 2 * n_seq // N_EXP
    assert n_seq % MOE_TILE == 0 and n_seq // MOE_TILE < OFF_STRIDE and cap % MOE_WIN == 0
    assert cap <= 256
    slot_row, slot_col, gate_col, off = _route(logits, batch=batch, n_seq=n_seq, cap=cap)
    off = off[:, :, :OFF_STRIDE].reshape(batch * N_EXP * OFF_STRIDE)
    xs = _dispatch(off, slot_row, h2, batch=batch, n_seq=n_seq, cap=cap)
    ye = _experts(xs, w_gate, w_up, w_down, layer, batch=batch, cap=cap, mod_side=mod_side)
    mod_next = None
    if mod_side is not None:
        ye, mod_next = ye
    out = _combine(off, ye, slot_col, gate_col, x1, mod5, 0, ln_g, ln_b, batch=batch, n_seq=n_seq, cap=cap)
    return out, mod_next


def _cd_in_kernel(x_ref, sh_ref, sc_ref, w_ref, lg_ref, lb_ref, u_ref, vg_ref, xg_ref):
    h = (_ln(x_ref[...]) * (1.0 + sc_ref[...]) + sh_ref[...]).astype(BF16)
    chunk = 512
    for c0 in range(0, SG_W, chunk):
        u_ref[:, c0:c0 + chunk] = jax.nn.gelu(
            jnp.dot(h, w_ref[:, c0:c0 + chunk], preferred_element_type=F32)).astype(BF16)
    for c0 in range(0, SG_W, chunk):
        z = jax.nn.gelu(jnp.dot(h, w_ref[:, SG_W + c0:SG_W + c0 + chunk], preferred_element_type=F32))
        parts = [_ln(z[:, j * HEAD:(j + 1) * HEAD]) for j in range(chunk // HEAD)]
        vg = jnp.concatenate(parts, axis=1) * lg_ref[:, c0:c0 + chunk] + lb_ref[:, c0:c0 + chunk]
        vg_ref[:, c0:c0 + chunk] = vg.astype(BF16)
    for c0 in range(0, CONV_CH, chunk):
        a = jnp.dot(h, w_ref[:, 2 * SG_W + c0:2 * SG_W + c0 + chunk], preferred_element_type=F32)
        gt = jnp.dot(h, w_ref[:, 2 * SG_W + CONV_CH + c0:2 * SG_W + CONV_CH + c0 + chunk],
                     preferred_element_type=F32)
        xg_ref[:, c0:c0 + chunk] = a * jax.nn.sigmoid(gt)


def _cd_in_proj(x2d, mod5, layer, w_bf, sg_ln_g, sg_ln_b, *, n_seq):
    rows = x2d.shape[0]
    tm = 256
    tiles_per_seq = n_seq // tm
    mod_spec = lambda chunk: pl.BlockSpec((None, None, None, 1, D),
                                          lambda i: (layer, i // tiles_per_seq, chunk, 0, 0))
    vec_spec = pl.BlockSpec((1, SG_W), lambda i: (0, 0))
    row_spec = lambda w: pl.BlockSpec((tm, w), lambda i: (i, 0))
    return pl.pallas_call(
        _cd_in_kernel,
        out_shape=(jax.ShapeDtypeStruct((rows, SG_W), BF16),
                   jax.ShapeDtypeStruct((rows, SG_W), BF16),
                   jax.ShapeDtypeStruct((rows, CONV_CH), F32)),
        grid=(rows // tm,),
        in_specs=[row_spec(D), mod_spec(0), mod_spec(1),
                  pl.BlockSpec((D, CD_IN), lambda i: (0, 0)), vec_spec, vec_spec],
        out_specs=(row_spec(SG_W), row_spec(SG_W), row_spec(CONV_CH)),
        compiler_params=_cparams(("arbitrary",)),
        name="cd_in_proj",
    )(x2d, mod5, mod5, w_bf, sg_ln_g.reshape(1, SG_W), sg_ln_b.reshape(1, SG_W))


CONV_HALO = 16
CONV_ROWS = 32


def _cd_mix_kernel(u_ref, vg_ref, xp_ref, xc_ref, xn_ref, sgw_ref, sgb_ref, cw_ref, cb_ref, lg_ref, lb_ref,
                   o_ref, sh_ref, cv_ref, *, tiles_per_seq):
    i = pl.program_id(1)
    tm = xc_ref.shape[0]
    for c in range(tm // CHUNK):
        for g in range(SG_G):
            rs, cs = slice(c * CHUNK, (c + 1) * CHUNK), slice(g * HEAD, (g + 1) * HEAD)
            sp = jnp.dot(sgw_ref[g], vg_ref[rs, cs], preferred_element_type=F32) + sgb_ref[g]
            o_ref[rs, cs] = (u_ref[rs, cs].astype(F32) * sp).astype(BF16)
    sh_ref[0, 0:CONV_HALO, :] = jnp.where(i > 0, xp_ref[...], 0.0)
    sh_ref[0, CONV_HALO:CONV_HALO + tm, :] = xc_ref[...]
    sh_ref[0, CONV_HALO + tm:, :] = jnp.where(i < tiles_per_seq - 1, xn_ref[...], 0.0)
    keep = tm + 2 * CONV_HALO - SUBLANES
    for s in range(1, SUBLANES):
        sh_ref[s, 0:keep, :] = sh_ref[0, s:s + keep, :]
    off = CONV_HALO - CONV_K // 2
    for r0 in range(0, tm, CONV_ROWS):
        for c0 in range(0, CONV_CH, LANES):
            acc = jnp.zeros((CONV_ROWS, LANES), F32) + cb_ref[:, c0:c0 + LANES]
            for k in range(CONV_K):
                a = r0 + off + k
                acc = acc + (sh_ref[a % SUBLANES, a - a % SUBLANES:a - a % SUBLANES + CONV_ROWS, c0:c0 + LANES]
                             * cw_ref[k:k + 1, c0:c0 + LANES])
            cv_ref[r0:r0 + CONV_ROWS, c0:c0 + LANES] = acc
    y = _ln(cv_ref[...]) * lg_ref[...] + lb_ref[...]
    o_ref[:, SG_W:] = jax.nn.silu(y).astype(BF16)


def _cd_mix(u, vg, xg, sg_w, sg_b, conv_w, conv_b, conv_ln_g, conv_ln_b, *, batch, n_seq):
    tm = 256
    tiles = n_seq // tm
    hpt = tm // CONV_HALO
    n_halo = batch * n_seq // CONV_HALO
    row_spec = lambda w: pl.BlockSpec((tm, w), lambda b, i: (b * tiles + i, 0))
    vec_spec = pl.BlockSpec((1, CONV_CH), lambda b, i: (0, 0))
    sgb_full = jnp.broadcast_to(sg_b[:, :, None], (SG_G, CHUNK, HEAD))
    return pl.pallas_call(
        functools.partial(_cd_mix_kernel, tiles_per_seq=tiles),
        out_shape=jax.ShapeDtypeStruct((batch * n_seq, D), BF16),
        grid=(batch, tiles),
        in_specs=[
            row_spec(SG_W), row_spec(SG_W),
            pl.BlockSpec((CONV_HALO, CONV_CH), lambda b, i: (jnp.maximum((b * tiles + i) * hpt - 1, 0), 0)),
            row_spec(CONV_CH),
            pl.BlockSpec((CONV_HALO, CONV_CH), lambda b, i: (jnp.minimum((b * tiles + i + 1) * hpt, n_halo - 1), 0)),
            pl.BlockSpec((SG_G, CHUNK, CHUNK), lambda b, i: (0, 0, 0)),
            pl.BlockSpec((SG_G, CHUNK, HEAD), lambda b, i: (0, 0, 0)),
            pl.BlockSpec((CONV_K + 1, CONV_CH), lambda b, i: (0, 0)),
            vec_spec, vec_spec, vec_spec,
        ],
        out_specs=pl.BlockSpec((tm, D), lambda b, i: (b * tiles + i, 0)),
        scratch_shapes=[pltpu.VMEM((SUBLANES, tm + 2 * CONV_HALO, CONV_CH), F32), pltpu.VMEM((tm, CONV_CH), F32)],
        compiler_params=_cparams(("arbitrary", "arbitrary")),
        name="cd_mix",
    )(u, vg, xg, xg, xg, sg_w.astype(BF16), sgb_full,
      jnp.pad(conv_w.reshape(CONV_K, CONV_CH), ((0, 1), (0, 0))),
      conv_b.reshape(1, CONV_CH), conv_ln_g.reshape(1, CONV_CH), conv_ln_b.reshape(1, CONV_CH))


def kernel(x, c, ctx, c_ctx, w_mod, b_mod, ln1_g, ln1_b, ln2_g, ln2_b, w_router, w_gate, w_up, w_down,
           ab_w_in, ab_w_out, sink, cd_w_in, cd_w_out, sg_ln_g, sg_ln_b, sg_w, sg_b,
           conv_w, conv_b, conv_ln_g, conv_ln_b):
    batch, n_seq, _ = x.shape
    n_ctx = ctx.shape[1]
    assert x.shape[2] == D and batch + 1 <= MOD_ROWS

    c_pad = jnp.concatenate([c, c_ctx[None, :], jnp.zeros((MOD_ROWS - batch - 1, D), F32)], axis=0)
    mod5 = _modulation(c_pad, w_mod, b_mod, 0).reshape(1, MOD_ROWS, 6, 1, D)
    w_router_pad = jnp.pad(w_router, ((0, 0), (0, 0), (0, LANES - N_EXP))).astype(BF16)

    x0 = x.reshape(batch * n_seq, D)
    ctx2d = ctx.reshape(batch * n_ctx, D)

    w_in = ab_w_in[0]
    tabs = _rope_tables(n_seq)
    p_lat = _ab_in_proj(x0, mod5, 0, lambda b: b, w_in, tabs, n_seq=n_seq,
                        rope_cols=K_END, q_cols=Q_END, name="ab_in_proj")
    p_ctx = _ab_in_proj(ctx2d, mod5, 0, lambda b: batch, w_in[:, Q_END:V_END], tabs, n_seq=n_ctx,
                        rope_cols=0, q_cols=0, name="ab_ctx_proj")
    attn = _attention(p_lat, p_ctx, sink[0], batch=batch, n_seq=n_seq, n_ctx=n_ctx)
    four = _fourier(p_lat, batch=batch, n_seq=n_seq)
    x1, h2, logits = _out_proj([attn, four], ab_w_out[0], x0, mod5, 0, ln1_g[0], ln1_b[0],
                               w_router_pad[0], n_seq=n_seq, folded_last=True)
    x2, mod_l1 = _moe_block(x1, h2, logits, mod5, w_gate, w_up, w_down, 0, ln2_g[0], ln2_b[0],
                            batch=batch, n_seq=n_seq, mod_side=(c_pad, w_mod, b_mod, 1))
    mod5 = mod_l1.reshape(1, MOD_ROWS, 6, 1, D)

    u, vg, xg = _cd_in_proj(x2, mod5, 0, cd_w_in[0].astype(BF16), sg_ln_g[0], sg_ln_b[0], n_seq=n_seq)
    mix = _cd_mix(u, vg, xg, sg_w[0], sg_b[0], conv_w[0], conv_b[0], conv_ln_g[0], conv_ln_b[0],
                  batch=batch, n_seq=n_seq)
    x3, h4, logits1 = _out_proj([mix], cd_w_out[0], x2, mod5, 0, ln1_g[1], ln1_b[1],
                                w_router_pad[1], n_seq=n_seq)
    x4, _ = _moe_block(x3, h4, logits1, mod5, w_gate, w_up, w_down, 1, ln2_g[1], ln2_b[1], batch=batch, n_seq=n_seq)
    return x4.reshape(batch, n_seq, D)
```

```python
import functools

import jax
import jax.numpy as jnp
from jax import lax
from jax.experimental import pallas as pl
from jax.experimental.pallas import tpu as pltpu

F32 = jnp.float32
BF16 = jnp.bfloat16

D = 2048
HEAD = 128
N_Q = 12
N_KV = 4
GQA = 3
WINDOW = 128
GRID_W = 64
ROPE_BASE = 10000.0
Q_END = N_Q * HEAD
KV_W = N_KV * HEAD
K_END = Q_END + KV_W
V_END = K_END + KV_W
FOUR_G = 4
FOUR_W = FOUR_G * HEAD
AB_IN = V_END + FOUR_W
SG_G = 8
SG_W = SG_G * HEAD
CHUNK = 128
CONV_CH = 1024
CONV_K = 31
CD_IN = 4096
N_EXP = 16
EXP_FF = 2048
DEPTH = 2
ALPHA = (2 * DEPTH) ** 0.25
LN_EPS = 1e-6
ATTN_SCALE = HEAD ** -0.5
MOD_ROWS = 16
LANES = 128
SUBLANES = 8
VMEM_LIMIT = 56 * 1024 * 1024
MOE_TILE = 256
MOE_WIN = 64
OFF_STRIDE = 16
TN_UP = 512
TN_DOWN = 512


def _cparams(sem):
    return pltpu.CompilerParams(dimension_semantics=sem, vmem_limit_bytes=VMEM_LIMIT)


def _ln(x):
    mu = jnp.mean(x, axis=-1, keepdims=True)
    xc = x - mu
    var = jnp.mean(xc * xc, axis=-1, keepdims=True)
    return xc * lax.rsqrt(var + LN_EPS)


def _mod_kernel(c_ref, w_ref, b_ref, o_ref):
    s = jax.nn.silu(c_ref[...]).astype(BF16)
    o_ref[...] = jnp.dot(s, w_ref[...].astype(BF16), preferred_element_type=F32) + b_ref[...]


def _modulation(c_pad, w_mod, b_mod, layer):
    tn = 1024
    return pl.pallas_call(
        _mod_kernel,
        out_shape=jax.ShapeDtypeStruct((MOD_ROWS, 6 * D), F32),
        grid=(6 * D // tn,),
        in_specs=[
            pl.BlockSpec((MOD_ROWS, D), lambda j: (0, 0)),
            pl.BlockSpec((None, D, tn), lambda j: (layer, 0, j)),
            pl.BlockSpec((None, 1, tn), lambda j: (layer, 0, j)),
        ],
        out_specs=pl.BlockSpec((MOD_ROWS, tn), lambda j: (0, j)),
        compiler_params=_cparams(("arbitrary",)),
        name="modulation",
    )(c_pad, w_mod, b_mod.reshape(DEPTH, 1, 6 * D))


def _rope(a, cos, sa, sb):
    return a * cos + pltpu.roll(a, HEAD - 32, 1) * sa + pltpu.roll(a, 32, 1) * sb


def _ab_in_kernel(x_ref, sh_ref, sc_ref, wf_ref, cos_ref, sa_ref, sb_ref, o_ref, w_ref, *, n_cols, rope_cols, q_cols):
    @pl.when(pl.program_id(0) == 0)
    def _():
        for r0 in range(0, D, 256):
            w_ref[r0:r0 + 256, :] = wf_ref[r0:r0 + 256, :].astype(BF16)

    h = (_ln(x_ref[...]) * (1.0 + sc_ref[...]) + sh_ref[...]).astype(BF16)
    chunk = 512
    for c0 in range(0, n_cols, chunk):
        acc = jnp.dot(h, w_ref[:, c0:c0 + chunk], preferred_element_type=F32)
        if c0 < rope_cols:
            cos, sa, sb = cos_ref[...], sa_ref[...], sb_ref[...]
            parts = []
            for j in range(chunk // HEAD):
                r = _rope(acc[:, j * HEAD:(j + 1) * HEAD], cos, sa, sb)
                parts.append(r * ATTN_SCALE if c0 < q_cols else r)
            acc = jnp.concatenate(parts, axis=1)
        o_ref[:, c0:c0 + chunk] = acc.astype(BF16)


def _ab_in_proj(x2d, mod5, layer, row_of_tile, w, rope_tabs, *, n_seq, rope_cols, q_cols, name):
    rows, _ = x2d.shape
    n_cols = w.shape[1]
    tm = 256
    tiles_per_seq = n_seq // tm
    cos, sa, sb = rope_tabs
    kern = functools.partial(_ab_in_kernel, n_cols=n_cols, rope_cols=rope_cols, q_cols=q_cols)
    mod_spec = lambda chunk: pl.BlockSpec((None, None, None, 1, D),
                                          lambda i: (layer, row_of_tile(i // tiles_per_seq), chunk, 0, 0))
    tab_spec = pl.BlockSpec((tm, HEAD), lambda i: (i % tiles_per_seq, 0))
    return pl.pallas_call(
        kern,
        out_shape=jax.ShapeDtypeStruct((rows, n_cols), BF16),
        grid=(rows // tm,),
        in_specs=[
            pl.BlockSpec((tm, D), lambda i: (i, 0)),
            mod_spec(0), mod_spec(1),
            pl.BlockSpec((D, n_cols), lambda i: (0, 0), pipeline_mode=pl.Buffered(1)),
            tab_spec, tab_spec, tab_spec,
        ],
        out_specs=pl.BlockSpec((tm, n_cols), lambda i: (i, 0)),
        scratch_shapes=[pltpu.VMEM((D, n_cols), BF16)],
        compiler_params=_cparams(("arbitrary",)),
        name=name,
    )(x2d, mod5, mod5, w, cos, sa, sb)


def _rope_tables(n):
    quarter = HEAD // 4
    t = jnp.arange(n)
    r = (t // GRID_W).astype(F32)
    col = (t % GRID_W).astype(F32)
    inv = ROPE_BASE ** (-jnp.arange(quarter, dtype=F32) / quarter)
    ang_r, ang_c = r[:, None] * inv, col[:, None] * inv
    zero = jnp.zeros_like(ang_r)
    cos = jnp.concatenate([jnp.cos(ang_r)] * 2 + [jnp.cos(ang_c)] * 2, axis=1)
    sa = jnp.concatenate([-jnp.sin(ang_r), zero, -jnp.sin(ang_c), zero], axis=1)
    sb = jnp.concatenate([zero, jnp.sin(ang_r), zero, jnp.sin(ang_c)], axis=1)
    return cos, sa, sb


def _attn_kernel(sink_ref, q_ref, k_ref, v_ref, kc_ref, vc_ref, o_ref, bias_ref, *, n_seq):
    hk = pl.program_id(1)
    kc = kc_ref[...]
    vc = vc_ref[...]
    kw_len = 3 * WINDOW
    rows = GQA * WINDOW
    n_blocks = n_seq // WINDOW
    row = lax.broadcasted_iota(jnp.int32, (rows, 1), 0)
    sink = jnp.where(row < WINDOW, sink_ref[hk * GQA],
                     jnp.where(row < 2 * WINDOW, sink_ref[hk * GQA + 1], sink_ref[hk * GQA + 2]))
    rel = (row & (WINDOW - 1)) - lax.broadcasted_iota(jnp.int32, (1, kw_len), 1)
    for i in range(3):
        bias_ref[i] = jnp.where(jnp.abs(rel + i * WINDOW) <= WINDOW, 0.0, -1e30)
    dn = (((1,), (1,)), ((), ()))

    def block(n, back):
        r0 = n * WINDOW if isinstance(n, int) else pl.multiple_of(n * WINDOW, WINDOW)
        start = r0 - back * WINDOW
        if not isinstance(start, int):
            start = pl.multiple_of(start, WINDOW)
        qs = q_ref[pl.ds(r0, WINDOW), :]
        q3 = jnp.concatenate([qs[:, g * HEAD:(g + 1) * HEAD] for g in range(GQA)], axis=0)
        kw = k_ref[pl.ds(start, kw_len), :]
        vw = v_ref[pl.ds(start, kw_len), :]
        s_w = lax.dot_general(q3, kw, dn, preferred_element_type=F32) + bias_ref[back]
        s_c = lax.dot_general(q3, kc, dn, preferred_element_type=F32)
        m = jnp.maximum(jnp.maximum(jnp.max(s_w, axis=1, keepdims=True), jnp.max(s_c, axis=1, keepdims=True)), sink)
        p_w = jnp.exp(s_w - m)
        p_c = jnp.exp(s_c - m)
        den = jnp.sum(p_w, axis=1, keepdims=True) + jnp.sum(p_c, axis=1, keepdims=True) + jnp.exp(sink - m)
        o = (jnp.dot(p_w.astype(BF16), vw, preferred_element_type=F32)
             + jnp.dot(p_c.astype(BF16), vc, preferred_element_type=F32)) / den
        o_ref[pl.ds(r0, WINDOW), :] = jnp.concatenate(
            [o[g * WINDOW:(g + 1) * WINDOW, :] for g in range(GQA)], axis=1).astype(BF16)

    block(0, 0)
    block(n_blocks - 1, 2)

    per_iter = max(g for g in (7, 2, 1) if (n_blocks - 2) % g == 0)

    def interior(i, carry):
        for g in range(per_iter):
            block(1 + per_iter * i + g, 1)
        return carry

    lax.fori_loop(0, (n_blocks - 2) // per_iter, interior, 0)


def _attention(p_lat, p_ctx, sink, *, batch, n_seq, n_ctx):
    qw = GQA * HEAD
    assert n_seq % WINDOW == 0 and n_seq >= 3 * WINDOW
    return pl.pallas_call(
        functools.partial(_attn_kernel, n_seq=n_seq),
        out_shape=jax.ShapeDtypeStruct((batch * n_seq, Q_END), BF16),
        grid=(batch, N_KV),
        in_specs=[
            pl.BlockSpec(memory_space=pltpu.SMEM),
            pl.BlockSpec((n_seq, qw), lambda b, h: (b, h)),
            pl.BlockSpec((n_seq, HEAD), lambda b, h: (b, Q_END // HEAD + h)),
            pl.BlockSpec((n_seq, HEAD), lambda b, h: (b, K_END // HEAD + h)),
            pl.BlockSpec((n_ctx, HEAD), lambda b, h: (b, h)),
            pl.BlockSpec((n_ctx, HEAD), lambda b, h: (b, N_KV + h)),
        ],
        out_specs=pl.BlockSpec((n_seq, qw), lambda b, h: (b, h)),
        scratch_shapes=[pltpu.VMEM((3, GQA * WINDOW, 3 * WINDOW), F32)],
        compiler_params=_cparams(("arbitrary", "arbitrary")),
        name="banded_gqa",
    )(sink, p_lat, p_lat, p_lat, p_ctx, p_ctx)


DFT_ROWS = 32


def _fourier_kernel(z_ref, cs_ref, ca_ref, sa_ref, cb_ref, sb_ref, o_ref, cn_ref, sn_ref):
    n = z_ref.shape[0]
    half = n // 2

    @pl.when(pl.program_id(0) == 0)
    def _():
        cb, sb = cb_ref[...], sb_ref[...]

        def rows(a, carry):
            ca, sa = ca_ref[a], sa_ref[a]
            r = pl.ds(pl.multiple_of(a * DFT_ROWS, DFT_ROWS), DFT_ROWS)
            cn_ref[r, :] = (ca * cb - sa * sb).astype(BF16)
            sn_ref[r, :] = (sa * cb + ca * sb).astype(BF16)
            return carry

        lax.fori_loop(0, ca_ref.shape[0], rows, 0)

    cs = cs_ref[...]
    a_parts, b_parts = [], []
    for g in range(FOUR_G):
        ab = jnp.dot(z_ref[:, g * HEAD:(g + 1) * HEAD], cs, preferred_element_type=F32)
        a_parts.append(ab[:, :HEAD])
        b_parts.append(ab[:, HEAD:])
    a = jnp.concatenate(a_parts, axis=1).astype(BF16)
    b = jnp.concatenate(b_parts, axis=1).astype(BF16)
    p = jnp.dot(cn_ref[...], a, preferred_element_type=F32)
    q = jnp.dot(sn_ref[...], b, preferred_element_type=F32)
    scale = 1.0 / (n * HEAD) ** 0.5
    o_ref[0:half, :] = ((p[0:half, :] - q[0:half, :]) * scale).astype(BF16)
    o_ref[half:n, :] = ((p[1:half + 1, :] + q[1:half + 1, :]) * scale).astype(BF16)


def _dft_rows(j, m):
    ang = ((j[:, None] * jnp.arange(m, dtype=jnp.int32)[None, :]) % m).astype(F32) * (2.0 * jnp.pi / m)
    return jnp.cos(ang), jnp.sin(ang)


def _fourier(p_lat, *, batch, n_seq):
    assert n_seq % (2 * DFT_ROWS) == 0
    n_a = n_seq // (2 * DFT_ROWS) + 1
    cc, sc = _dft_rows(jnp.arange(HEAD, dtype=jnp.int32), HEAD)
    cs = jnp.concatenate([cc, sc], axis=1).astype(BF16)
    ca, sa = _dft_rows(jnp.arange(n_a, dtype=jnp.int32) * DFT_ROWS, n_seq)
    cb, sb = _dft_rows(jnp.arange(DFT_ROWS, dtype=jnp.int32), n_seq)
    coarse_spec = pl.BlockSpec((n_a, 1, n_seq), lambda b: (0, 0, 0))
    fine_spec = pl.BlockSpec((DFT_ROWS, n_seq), lambda b: (0, 0))
    return pl.pallas_call(
        _fourier_kernel,
        out_shape=jax.ShapeDtypeStruct((batch * n_seq, FOUR_W), BF16),
        grid=(batch,),
        in_specs=[
            pl.BlockSpec((n_seq, FOUR_W), lambda b: (b, V_END // FOUR_W)),
            pl.BlockSpec((HEAD, 2 * HEAD), lambda b: (0, 0)),
            coarse_spec, coarse_spec, fine_spec, fine_spec,
        ],
        out_specs=pl.BlockSpec((n_seq, FOUR_W), lambda b: (b, 0)),
        scratch_shapes=[pltpu.VMEM((n_a * DFT_ROWS, n_seq), BF16), pltpu.VMEM((n_a * DFT_ROWS, n_seq), BF16)],
        compiler_params=_cparams(("arbitrary",)),
        name="fourier_mix",
    )(p_lat, cs, ca.reshape(n_a, 1, n_seq), sa.reshape(n_a, 1, n_seq), cb, sb)


def _out_proj_kernel(*refs, n_parts, folded):
    a_refs = refs[:n_parts]
    rest = refs[n_parts:]
    if folded:
        flip_ref, rest = rest[0], rest[1:]
    (wf_ref, x_ref, g1_ref, lg_ref, lb_ref, sh_ref, sc_ref, wr_ref, x1_ref, h2_ref, lo_ref, w_ref) = rest

    @pl.when(pl.program_id(0) == 0)
    def _():
        for r0 in range(0, D, 256):
            w_ref[r0:r0 + 256, :] = wf_ref[r0:r0 + 256, :].astype(BF16)

    y = None
    k0 = 0
    for p, a_ref in enumerate(a_refs):
        kk = a_ref.shape[1]
        a = a_ref[...]
        if folded and p == n_parts - 1:
            a = jnp.dot(flip_ref[...], a, preferred_element_type=F32).astype(BF16)
        t = jnp.dot(a, w_ref[k0:k0 + kk, :], preferred_element_type=F32)
        y = t if y is None else y + t
        k0 += kk
    x1 = _ln(ALPHA * x_ref[...] + g1_ref[...] * y) * lg_ref[...] + lb_ref[...]
    x1_ref[...] = x1
    h2 = (_ln(x1) * (1.0 + sc_ref[...]) + sh_ref[...]).astype(BF16)
    h2_ref[...] = h2
    lo_ref[...] = jnp.dot(h2, wr_ref[...], preferred_element_type=F32)


def _out_proj(parts, w, x2d, mod5, layer, ln_g, ln_b, w_router_pad, *, n_seq, folded_last=False):
    rows = x2d.shape[0]
    tm = 256
    tiles_per_seq = n_seq // tm
    half_tiles = tiles_per_seq // 2
    mod_spec = lambda chunk: pl.BlockSpec((None, None, None, 1, D),
                                          lambda i: (layer, i // tiles_per_seq, chunk, 0, 0))
    vec_spec = pl.BlockSpec((1, D), lambda i: (0, 0))
    in_specs = [pl.BlockSpec((tm, p.shape[1]), lambda i: (i, 0)) for p in parts]
    extra = []
    if folded_last:
        assert tiles_per_seq % 2 == 0

        def folded_tile(i):
            li = i % tiles_per_seq
            return i - li + jnp.where(li < half_tiles, li, half_tiles + tiles_per_seq - 1 - li)

        in_specs[-1] = pl.BlockSpec((tm, parts[-1].shape[1]), lambda i: (folded_tile(i), 0))
        in_specs.append(pl.BlockSpec((None, tm, tm), lambda i: ((i % tiles_per_seq) // half_tiles, 0, 0)))
        eye = jnp.eye(tm, dtype=BF16)
        extra = [jnp.stack([eye, eye[::-1]])]
    in_specs += [
        pl.BlockSpec((D, D), lambda i: (0, 0), pipeline_mode=pl.Buffered(1)),
        pl.BlockSpec((tm, D), lambda i: (i, 0)),
        mod_spec(2), vec_spec, vec_spec, mod_spec(3), mod_spec(4),
        pl.BlockSpec((D, LANES), lambda i: (0, 0)),
    ]
    return pl.pallas_call(
        functools.partial(_out_proj_kernel, n_parts=len(parts), folded=folded_last),
        out_shape=(jax.ShapeDtypeStruct((rows, D), F32),
                   jax.ShapeDtypeStruct((rows, D), BF16),
                   jax.ShapeDtypeStruct((rows, LANES), F32)),
        grid=(rows // tm,),
        in_specs=in_specs,
        out_specs=(pl.BlockSpec((tm, D), lambda i: (i, 0)),
                   pl.BlockSpec((tm, D), lambda i: (i, 0)),
                   pl.BlockSpec((tm, LANES), lambda i: (i, 0))),
        scratch_shapes=[pltpu.VMEM((D, D), BF16)],
        compiler_params=_cparams(("arbitrary",)),
        name="out_proj_ln",
    )(*parts, *extra, w, x2d, mod5, ln_g.reshape(1, D), ln_b.reshape(1, D), mod5, mod5, w_router_pad)


def _route_kernel(lo_ref, slot_row_ref, slot_col_ref, gate_col_ref, off_ref, tri_ref, *, n_seq, cap, n_b):
    @pl.when(pl.program_id(0) == 0)
    def _():
        blk = 256
        r = lax.broadcasted_iota(jnp.int32, (blk, n_seq), 0)
        c = lax.broadcasted_iota(jnp.int32, (blk, n_seq), 1)
        for r0 in range(0, n_seq, blk):
            tri_ref[r0:r0 + blk, :] = jnp.where(r + r0 < c, 1.0, 0.0).astype(BF16)

    affs = []
    for b in range(n_b):
        lt = jnp.transpose(lo_ref[b * n_seq:(b + 1) * n_seq, :])[:N_EXP, :]
        ex = jnp.exp(lt - jnp.max(lt, axis=0, keepdims=True))
        affs.append(ex / jnp.sum(ex, axis=0, keepdims=True))
    aff = jnp.concatenate(affs, axis=0)
    n_rows = n_b * N_EXP

    def count_ge(t):
        return jnp.sum(jnp.where(aff >= t, 1.0, 0.0), axis=1, keepdims=True)

    def bisect_log(_, lohi):
        lo, hi = lohi
        mid = 0.5 * (lo + hi)
        ok = count_ge(jnp.exp(mid)) >= cap
        return jnp.where(ok, mid, lo), jnp.where(ok, hi, mid)

    def bisect_lin(_, lohi):
        lo, hi = lohi
        mid = 0.5 * (lo + hi)
        ok = count_ge(mid) >= cap
        return jnp.where(ok, mid, lo), jnp.where(ok, hi, mid)

    lo0 = jnp.full((n_rows, 1), -150.0, F32)
    hi0 = jnp.full((n_rows, 1), 1.0, F32)
    lo_l, hi_l = lax.fori_loop(0, 18, bisect_log, (lo0, hi0))
    thr, above = lax.fori_loop(0, 34, bisect_lin, (jnp.exp(lo_l), jnp.exp(hi_l)))
    gt = aff >= above
    eq = (aff >= thr) & (aff < above)
    need = cap - jnp.sum(jnp.where(gt, 1.0, 0.0), axis=1, keepdims=True)
    tri = tri_ref[...]
    eq_before = jnp.dot(jnp.where(eq, 1.0, 0.0).astype(BF16), tri, preferred_element_type=F32)
    sel = gt | (eq & (eq_before < need))
    sel_bf = jnp.where(sel, 1.0, 0.0).astype(BF16)
    slot = jnp.where(sel, jnp.dot(sel_bf, tri, preferred_element_type=F32), -1.0)
    gate = jnp.where(sel, aff, 0.0)
    tt = lax.broadcasted_iota(jnp.int32, (n_seq, LANES), 0)
    ti = lax.broadcasted_iota(jnp.int32, (n_seq, LANES), 1)
    before = jnp.where(tt < ti * MOE_TILE, 1.0, 0.0).astype(BF16)
    off = jnp.dot(sel_bf, before, preferred_element_type=F32).astype(jnp.int32)
    pad = jnp.full((LANES - N_EXP, n_seq), -1.0, F32)
    for b in range(n_b):
        rows = slice(b * N_EXP, (b + 1) * N_EXP)
        slot_row_ref[b] = slot[rows, :]
        off_ref[b] = off[rows, :]
        slot_col_ref[b * n_seq:(b + 1) * n_seq, :] = jnp.transpose(jnp.concatenate([slot[rows, :], pad], axis=0))
        gate_col_ref[b * n_seq:(b + 1) * n_seq, :] = jnp.transpose(jnp.concatenate([gate[rows, :], pad * 0.0], axis=0))


def _route(logits, *, batch, n_seq, cap):
    n_b = 4 if batch % 4 == 0 else 1
    return pl.pallas_call(
        functools.partial(_route_kernel, n_seq=n_seq, cap=cap, n_b=n_b),
        out_shape=(jax.ShapeDtypeStruct((batch, N_EXP, n_seq), F32),
                   jax.ShapeDtypeStruct((batch * n_seq, LANES), F32),
                   jax.ShapeDtypeStruct((batch * n_seq, LANES), F32),
                   jax.ShapeDtypeStruct((batch, N_EXP, LANES), jnp.int32)),
        grid=(batch // n_b,),
        in_specs=[pl.BlockSpec((n_b * n_seq, LANES), lambda b: (b, 0))],
        out_specs=(pl.BlockSpec((n_b, N_EXP, n_seq), lambda b: (b, 0, 0)),
                   pl.BlockSpec((n_b * n_seq, LANES), lambda b: (b, 0)),
                   pl.BlockSpec((n_b * n_seq, LANES), lambda b: (b, 0)),
                   pl.BlockSpec((n_b, N_EXP, LANES), lambda b: (b, 0, 0))),
        scratch_shapes=[pltpu.VMEM((n_seq, n_seq), BF16)],
        compiler_params=_cparams(("arbitrary",)),
        name="ec_route",
    )(logits)


def _windows(off_ref, row0, n_exp, tile, cap):
    firsts, n_win = [], 0
    for e in range(n_exp):
        lo = off_ref[row0 + e * OFF_STRIDE + tile]
        hi = off_ref[row0 + e * OFF_STRIDE + tile + 1]
        first = (lo >> 4) << 4
        firsts.append(first)
        n_win = jnp.maximum(n_win, (hi - first + MOE_WIN - 1) // MOE_WIN)
    return firsts, n_win


def _dispatch_kernel(off_ref, slot_ref, h_ref, o_ref, *, e_blk, cap):
    b, half = pl.program_id(0), pl.program_id(1)
    o_ref[...] = jnp.zeros(o_ref.shape, o_ref.dtype)
    siota = lax.broadcasted_iota(jnp.int32, (MOE_WIN, 1), 0)
    n_cols = h_ref.shape[1]
    chunk = 512

    def token_tile(tile, carry):
        firsts, n_win = _windows(off_ref, (b * N_EXP + half * e_blk) * OFF_STRIDE, e_blk, tile, cap)
        t0 = pl.multiple_of(tile * MOE_TILE, MOE_TILE)

        def window(w, carry):
            bases, parts = [], []
            for e in range(e_blk):
                start = firsts[e] + w * MOE_WIN
                base = pl.multiple_of(jnp.minimum(start, cap - MOE_WIN), 16)
                srow = slot_ref[e, tile]
                srow = jnp.where(srow >= start.astype(F32), srow, -1.0)
                parts.append(jnp.where(srow == (base + siota).astype(F32), 1.0, 0.0).astype(BF16))
                bases.append(base)
            onehot = jnp.concatenate(parts, axis=0)
            for c0 in range(0, n_cols, chunk):
                res = jnp.dot(onehot, h_ref[pl.ds(t0, MOE_TILE), c0:c0 + chunk],
                              preferred_element_type=F32).astype(BF16)
                for e in range(e_blk):
                    rows = pl.ds(bases[e], MOE_WIN)
                    o_ref[e, rows, c0:c0 + chunk] = (o_ref[e, rows, c0:c0 + chunk]
                                                     + res[e * MOE_WIN:(e + 1) * MOE_WIN, :])
            return carry

        return lax.fori_loop(0, n_win, window, carry)

    lax.fori_loop(0, slot_ref.shape[1], token_tile, 0)


def _dispatch(off, slot_row, h2, *, batch, n_seq, cap):
    e_blk = 8
    tiles = n_seq // MOE_TILE
    return pl.pallas_call(
        functools.partial(_dispatch_kernel, e_blk=e_blk, cap=cap),
        out_shape=jax.ShapeDtypeStruct((N_EXP, batch * cap, D), BF16),
        grid_spec=pltpu.PrefetchScalarGridSpec(
            num_scalar_prefetch=1,
            grid=(batch, N_EXP // e_blk),
            in_specs=[
                pl.BlockSpec((None, e_blk, tiles, 1, MOE_TILE), lambda b, j, off: (b, j, 0, 0, 0)),
                pl.BlockSpec((n_seq, D), lambda b, j, off: (b, 0)),
            ],
            out_specs=pl.BlockSpec((e_blk, cap, D), lambda b, j, off: (j, b, 0)),
        ),
        compiler_params=_cparams(("arbitrary", "arbitrary")),
        name="ec_dispatch",
    )(off, slot_row.reshape(batch, N_EXP, tiles, 1, MOE_TILE), h2)


def _gate_up_kernel(x_ref, wg_ref, wu_ref, o_ref):
    x = x_ref[...]
    part = 256
    for c0 in range(0, o_ref.shape[1], part):
        g = jnp.dot(x, wg_ref[:, c0:c0 + part].astype(BF16), preferred_element_type=F32)
        u = jnp.dot(x, wu_ref[:, c0:c0 + part].astype(BF16), preferred_element_type=F32)
        o_ref[:, c0:c0 + part] = (jax.nn.silu(g) * u).astype(BF16)


def _down_kernel(h_ref, wd_ref, *rest, n_mod_steps=0):
    if len(rest) > 1:
        c_ref, wm_ref, bm_ref, wc_ref, o_ref, mod_ref, wcb_ref = rest
        step = pl.program_id(0) * pl.num_programs(1) + pl.program_id(1)

        @pl.when(step < n_mod_steps)
        def _():
            _mod_kernel(c_ref, wm_ref, bm_ref, mod_ref)

        @pl.when(step >= n_mod_steps)
        def _():
            wcb_ref[...] = wc_ref[...].astype(BF16)
    else:
        o_ref, = rest
    y = jnp.dot(h_ref[...], wd_ref[...].astype(BF16), preferred_element_type=F32).astype(BF16)
    cap = o_ref.shape[1]
    for b in range(o_ref.shape[0]):
        o_ref[b] = y[b * cap:(b + 1) * cap, :]


MOD_SIDE_TN = 256


def _experts(xs, w_gate, w_up, w_down, layer, *, batch, cap, mod_side=None):
    m = batch * cap
    tn, tn_down = TN_UP, TN_DOWN
    hid = pl.pallas_call(
        _gate_up_kernel,
        out_shape=jax.ShapeDtypeStruct((N_EXP, m, EXP_FF), BF16),
        grid=(N_EXP, EXP_FF // tn),
        in_specs=[
            pl.BlockSpec((None, m, D), lambda e, j: (e, 0, 0)),
            pl.BlockSpec((None, None, D, tn), lambda e, j: (layer, e, 0, j)),
            pl.BlockSpec((None, None, D, tn), lambda e, j: (layer, e, 0, j)),
        ],
        out_specs=pl.BlockSpec((None, m, tn), lambda e, j: (e, 0, j)),
        compiler_params=_cparams(("arbitrary", "arbitrary")),
        name="expert_gate_up",
    )(xs, w_gate, w_up)
    steps = D // tn_down
    in_specs = [
        pl.BlockSpec((None, m, EXP_FF), lambda e, j: (e, 0, 0)),
        pl.BlockSpec((None, None, EXP_FF, tn_down), lambda e, j: (layer, e, 0, j)),
    ]
    out_shape = [jax.ShapeDtypeStruct((batch, N_EXP * cap, D), BF16)]
    out_specs = [pl.BlockSpec((batch, cap, tn_down), lambda e, j: (0, e, j))]
    args = [hid, w_down]
    n_blk = 0
    if mod_side is not None:
        c_pad, w_mod, b_mod, l, w_cast = mod_side
        n_blk = 6 * D // MOD_SIDE_TN
        n_cast = w_cast.shape[1] // MOD_SIDE_TN
        assert n_blk + n_cast <= N_EXP * steps and w_cast.shape[1] % MOD_SIDE_TN == 0
        blk = lambda e, j: jnp.minimum(e * steps + j, n_blk - 1)
        cast_blk = lambda e, j: jnp.clip(e * steps + j - n_blk, 0, n_cast - 1)
        in_specs += [
            pl.BlockSpec((MOD_ROWS, D), lambda e, j: (0, 0)),
            pl.BlockSpec((None, D, MOD_SIDE_TN), lambda e, j: (l, 0, blk(e, j))),
            pl.BlockSpec((None, 1, MOD_SIDE_TN), lambda e, j: (l, 0, blk(e, j))),
            pl.BlockSpec((w_cast.shape[0], MOD_SIDE_TN), lambda e, j: (0, cast_blk(e, j))),
        ]
        out_shape += [jax.ShapeDtypeStruct((MOD_ROWS, 6 * D), F32), jax.ShapeDtypeStruct(w_cast.shape, BF16)]
        out_specs += [pl.BlockSpec((MOD_ROWS, MOD_SIDE_TN), lambda e, j: (0, blk(e, j))),
                      pl.BlockSpec((w_cast.shape[0], MOD_SIDE_TN), lambda e, j: (0, cast_blk(e, j)))]
        args += [c_pad, w_mod, b_mod.reshape(DEPTH, 1, 6 * D), w_cast]
    out = pl.pallas_call(
        functools.partial(_down_kernel, n_mod_steps=n_blk),
        out_shape=tuple(out_shape),
        grid=(N_EXP, steps),
        in_specs=in_specs,
        out_specs=tuple(out_specs),
        compiler_params=_cparams(("arbitrary", "arbitrary")),
        name="expert_down",
    )(*args)
    return out if mod_side is not None else out[0]


def _combine_kernel(off_ref, ye_ref, slot_ref, gate_ref, x_ref, g2_ref, lg_ref, lb_ref, o_ref, moe_ref, *, cap):
    b, tile = pl.program_id(0), pl.program_id(1)
    firsts, n_win = _windows(off_ref, b * N_EXP * OFF_STRIDE, N_EXP, tile, cap)
    lane_e = lax.broadcasted_iota(jnp.int32, (LANES, N_EXP * MOE_WIN), 1) // MOE_WIN
    spread = jnp.where(lax.broadcasted_iota(jnp.int32, (LANES, N_EXP * MOE_WIN), 0) == lane_e, 1.0, 0.0).astype(BF16)
    slot = jnp.dot(slot_ref[...].astype(BF16), spread, preferred_element_type=F32)
    gate = jnp.dot(gate_ref[...].astype(BF16), spread, preferred_element_type=F32).astype(BF16)
    siota = lax.broadcasted_iota(jnp.int32, (1, MOE_WIN), 1)

    def window(w):
        starts, targets, rows = [], [], []
        for e in range(N_EXP):
            start = firsts[e] + w * MOE_WIN
            base = pl.multiple_of(jnp.minimum(start, cap - MOE_WIN), 16)
            starts.append(jnp.full((1, MOE_WIN), start, jnp.int32))
            targets.append(base + siota)
            rows.append(ye_ref[pl.ds(e * cap + base, MOE_WIN), :])
        start_l = jnp.concatenate(starts, axis=1).astype(F32)
        target_l = jnp.concatenate(targets, axis=1).astype(F32)
        comb = jnp.where((slot == target_l) & (slot >= start_l), gate, jnp.zeros_like(gate))
        return jnp.dot(comb, jnp.concatenate(rows, axis=0), preferred_element_type=F32)

    moe_ref[...] = window(0)

    @pl.when(n_win > 1)
    def _():
        def more(w, carry):
            moe_ref[...] += window(w)
            return carry
        lax.fori_loop(1, n_win, more, 0)

    o_ref[...] = _ln(ALPHA * x_ref[...] + g2_ref[...] * moe_ref[...]) * lg_ref[...] + lb_ref[...]


def _combine(off, ye, slot_col, gate_col, x1, mod5, layer, ln_g, ln_b, *, batch, n_seq, cap):
    tm = MOE_TILE
    tiles = n_seq // tm
    vec_spec = pl.BlockSpec((1, D), lambda b, i, off: (0, 0))
    row_spec = lambda w: pl.BlockSpec((tm, w), lambda b, i, off: (b * tiles + i, 0))
    return pl.pallas_call(
        functools.partial(_combine_kernel, cap=cap),
        out_shape=jax.ShapeDtypeStruct((batch * n_seq, D), F32),
        grid_spec=pltpu.PrefetchScalarGridSpec(
            num_scalar_prefetch=1,
            grid=(batch, tiles),
            in_specs=[
                pl.BlockSpec((None, N_EXP * cap, D), lambda b, i, off: (b, 0, 0)),
                row_spec(LANES), row_spec(LANES), row_spec(D),
                pl.BlockSpec((None, None, None, 1, D), lambda b, i, off: (layer, b, 5, 0, 0)),
                vec_spec, vec_spec,
            ],
            out_specs=row_spec(D),
            scratch_shapes=[pltpu.VMEM((tm, D), F32)],
        ),
        compiler_params=_cparams(("arbitrary", "arbitrary")),
        name="ec_combine_ln",
    )(off, ye, slot_col, gate_col, x1, mod5, ln_g.reshape(1, D), ln_b.reshape(1, D))


def _moe_block(x1, h2, logits, mod5, w_gate, w_up, w_down, layer, ln_g, ln_b, *, batch, n_seq, mod_side=None):
    cap = 2 * n_seq // N_EXP
    assert n_seq % MOE_TILE == 0 and n_seq // MOE_TILE < OFF_STRIDE and cap % MOE_WIN == 0
    assert cap <= 256
    slot_row, slot_col, gate_col, off = _route(logits, batch=batch, n_seq=n_seq, cap=cap)
    off = off[:, :, :OFF_STRIDE].reshape(batch * N_EXP * OFF_STRIDE)
    xs = _dispatch(off, slot_row, h2, batch=batch, n_seq=n_seq, cap=cap)
    ye = _experts(xs, w_gate, w_up, w_down, layer, batch=batch, cap=cap, mod_side=mod_side)
    side_out = ()
    if mod_side is not None:
        ye, *side_out = ye
    out = _combine(off, ye, slot_col, gate_col, x1, mod5, 0, ln_g, ln_b, batch=batch, n_seq=n_seq, cap=cap)
    return (out, *side_out)


def _cd_in_kernel(x_ref, sh_ref, sc_ref, w_ref, lg_ref, lb_ref, u_ref, vg_ref, xg_ref):
    h = (_ln(x_ref[...]) * (1.0 + sc_ref[...]) + sh_ref[...]).astype(BF16)
    chunk = 512
    for c0 in range(0, SG_W, chunk):
        u_ref[:, c0:c0 + chunk] = jax.nn.gelu(
            jnp.dot(h, w_ref[:, c0:c0 + chunk], preferred_element_type=F32)).astype(BF16)
    for c0 in range(0, SG_W, chunk):
        z = jax.nn.gelu(jnp.dot(h, w_ref[:, SG_W + c0:SG_W + c0 + chunk], preferred_element_type=F32))
        parts = [_ln(z[:, j * HEAD:(j + 1) * HEAD]) for j in range(chunk // HEAD)]
        vg = jnp.concatenate(parts, axis=1) * lg_ref[:, c0:c0 + chunk] + lb_ref[:, c0:c0 + chunk]
        vg_ref[:, c0:c0 + chunk] = vg.astype(BF16)
    for c0 in range(0, CONV_CH, chunk):
        a = jnp.dot(h, w_ref[:, 2 * SG_W + c0:2 * SG_W + c0 + chunk], preferred_element_type=F32)
        gt = jnp.dot(h, w_ref[:, 2 * SG_W + CONV_CH + c0:2 * SG_W + CONV_CH + c0 + chunk],
                     preferred_element_type=F32)
        xg_ref[:, c0:c0 + chunk] = a * jax.nn.sigmoid(gt)


def _cd_in_proj(x2d, mod5, layer, w_bf, sg_ln_g, sg_ln_b, *, n_seq):
    rows = x2d.shape[0]
    tm = 256
    tiles_per_seq = n_seq // tm
    mod_spec = lambda chunk: pl.BlockSpec((None, None, None, 1, D),
                                          lambda i: (layer, i // tiles_per_seq, chunk, 0, 0))
    vec_spec = pl.BlockSpec((1, SG_W), lambda i: (0, 0))
    row_spec = lambda w: pl.BlockSpec((tm, w), lambda i: (i, 0))
    return pl.pallas_call(
        _cd_in_kernel,
        out_shape=(jax.ShapeDtypeStruct((rows, SG_W), BF16),
                   jax.ShapeDtypeStruct((rows, SG_W), BF16),
                   jax.ShapeDtypeStruct((rows, CONV_CH), F32)),
        grid=(rows // tm,),
        in_specs=[row_spec(D), mod_spec(0), mod_spec(1),
                  pl.BlockSpec((D, CD_IN), lambda i: (0, 0)), vec_spec, vec_spec],
        out_specs=(row_spec(SG_W), row_spec(SG_W), row_spec(CONV_CH)),
        compiler_params=_cparams(("arbitrary",)),
        name="cd_in_proj",
    )(x2d, mod5, mod5, w_bf, sg_ln_g.reshape(1, SG_W), sg_ln_b.reshape(1, SG_W))


CONV_HALO = 16
CONV_ROWS = 32


def _cd_mix_kernel(u_ref, vg_ref, xp_ref, xc_ref, xn_ref, sgw_ref, sgb_ref, cw_ref, cb_ref, lg_ref, lb_ref,
                   o_ref, sh_ref, cv_ref, *, tiles_per_seq):
    i = pl.program_id(1)
    tm = xc_ref.shape[0]
    for c in range(tm // CHUNK):
        for g in range(SG_G):
            rs, cs = slice(c * CHUNK, (c + 1) * CHUNK), slice(g * HEAD, (g + 1) * HEAD)
            sp = jnp.dot(sgw_ref[g], vg_ref[rs, cs], preferred_element_type=F32) + sgb_ref[g]
            o_ref[rs, cs] = (u_ref[rs, cs].astype(F32) * sp).astype(BF16)
    sh_ref[0, 0:CONV_HALO, :] = jnp.where(i > 0, xp_ref[...], 0.0)
    sh_ref[0, CONV_HALO:CONV_HALO + tm, :] = xc_ref[...]
    sh_ref[0, CONV_HALO + tm:, :] = jnp.where(i < tiles_per_seq - 1, xn_ref[...], 0.0)
    keep = tm + 2 * CONV_HALO - SUBLANES
    for s in range(1, SUBLANES):
        sh_ref[s, 0:keep, :] = sh_ref[0, s:s + keep, :]
    off = CONV_HALO - CONV_K // 2
    for r0 in range(0, tm, CONV_ROWS):
        for c0 in range(0, CONV_CH, LANES):
            acc = jnp.zeros((CONV_ROWS, LANES), F32) + cb_ref[:, c0:c0 + LANES]
            for k in range(CONV_K):
                a = r0 + off + k
                acc = acc + (sh_ref[a % SUBLANES, a - a % SUBLANES:a - a % SUBLANES + CONV_ROWS, c0:c0 + LANES]
                             * cw_ref[k:k + 1, c0:c0 + LANES])
            cv_ref[r0:r0 + CONV_ROWS, c0:c0 + LANES] = acc
    y = _ln(cv_ref[...]) * lg_ref[...] + lb_ref[...]
    o_ref[:, SG_W:] = jax.nn.silu(y).astype(BF16)


def _cd_mix(u, vg, xg, sg_w, sg_b, conv_w, conv_b, conv_ln_g, conv_ln_b, *, batch, n_seq):
    tm = 256
    tiles = n_seq // tm
    hpt = tm // CONV_HALO
    n_halo = batch * n_seq // CONV_HALO
    row_spec = lambda w: pl.BlockSpec((tm, w), lambda b, i: (b * tiles + i, 0))
    vec_spec = pl.BlockSpec((1, CONV_CH), lambda b, i: (0, 0))
    sgb_full = jnp.broadcast_to(sg_b[:, :, None], (SG_G, CHUNK, HEAD))
    return pl.pallas_call(
        functools.partial(_cd_mix_kernel, tiles_per_seq=tiles),
        out_shape=jax.ShapeDtypeStruct((batch * n_seq, D), BF16),
        grid=(batch, tiles),
        in_specs=[
            row_spec(SG_W), row_spec(SG_W),
            pl.BlockSpec((CONV_HALO, CONV_CH), lambda b, i: (jnp.maximum((b * tiles + i) * hpt - 1, 0), 0)),
            row_spec(CONV_CH),
            pl.BlockSpec((CONV_HALO, CONV_CH), lambda b, i: (jnp.minimum((b * tiles + i + 1) * hpt, n_halo - 1), 0)),
            pl.BlockSpec((SG_G, CHUNK, CHUNK), lambda b, i: (0, 0, 0)),
            pl.BlockSpec((SG_G, CHUNK, HEAD), lambda b, i: (0, 0, 0)),
            pl.BlockSpec((CONV_K + 1, CONV_CH), lambda b, i: (0, 0)),
            vec_spec, vec_spec, vec_spec,
        ],
        out_specs=pl.BlockSpec((tm, D), lambda b, i: (b * tiles + i, 0)),
        scratch_shapes=[pltpu.VMEM((SUBLANES, tm + 2 * CONV_HALO, CONV_CH), F32), pltpu.VMEM((tm, CONV_CH), F32)],
        compiler_params=_cparams(("arbitrary", "arbitrary")),
        name="cd_mix",
    )(u, vg, xg, xg, xg, sg_w.astype(BF16), sgb_full,
      jnp.pad(conv_w.reshape(CONV_K, CONV_CH), ((0, 1), (0, 0))),
      conv_b.reshape(1, CONV_CH), conv_ln_g.reshape(1, CONV_CH), conv_ln_b.reshape(1, CONV_CH))


def kernel(x, c, ctx, c_ctx, w_mod, b_mod, ln1_g, ln1_b, ln2_g, ln2_b, w_router, w_gate, w_up, w_down,
           ab_w_in, ab_w_out, sink, cd_w_in, cd_w_out, sg_ln_g, sg_ln_b, sg_w, sg_b,
           conv_w, conv_b, conv_ln_g, conv_ln_b):
    batch, n_seq, _ = x.shape
    n_ctx = ctx.shape[1]
    assert x.shape[2] == D and batch + 1 <= MOD_ROWS

    c_pad = jnp.concatenate([c, c_ctx[None, :], jnp.zeros((MOD_ROWS - batch - 1, D), F32)], axis=0)
    mod5 = _modulation(c_pad, w_mod, b_mod, 0).reshape(1, MOD_ROWS, 6, 1, D)
    w_router_pad = jnp.pad(w_router, ((0, 0), (0, 0), (0, LANES - N_EXP))).astype(BF16)

    x0 = x.reshape(batch * n_seq, D)
    ctx2d = ctx.reshape(batch * n_ctx, D)

    w_in = ab_w_in[0]
    tabs = _rope_tables(n_seq)
    p_lat = _ab_in_proj(x0, mod5, 0, lambda b: b, w_in, tabs, n_seq=n_seq,
                        rope_cols=K_END, q_cols=Q_END, name="ab_in_proj")
    p_ctx = _ab_in_proj(ctx2d, mod5, 0, lambda b: batch, w_in[:, Q_END:V_END], tabs, n_seq=n_ctx,
                        rope_cols=0, q_cols=0, name="ab_ctx_proj")
    attn = _attention(p_lat, p_ctx, sink[0], batch=batch, n_seq=n_seq, n_ctx=n_ctx)
    four = _fourier(p_lat, batch=batch, n_seq=n_seq)
    x1, h2, logits = _out_proj([attn, four], ab_w_out[0], x0, mod5, 0, ln1_g[0], ln1_b[0],
                               w_router_pad[0], n_seq=n_seq, folded_last=True)
    x2, mod_l1, cd_w_bf = _moe_block(x1, h2, logits, mod5, w_gate, w_up, w_down, 0, ln2_g[0], ln2_b[0],
                                     batch=batch, n_seq=n_seq, mod_side=(c_pad, w_mod, b_mod, 1, cd_w_in[0]))
    mod5 = mod_l1.reshape(1, MOD_ROWS, 6, 1, D)

    u, vg, xg = _cd_in_proj(x2, mod5, 0, cd_w_bf, sg_ln_g[0], sg_ln_b[0], n_seq=n_seq)
    mix = _cd_mix(u, vg, xg, sg_w[0], sg_b[0], conv_w[0], conv_b[0], conv_ln_g[0], conv_ln_b[0],
                  batch=batch, n_seq=n_seq)
    x3, h4, logits1 = _out_proj([mix], cd_w_out[0], x2, mod5, 0, ln1_g[1], ln1_b[1],
                                w_router_pad[1], n_seq=n_seq)
    x4, = _moe_block(x3, h4, logits1, mod5, w_gate, w_up, w_down, 1, ln2_g[1], ln2_b[1], batch=batch, n_seq=n_seq)
    return x4.reshape(batch, n_seq, D)
```

```python
import functools

import jax
import jax.numpy as jnp
from jax import lax
from jax.experimental import pallas as pl
from jax.experimental.pallas import tpu as pltpu

F32 = jnp.float32
BF16 = jnp.bfloat16

D = 2048
HEAD = 128
N_Q = 12
N_KV = 4
GQA = 3
WINDOW = 128
GRID_W = 64
ROPE_BASE = 10000.0
Q_END = N_Q * HEAD
KV_W = N_KV * HEAD
K_END = Q_END + KV_W
V_END = K_END + KV_W
FOUR_G = 4
FOUR_W = FOUR_G * HEAD
AB_IN = V_END + FOUR_W
SG_G = 8
SG_W = SG_G * HEAD
CHUNK = 128
CONV_CH = 1024
CONV_K = 31
CD_IN = 4096
N_EXP = 16
EXP_FF = 2048
DEPTH = 2
ALPHA = (2 * DEPTH) ** 0.25
LN_EPS = 1e-6
ATTN_SCALE = HEAD ** -0.5
MOD_ROWS = 16
LANES = 128
SUBLANES = 8
VMEM_LIMIT = 56 * 1024 * 1024
MOE_TILE = 256
MOE_WIN = 64
OFF_STRIDE = 16
TN_UP = 512
TN_DOWN = 512


def _cparams(sem):
    return pltpu.CompilerParams(dimension_semantics=sem, vmem_limit_bytes=VMEM_LIMIT)


def _ln(x):
    mu = jnp.mean(x, axis=-1, keepdims=True)
    xc = x - mu
    var = jnp.mean(xc * xc, axis=-1, keepdims=True)
    return xc * lax.rsqrt(var + LN_EPS)


def _mod_kernel(c_ref, w_ref, b_ref, o_ref):
    s = jax.nn.silu(c_ref[...]).astype(BF16)
    o_ref[...] = jnp.dot(s, w_ref[...].astype(BF16), preferred_element_type=F32) + b_ref[...]


def _modulation(c_pad, w_mod, b_mod, layer):
    tn = 1024
    return pl.pallas_call(
        _mod_kernel,
        out_shape=jax.ShapeDtypeStruct((MOD_ROWS, 6 * D), F32),
        grid=(6 * D // tn,),
        in_specs=[
            pl.BlockSpec((MOD_ROWS, D), lambda j: (0, 0)),
            pl.BlockSpec((None, D, tn), lambda j: (layer, 0, j)),
            pl.BlockSpec((None, 1, tn), lambda j: (layer, 0, j)),
        ],
        out_specs=pl.BlockSpec((MOD_ROWS, tn), lambda j: (0, j)),
        compiler_params=_cparams(("arbitrary",)),
        name="modulation",
    )(c_pad, w_mod, b_mod.reshape(DEPTH, 1, 6 * D))


def _rope(a, cos, sa, sb):
    return a * cos + pltpu.roll(a, HEAD - 32, 1) * sa + pltpu.roll(a, 32, 1) * sb


def _ab_in_kernel(x_ref, sh_ref, sc_ref, wf_ref, cos_ref, sa_ref, sb_ref, o_ref, w_ref, *, n_cols, rope_cols, q_cols):
    @pl.when(pl.program_id(0) == 0)
    def _():
        for r0 in range(0, D, 256):
            w_ref[r0:r0 + 256, :] = wf_ref[r0:r0 + 256, :].astype(BF16)

    h = (_ln(x_ref[...]) * (1.0 + sc_ref[...]) + sh_ref[...]).astype(BF16)
    chunk = 512
    for c0 in range(0, n_cols, chunk):
        acc = jnp.dot(h, w_ref[:, c0:c0 + chunk], preferred_element_type=F32)
        if c0 < rope_cols:
            cos, sa, sb = cos_ref[...], sa_ref[...], sb_ref[...]
            parts = []
            for j in range(chunk // HEAD):
                r = _rope(acc[:, j * HEAD:(j + 1) * HEAD], cos, sa, sb)
                parts.append(r * ATTN_SCALE if c0 < q_cols else r)
            acc = jnp.concatenate(parts, axis=1)
        o_ref[:, c0:c0 + chunk] = acc.astype(BF16)


def _ab_in_proj(x2d, mod5, layer, row_of_tile, w, rope_tabs, *, n_seq, rope_cols, q_cols, name, fuse_inputs=None):
    rows, _ = x2d.shape
    n_cols = w.shape[1]
    tm = 256
    tiles_per_seq = n_seq // tm
    cos, sa, sb = rope_tabs
    kern = functools.partial(_ab_in_kernel, n_cols=n_cols, rope_cols=rope_cols, q_cols=q_cols)
    mod_spec = lambda chunk: pl.BlockSpec((None, None, None, 1, D),
                                          lambda i: (layer, row_of_tile(i // tiles_per_seq), chunk, 0, 0))
    tab_spec = pl.BlockSpec((tm, HEAD), lambda i: (i % tiles_per_seq, 0))
    return pl.pallas_call(
        kern,
        out_shape=jax.ShapeDtypeStruct((rows, n_cols), BF16),
        grid=(rows // tm,),
        in_specs=[
            pl.BlockSpec((tm, D), lambda i: (i, 0)),
            mod_spec(0), mod_spec(1),
            pl.BlockSpec((D, n_cols), lambda i: (0, 0), pipeline_mode=pl.Buffered(1)),
            tab_spec, tab_spec, tab_spec,
        ],
        out_specs=pl.BlockSpec((tm, n_cols), lambda i: (i, 0)),
        scratch_shapes=[pltpu.VMEM((D, n_cols), BF16)],
        compiler_params=pltpu.CompilerParams(dimension_semantics=("arbitrary",), vmem_limit_bytes=VMEM_LIMIT,
                                             allow_input_fusion=fuse_inputs),
        name=name,
    )(x2d, mod5, mod5, w, cos, sa, sb)


def _rope_tables(n):
    quarter = HEAD // 4
    t = jnp.arange(n)
    r = (t // GRID_W).astype(F32)
    col = (t % GRID_W).astype(F32)
    inv = ROPE_BASE ** (-jnp.arange(quarter, dtype=F32) / quarter)
    ang_r, ang_c = r[:, None] * inv, col[:, None] * inv
    zero = jnp.zeros_like(ang_r)
    cos = jnp.concatenate([jnp.cos(ang_r)] * 2 + [jnp.cos(ang_c)] * 2, axis=1)
    sa = jnp.concatenate([-jnp.sin(ang_r), zero, -jnp.sin(ang_c), zero], axis=1)
    sb = jnp.concatenate([zero, jnp.sin(ang_r), zero, jnp.sin(ang_c)], axis=1)
    return cos, sa, sb


def _attn_kernel(sink_ref, q_ref, k_ref, v_ref, kc_ref, vc_ref, o_ref, bias_ref, *, n_seq):
    hk = pl.program_id(1)
    kc = kc_ref[...]
    vc = vc_ref[...]
    kw_len = 3 * WINDOW
    rows = GQA * WINDOW
    n_blocks = n_seq // WINDOW
    row = lax.broadcasted_iota(jnp.int32, (rows, 1), 0)
    sink = jnp.where(row < WINDOW, sink_ref[hk * GQA],
                     jnp.where(row < 2 * WINDOW, sink_ref[hk * GQA + 1], sink_ref[hk * GQA + 2]))
    rel = (row & (WINDOW - 1)) - lax.broadcasted_iota(jnp.int32, (1, kw_len), 1)
    for i in range(3):
        bias_ref[i] = jnp.where(jnp.abs(rel + i * WINDOW) <= WINDOW, 0.0, -1e30)
    dn = (((1,), (1,)), ((), ()))

    def block(n, back):
        r0 = n * WINDOW if isinstance(n, int) else pl.multiple_of(n * WINDOW, WINDOW)
        start = r0 - back * WINDOW
        if not isinstance(start, int):
            start = pl.multiple_of(start, WINDOW)
        qs = q_ref[pl.ds(r0, WINDOW), :]
        q3 = jnp.concatenate([qs[:, g * HEAD:(g + 1) * HEAD] for g in range(GQA)], axis=0)
        kw = k_ref[pl.ds(start, kw_len), :]
        vw = v_ref[pl.ds(start, kw_len), :]
        s_w = lax.dot_general(q3, kw, dn, preferred_element_type=F32) + bias_ref[back]
        s_c = lax.dot_general(q3, kc, dn, preferred_element_type=F32)
        m = jnp.maximum(jnp.maximum(jnp.max(s_w, axis=1, keepdims=True), jnp.max(s_c, axis=1, keepdims=True)), sink)
        p_w = jnp.exp(s_w - m)
        p_c = jnp.exp(s_c - m)
        den = jnp.sum(p_w, axis=1, keepdims=True) + jnp.sum(p_c, axis=1, keepdims=True) + jnp.exp(sink - m)
        o = (jnp.dot(p_w.astype(BF16), vw, preferred_element_type=F32)
             + jnp.dot(p_c.astype(BF16), vc, preferred_element_type=F32)) / den
        o_ref[pl.ds(r0, WINDOW), :] = jnp.concatenate(
            [o[g * WINDOW:(g + 1) * WINDOW, :] for g in range(GQA)], axis=1).astype(BF16)

    block(0, 0)
    block(n_blocks - 1, 2)

    per_iter = max(g for g in (7, 2, 1) if (n_blocks - 2) % g == 0)

    def interior(i, carry):
        for g in range(per_iter):
            block(1 + per_iter * i + g, 1)
        return carry

    lax.fori_loop(0, (n_blocks - 2) // per_iter, interior, 0)


def _attention(p_lat, p_ctx, sink, *, batch, n_seq, n_ctx):
    qw = GQA * HEAD
    assert n_seq % WINDOW == 0 and n_seq >= 3 * WINDOW
    return pl.pallas_call(
        functools.partial(_attn_kernel, n_seq=n_seq),
        out_shape=jax.ShapeDtypeStruct((batch * n_seq, Q_END), BF16),
        grid=(batch, N_KV),
        in_specs=[
            pl.BlockSpec(memory_space=pltpu.SMEM),
            pl.BlockSpec((n_seq, qw), lambda b, h: (b, h)),
            pl.BlockSpec((n_seq, HEAD), lambda b, h: (b, Q_END // HEAD + h)),
            pl.BlockSpec((n_seq, HEAD), lambda b, h: (b, K_END // HEAD + h)),
            pl.BlockSpec((n_ctx, HEAD), lambda b, h: (b, h)),
            pl.BlockSpec((n_ctx, HEAD), lambda b, h: (b, N_KV + h)),
        ],
        out_specs=pl.BlockSpec((n_seq, qw), lambda b, h: (b, h)),
        scratch_shapes=[pltpu.VMEM((3, GQA * WINDOW, 3 * WINDOW), F32)],
        compiler_params=_cparams(("arbitrary", "arbitrary")),
        name="banded_gqa",
    )(sink, p_lat, p_lat, p_lat, p_ctx, p_ctx)


DFT_ROWS = 32


def _fourier_kernel(z_ref, cs_ref, ca_ref, sa_ref, cb_ref, sb_ref, o_ref, cn_ref, sn_ref):
    n = z_ref.shape[0]
    half = n // 2

    @pl.when(pl.program_id(0) == 0)
    def _():
        cb, sb = cb_ref[...], sb_ref[...]

        def rows(a, carry):
            ca, sa = ca_ref[a], sa_ref[a]
            r = pl.ds(pl.multiple_of(a * DFT_ROWS, DFT_ROWS), DFT_ROWS)
            cn_ref[r, :] = (ca * cb - sa * sb).astype(BF16)
            sn_ref[r, :] = (sa * cb + ca * sb).astype(BF16)
            return carry

        lax.fori_loop(0, ca_ref.shape[0], rows, 0)

    cs = cs_ref[...]
    a_parts, b_parts = [], []
    for g in range(FOUR_G):
        ab = jnp.dot(z_ref[:, g * HEAD:(g + 1) * HEAD], cs, preferred_element_type=F32)
        a_parts.append(ab[:, :HEAD])
        b_parts.append(ab[:, HEAD:])
    a = jnp.concatenate(a_parts, axis=1).astype(BF16)
    b = jnp.concatenate(b_parts, axis=1).astype(BF16)
    p = jnp.dot(cn_ref[...], a, preferred_element_type=F32)
    q = jnp.dot(sn_ref[...], b, preferred_element_type=F32)
    scale = 1.0 / (n * HEAD) ** 0.5
    o_ref[0:half, :] = ((p[0:half, :] - q[0:half, :]) * scale).astype(BF16)
    o_ref[half:n, :] = ((p[1:half + 1, :] + q[1:half + 1, :]) * scale).astype(BF16)


def _dft_rows(j, m):
    ang = ((j[:, None] * jnp.arange(m, dtype=jnp.int32)[None, :]) % m).astype(F32) * (2.0 * jnp.pi / m)
    return jnp.cos(ang), jnp.sin(ang)


def _fourier(p_lat, *, batch, n_seq):
    assert n_seq % (2 * DFT_ROWS) == 0
    n_a = n_seq // (2 * DFT_ROWS) + 1
    cc, sc = _dft_rows(jnp.arange(HEAD, dtype=jnp.int32), HEAD)
    cs = jnp.concatenate([cc, sc], axis=1).astype(BF16)
    ca, sa = _dft_rows(jnp.arange(n_a, dtype=jnp.int32) * DFT_ROWS, n_seq)
    cb, sb = _dft_rows(jnp.arange(DFT_ROWS, dtype=jnp.int32), n_seq)
    coarse_spec = pl.BlockSpec((n_a, 1, n_seq), lambda b: (0, 0, 0))
    fine_spec = pl.BlockSpec((DFT_ROWS, n_seq), lambda b: (0, 0))
    return pl.pallas_call(
        _fourier_kernel,
        out_shape=jax.ShapeDtypeStruct((batch * n_seq, FOUR_W), BF16),
        grid=(batch,),
        in_specs=[
            pl.BlockSpec((n_seq, FOUR_W), lambda b: (b, V_END // FOUR_W)),
            pl.BlockSpec((HEAD, 2 * HEAD), lambda b: (0, 0)),
            coarse_spec, coarse_spec, fine_spec, fine_spec,
        ],
        out_specs=pl.BlockSpec((n_seq, FOUR_W), lambda b: (b, 0)),
        scratch_shapes=[pltpu.VMEM((n_a * DFT_ROWS, n_seq), BF16), pltpu.VMEM((n_a * DFT_ROWS, n_seq), BF16)],
        compiler_params=_cparams(("arbitrary",)),
        name="fourier_mix",
    )(p_lat, cs, ca.reshape(n_a, 1, n_seq), sa.reshape(n_a, 1, n_seq), cb, sb)


def _out_proj_kernel(*refs, n_parts, folded):
    a_refs = refs[:n_parts]
    rest = refs[n_parts:]
    if folded:
        flip_ref, rest = rest[0], rest[1:]
    (wf_ref, x_ref, g1_ref, lg_ref, lb_ref, sh_ref, sc_ref, wr_ref, x1_ref, h2_ref, lo_ref, w_ref) = rest

    @pl.when(pl.program_id(0) == 0)
    def _():
        for r0 in range(0, D, 256):
            w_ref[r0:r0 + 256, :] = wf_ref[r0:r0 + 256, :].astype(BF16)

    y = None
    k0 = 0
    for p, a_ref in enumerate(a_refs):
        kk = a_ref.shape[1]
        a = a_ref[...]
        if folded and p == n_parts - 1:
            a = jnp.dot(flip_ref[...], a, preferred_element_type=F32).astype(BF16)
        t = jnp.dot(a, w_ref[k0:k0 + kk, :], preferred_element_type=F32)
        y = t if y is None else y + t
        k0 += kk
    x1 = _ln(ALPHA * x_ref[...] + g1_ref[...] * y) * lg_ref[...] + lb_ref[...]
    x1_ref[...] = x1
    h2 = (_ln(x1) * (1.0 + sc_ref[...]) + sh_ref[...]).astype(BF16)
    h2_ref[...] = h2
    lo_ref[...] = jnp.dot(h2, wr_ref[...], preferred_element_type=F32)


def _out_proj(parts, w, x2d, mod5, layer, ln_g, ln_b, w_router_pad, *, n_seq, folded_last=False):
    rows = x2d.shape[0]
    tm = 256
    tiles_per_seq = n_seq // tm
    half_tiles = tiles_per_seq // 2
    mod_spec = lambda chunk: pl.BlockSpec((None, None, None, 1, D),
                                          lambda i: (layer, i // tiles_per_seq, chunk, 0, 0))
    vec_spec = pl.BlockSpec((1, D), lambda i: (0, 0))
    in_specs = [pl.BlockSpec((tm, p.shape[1]), lambda i: (i, 0)) for p in parts]
    extra = []
    if folded_last:
        assert tiles_per_seq % 2 == 0

        def folded_tile(i):
            li = i % tiles_per_seq
            return i - li + jnp.where(li < half_tiles, li, half_tiles + tiles_per_seq - 1 - li)

        in_specs[-1] = pl.BlockSpec((tm, parts[-1].shape[1]), lambda i: (folded_tile(i), 0))
        in_specs.append(pl.BlockSpec((None, tm, tm), lambda i: ((i % tiles_per_seq) // half_tiles, 0, 0)))
        eye = jnp.eye(tm, dtype=BF16)
        extra = [jnp.stack([eye, eye[::-1]])]
    in_specs += [
        pl.BlockSpec((D, D), lambda i: (0, 0), pipeline_mode=pl.Buffered(1)),
        pl.BlockSpec((tm, D), lambda i: (i, 0)),
        mod_spec(2), vec_spec, vec_spec, mod_spec(3), mod_spec(4),
        pl.BlockSpec((D, LANES), lambda i: (0, 0)),
    ]
    return pl.pallas_call(
        functools.partial(_out_proj_kernel, n_parts=len(parts), folded=folded_last),
        out_shape=(jax.ShapeDtypeStruct((rows, D), F32),
                   jax.ShapeDtypeStruct((rows, D), BF16),
                   jax.ShapeDtypeStruct((rows, LANES), F32)),
        grid=(rows // tm,),
        in_specs=in_specs,
        out_specs=(pl.BlockSpec((tm, D), lambda i: (i, 0)),
                   pl.BlockSpec((tm, D), lambda i: (i, 0)),
                   pl.BlockSpec((tm, LANES), lambda i: (i, 0))),
        scratch_shapes=[pltpu.VMEM((D, D), BF16)],
        compiler_params=_cparams(("arbitrary",)),
        name="out_proj_ln",
    )(*parts, *extra, w, x2d, mod5, ln_g.reshape(1, D), ln_b.reshape(1, D), mod5, mod5, w_router_pad)


def _route_kernel(lo_ref, slot_row_ref, slot_col_ref, gate_col_ref, off_ref, tri_ref, *, n_seq, cap, n_b):
    @pl.when(pl.program_id(0) == 0)
    def _():
        blk = 256
        r = lax.broadcasted_iota(jnp.int32, (blk, n_seq), 0)
        c = lax.broadcasted_iota(jnp.int32, (blk, n_seq), 1)
        for r0 in range(0, n_seq, blk):
            tri_ref[r0:r0 + blk, :] = jnp.where(r + r0 < c, 1.0, 0.0).astype(BF16)

    affs = []
    for b in range(n_b):
        lt = jnp.transpose(lo_ref[b * n_seq:(b + 1) * n_seq, :])[:N_EXP, :]
        ex = jnp.exp(lt - jnp.max(lt, axis=0, keepdims=True))
        affs.append(ex / jnp.sum(ex, axis=0, keepdims=True))
    aff = jnp.concatenate(affs, axis=0)
    n_rows = n_b * N_EXP

    def count_ge(t):
        return jnp.sum(jnp.where(aff >= t, 1.0, 0.0), axis=1, keepdims=True)

    def bisect_log(_, lohi):
        lo, hi = lohi
        mid = 0.5 * (lo + hi)
        ok = count_ge(jnp.exp(mid)) >= cap
        return jnp.where(ok, mid, lo), jnp.where(ok, hi, mid)

    def bisect_lin(_, lohi):
        lo, hi = lohi
        mid = 0.5 * (lo + hi)
        ok = count_ge(mid) >= cap
        return jnp.where(ok, mid, lo), jnp.where(ok, hi, mid)

    lo0 = jnp.full((n_rows, 1), -150.0, F32)
    hi0 = jnp.full((n_rows, 1), 1.0, F32)
    lo_l, hi_l = lax.fori_loop(0, 18, bisect_log, (lo0, hi0))
    thr, above = lax.fori_loop(0, 34, bisect_lin, (jnp.exp(lo_l), jnp.exp(hi_l)))
    gt = aff >= above
    eq = (aff >= thr) & (aff < above)
    need = cap - jnp.sum(jnp.where(gt, 1.0, 0.0), axis=1, keepdims=True)
    tri = tri_ref[...]
    eq_before = jnp.dot(jnp.where(eq, 1.0, 0.0).astype(BF16), tri, preferred_element_type=F32)
    sel = gt | (eq & (eq_before < need))
    sel_bf = jnp.where(sel, 1.0, 0.0).astype(BF16)
    slot = jnp.where(sel, jnp.dot(sel_bf, tri, preferred_element_type=F32), -1.0)
    gate = jnp.where(sel, aff, 0.0)
    tt = lax.broadcasted_iota(jnp.int32, (n_seq, LANES), 0)
    ti = lax.broadcasted_iota(jnp.int32, (n_seq, LANES), 1)
    before = jnp.where(tt < ti * MOE_TILE, 1.0, 0.0).astype(BF16)
    off = jnp.dot(sel_bf, before, preferred_element_type=F32).astype(jnp.int32)
    pad = jnp.full((LANES - N_EXP, n_seq), -1.0, F32)
    for b in range(n_b):
        rows = slice(b * N_EXP, (b + 1) * N_EXP)
        slot_row_ref[b] = slot[rows, :]
        off_ref[b] = off[rows, :]
        slot_col_ref[b * n_seq:(b + 1) * n_seq, :] = jnp.transpose(jnp.concatenate([slot[rows, :], pad], axis=0))
        gate_col_ref[b * n_seq:(b + 1) * n_seq, :] = jnp.transpose(jnp.concatenate([gate[rows, :], pad * 0.0], axis=0))


def _route(logits, *, batch, n_seq, cap):
    n_b = 4 if batch % 4 == 0 else 1
    return pl.pallas_call(
        functools.partial(_route_kernel, n_seq=n_seq, cap=cap, n_b=n_b),
        out_shape=(jax.ShapeDtypeStruct((batch, N_EXP, n_seq), F32),
                   jax.ShapeDtypeStruct((batch * n_seq, LANES), F32),
                   jax.ShapeDtypeStruct((batch * n_seq, LANES), F32),
                   jax.ShapeDtypeStruct((batch, N_EXP, LANES), jnp.int32)),
        grid=(batch // n_b,),
        in_specs=[pl.BlockSpec((n_b * n_seq, LANES), lambda b: (b, 0))],
        out_specs=(pl.BlockSpec((n_b, N_EXP, n_seq), lambda b: (b, 0, 0)),
                   pl.BlockSpec((n_b * n_seq, LANES), lambda b: (b, 0)),
                   pl.BlockSpec((n_b * n_seq, LANES), lambda b: (b, 0)),
                   pl.BlockSpec((n_b, N_EXP, LANES), lambda b: (b, 0, 0))),
        scratch_shapes=[pltpu.VMEM((n_seq, n_seq), BF16)],
        compiler_params=_cparams(("arbitrary",)),
        name="ec_route",
    )(logits)


def _windows(off_ref, row0, n_exp, tile, cap):
    firsts, n_win = [], 0
    for e in range(n_exp):
        lo = off_ref[row0 + e * OFF_STRIDE + tile]
        hi = off_ref[row0 + e * OFF_STRIDE + tile + 1]
        first = (lo >> 4) << 4
        firsts.append(first)
        n_win = jnp.maximum(n_win, (hi - first + MOE_WIN - 1) // MOE_WIN)
    return firsts, n_win


def _dispatch_kernel(off_ref, slot_ref, h_ref, o_ref, *, e_blk, cap):
    b, half = pl.program_id(0), pl.program_id(1)
    o_ref[...] = jnp.zeros(o_ref.shape, o_ref.dtype)
    siota = lax.broadcasted_iota(jnp.int32, (MOE_WIN, 1), 0)
    n_cols = h_ref.shape[1]
    chunk = 512

    def token_tile(tile, carry):
        firsts, n_win = _windows(off_ref, (b * N_EXP + half * e_blk) * OFF_STRIDE, e_blk, tile, cap)
        t0 = pl.multiple_of(tile * MOE_TILE, MOE_TILE)

        def window(w, carry):
            bases, parts = [], []
            for e in range(e_blk):
                start = firsts[e] + w * MOE_WIN
                base = pl.multiple_of(jnp.minimum(start, cap - MOE_WIN), 16)
                srow = slot_ref[e, tile]
                srow = jnp.where(srow >= start.astype(F32), srow, -1.0)
                parts.append(jnp.where(srow == (base + siota).astype(F32), 1.0, 0.0).astype(BF16))
                bases.append(base)
            onehot = jnp.concatenate(parts, axis=0)
            for c0 in range(0, n_cols, chunk):
                res = jnp.dot(onehot, h_ref[pl.ds(t0, MOE_TILE), c0:c0 + chunk],
                              preferred_element_type=F32).astype(BF16)
                for e in range(e_blk):
                    rows = pl.ds(bases[e], MOE_WIN)
                    o_ref[e, rows, c0:c0 + chunk] = (o_ref[e, rows, c0:c0 + chunk]
                                                     + res[e * MOE_WIN:(e + 1) * MOE_WIN, :])
            return carry

        return lax.fori_loop(0, n_win, window, carry)

    lax.fori_loop(0, slot_ref.shape[1], token_tile, 0)


def _dispatch(off, slot_row, h2, *, batch, n_seq, cap):
    e_blk = 8
    tiles = n_seq // MOE_TILE
    return pl.pallas_call(
        functools.partial(_dispatch_kernel, e_blk=e_blk, cap=cap),
        out_shape=jax.ShapeDtypeStruct((N_EXP, batch * cap, D), BF16),
        grid_spec=pltpu.PrefetchScalarGridSpec(
            num_scalar_prefetch=1,
            grid=(batch, N_EXP // e_blk),
            in_specs=[
                pl.BlockSpec((None, e_blk, tiles, 1, MOE_TILE), lambda b, j, off: (b, j, 0, 0, 0)),
                pl.BlockSpec((n_seq, D), lambda b, j, off: (b, 0)),
            ],
            out_specs=pl.BlockSpec((e_blk, cap, D), lambda b, j, off: (j, b, 0)),
        ),
        compiler_params=_cparams(("arbitrary", "arbitrary")),
        name="ec_dispatch",
    )(off, slot_row.reshape(batch, N_EXP, tiles, 1, MOE_TILE), h2)


def _gate_up_kernel(x_ref, wg_ref, wu_ref, o_ref):
    x = x_ref[...]
    part = 256
    for c0 in range(0, o_ref.shape[1], part):
        g = jnp.dot(x, wg_ref[:, c0:c0 + part].astype(BF16), preferred_element_type=F32)
        u = jnp.dot(x, wu_ref[:, c0:c0 + part].astype(BF16), preferred_element_type=F32)
        o_ref[:, c0:c0 + part] = (jax.nn.silu(g) * u).astype(BF16)


def _down_kernel(h_ref, wd_ref, *rest, n_mod_steps=0):
    if len(rest) > 1:
        c_ref, wm_ref, bm_ref, wc_ref, o_ref, mod_ref, wcb_ref = rest
        step = pl.program_id(0) * pl.num_programs(1) + pl.program_id(1)

        @pl.when(step < n_mod_steps)
        def _():
            _mod_kernel(c_ref, wm_ref, bm_ref, mod_ref)

        @pl.when(step >= n_mod_steps)
        def _():
            wcb_ref[...] = wc_ref[...].astype(BF16)
    else:
        o_ref, = rest
    y = jnp.dot(h_ref[...], wd_ref[...].astype(BF16), preferred_element_type=F32).astype(BF16)
    cap = o_ref.shape[1]
    for b in range(o_ref.shape[0]):
        o_ref[b] = y[b * cap:(b + 1) * cap, :]


MOD_SIDE_TN = 256


def _experts(xs, w_gate, w_up, w_down, layer, *, batch, cap, mod_side=None):
    m = batch * cap
    tn, tn_down = TN_UP, TN_DOWN
    hid = pl.pallas_call(
        _gate_up_kernel,
        out_shape=jax.ShapeDtypeStruct((N_EXP, m, EXP_FF), BF16),
        grid=(N_EXP, EXP_FF // tn),
        in_specs=[
            pl.BlockSpec((None, m, D), lambda e, j: (e, 0, 0)),
            pl.BlockSpec((None, None, D, tn), lambda e, j: (layer, e, 0, j)),
            pl.BlockSpec((None, None, D, tn), lambda e, j: (layer, e, 0, j)),
        ],
        out_specs=pl.BlockSpec((None, m, tn), lambda e, j: (e, 0, j)),
        compiler_params=_cparams(("arbitrary", "arbitrary")),
        name="expert_gate_up",
    )(xs, w_gate, w_up)
    steps = D // tn_down
    in_specs = [
        pl.BlockSpec((None, m, EXP_FF), lambda e, j: (e, 0, 0)),
        pl.BlockSpec((None, None, EXP_FF, tn_down), lambda e, j: (layer, e, 0, j)),
    ]
    out_shape = [jax.ShapeDtypeStruct((batch, N_EXP * cap, D), BF16)]
    out_specs = [pl.BlockSpec((batch, cap, tn_down), lambda e, j: (0, e, j))]
    args = [hid, w_down]
    n_blk = 0
    if mod_side is not None:
        c_pad, w_mod, b_mod, l, w_cast = mod_side
        n_blk = 6 * D // MOD_SIDE_TN
        n_cast = w_cast.shape[1] // MOD_SIDE_TN
        assert n_blk + n_cast <= N_EXP * steps and w_cast.shape[1] % MOD_SIDE_TN == 0
        blk = lambda e, j: jnp.minimum(e * steps + j, n_blk - 1)
        cast_blk = lambda e, j: jnp.clip(e * steps + j - n_blk, 0, n_cast - 1)
        in_specs += [
            pl.BlockSpec((MOD_ROWS, D), lambda e, j: (0, 0)),
            pl.BlockSpec((None, D, MOD_SIDE_TN), lambda e, j: (l, 0, blk(e, j))),
            pl.BlockSpec((None, 1, MOD_SIDE_TN), lambda e, j: (l, 0, blk(e, j))),
            pl.BlockSpec((w_cast.shape[0], MOD_SIDE_TN), lambda e, j: (0, cast_blk(e, j))),
        ]
        out_shape += [jax.ShapeDtypeStruct((MOD_ROWS, 6 * D), F32), jax.ShapeDtypeStruct(w_cast.shape, BF16)]
        out_specs += [pl.BlockSpec((MOD_ROWS, MOD_SIDE_TN), lambda e, j: (0, blk(e, j))),
                      pl.BlockSpec((w_cast.shape[0], MOD_SIDE_TN), lambda e, j: (0, cast_blk(e, j)))]
        args += [c_pad, w_mod, b_mod.reshape(DEPTH, 1, 6 * D), w_cast]
    out = pl.pallas_call(
        functools.partial(_down_kernel, n_mod_steps=n_blk),
        out_shape=tuple(out_shape),
        grid=(N_EXP, steps),
        in_specs=in_specs,
        out_specs=tuple(out_specs),
        compiler_params=_cparams(("arbitrary", "arbitrary")),
        name="expert_down",
    )(*args)
    return out if mod_side is not None else out[0]


def _combine_kernel(off_ref, ye_ref, slot_ref, gate_ref, x_ref, g2_ref, lg_ref, lb_ref, o_ref, moe_ref, *, cap):
    b, tile = pl.program_id(0), pl.program_id(1)
    firsts, n_win = _windows(off_ref, b * N_EXP * OFF_STRIDE, N_EXP, tile, cap)
    lane_e = lax.broadcasted_iota(jnp.int32, (LANES, N_EXP * MOE_WIN), 1) // MOE_WIN
    spread = jnp.where(lax.broadcasted_iota(jnp.int32, (LANES, N_EXP * MOE_WIN), 0) == lane_e, 1.0, 0.0).astype(BF16)
    slot = jnp.dot(slot_ref[...].astype(BF16), spread, preferred_element_type=F32)
    gate = jnp.dot(gate_ref[...].astype(BF16), spread, preferred_element_type=F32).astype(BF16)
    siota = lax.broadcasted_iota(jnp.int32, (1, MOE_WIN), 1)

    def window(w):
        starts, targets, rows = [], [], []
        for e in range(N_EXP):
            start = firsts[e] + w * MOE_WIN
            base = pl.multiple_of(jnp.minimum(start, cap - MOE_WIN), 16)
            starts.append(jnp.full((1, MOE_WIN), start, jnp.int32))
            targets.append(base + siota)
            rows.append(ye_ref[pl.ds(e * cap + base, MOE_WIN), :])
        start_l = jnp.concatenate(starts, axis=1).astype(F32)
        target_l = jnp.concatenate(targets, axis=1).astype(F32)
        comb = jnp.where((slot == target_l) & (slot >= start_l), gate, jnp.zeros_like(gate))
        return jnp.dot(comb, jnp.concatenate(rows, axis=0), preferred_element_type=F32)

    moe_ref[...] = window(0)

    @pl.when(n_win > 1)
    def _():
        def more(w, carry):
            moe_ref[...] += window(w)
            return carry
        lax.fori_loop(1, n_win, more, 0)

    o_ref[...] = _ln(ALPHA * x_ref[...] + g2_ref[...] * moe_ref[...]) * lg_ref[...] + lb_ref[...]


def _combine(off, ye, slot_col, gate_col, x1, mod5, layer, ln_g, ln_b, *, batch, n_seq, cap):
    tm = MOE_TILE
    tiles = n_seq // tm
    vec_spec = pl.BlockSpec((1, D), lambda b, i, off: (0, 0))
    row_spec = lambda w: pl.BlockSpec((tm, w), lambda b, i, off: (b * tiles + i, 0))
    return pl.pallas_call(
        functools.partial(_combine_kernel, cap=cap),
        out_shape=jax.ShapeDtypeStruct((batch * n_seq, D), F32),
        grid_spec=pltpu.PrefetchScalarGridSpec(
            num_scalar_prefetch=1,
            grid=(batch, tiles),
            in_specs=[
                pl.BlockSpec((None, N_EXP * cap, D), lambda b, i, off: (b, 0, 0)),
                row_spec(LANES), row_spec(LANES), row_spec(D),
                pl.BlockSpec((None, None, None, 1, D), lambda b, i, off: (layer, b, 5, 0, 0)),
                vec_spec, vec_spec,
            ],
            out_specs=row_spec(D),
            scratch_shapes=[pltpu.VMEM((tm, D), F32)],
        ),
        compiler_params=_cparams(("arbitrary", "arbitrary")),
        name="ec_combine_ln",
    )(off, ye, slot_col, gate_col, x1, mod5, ln_g.reshape(1, D), ln_b.reshape(1, D))


def _moe_block(x1, h2, logits, mod5, w_gate, w_up, w_down, layer, ln_g, ln_b, *, batch, n_seq, mod_side=None):
    cap = 2 * n_seq // N_EXP
    assert n_seq % MOE_TILE == 0 and n_seq // MOE_TILE < OFF_STRIDE and cap % MOE_WIN == 0
    assert cap <= 256
    slot_row, slot_col, gate_col, off = _route(logits, batch=batch, n_seq=n_seq, cap=cap)
    off = off[:, :, :OFF_STRIDE].reshape(batch * N_EXP * OFF_STRIDE)
    xs = _dispatch(off, slot_row, h2, batch=batch, n_seq=n_seq, cap=cap)
    ye = _experts(xs, w_gate, w_up, w_down, layer, batch=batch, cap=cap, mod_side=mod_side)
    side_out = ()
    if mod_side is not None:
        ye, *side_out = ye
    out = _combine(off, ye, slot_col, gate_col, x1, mod5, 0, ln_g, ln_b, batch=batch, n_seq=n_seq, cap=cap)
    return (out, *side_out)


def _cd_in_kernel(x_ref, sh_ref, sc_ref, w_ref, lg_ref, lb_ref, u_ref, vg_ref, xg_ref):
    h = (_ln(x_ref[...]) * (1.0 + sc_ref[...]) + sh_ref[...]).astype(BF16)
    chunk = 512
    for c0 in range(0, SG_W, chunk):
        u_ref[:, c0:c0 + chunk] = jax.nn.gelu(
            jnp.dot(h, w_ref[:, c0:c0 + chunk], preferred_element_type=F32)).astype(BF16)
    for c0 in range(0, SG_W, chunk):
        z = jax.nn.gelu(jnp.dot(h, w_ref[:, SG_W + c0:SG_W + c0 + chunk], preferred_element_type=F32))
        parts = [_ln(z[:, j * HEAD:(j + 1) * HEAD]) for j in range(chunk // HEAD)]
        vg = jnp.concatenate(parts, axis=1) * lg_ref[:, c0:c0 + chunk] + lb_ref[:, c0:c0 + chunk]
        vg_ref[:, c0:c0 + chunk] = vg.astype(BF16)
    for c0 in range(0, CONV_CH, chunk):
        a = jnp.dot(h, w_ref[:, 2 * SG_W + c0:2 * SG_W + c0 + chunk], preferred_element_type=F32)
        gt = jnp.dot(h, w_ref[:, 2 * SG_W + CONV_CH + c0:2 * SG_W + CONV_CH + c0 + chunk],
                     preferred_element_type=F32)
        xg_ref[:, c0:c0 + chunk] = a * jax.nn.sigmoid(gt)


def _cd_in_proj(x2d, mod5, layer, w_bf, sg_ln_g, sg_ln_b, *, n_seq):
    rows = x2d.shape[0]
    tm = 256
    tiles_per_seq = n_seq // tm
    mod_spec = lambda chunk: pl.BlockSpec((None, None, None, 1, D),
                                          lambda i: (layer, i // tiles_per_seq, chunk, 0, 0))
    vec_spec = pl.BlockSpec((1, SG_W), lambda i: (0, 0))
    row_spec = lambda w: pl.BlockSpec((tm, w), lambda i: (i, 0))
    return pl.pallas_call(
        _cd_in_kernel,
        out_shape=(jax.ShapeDtypeStruct((rows, SG_W), BF16),
                   jax.ShapeDtypeStruct((rows, SG_W), BF16),
                   jax.ShapeDtypeStruct((rows, CONV_CH), F32)),
        grid=(rows // tm,),
        in_specs=[row_spec(D), mod_spec(0), mod_spec(1),
                  pl.BlockSpec((D, CD_IN), lambda i: (0, 0)), vec_spec, vec_spec],
        out_specs=(row_spec(SG_W), row_spec(SG_W), row_spec(CONV_CH)),
        compiler_params=_cparams(("arbitrary",)),
        name="cd_in_proj",
    )(x2d, mod5, mod5, w_bf, sg_ln_g.reshape(1, SG_W), sg_ln_b.reshape(1, SG_W))


CONV_HALO = 16
CONV_ROWS = 32


def _cd_mix_kernel(u_ref, vg_ref, xp_ref, xc_ref, xn_ref, sgw_ref, sgb_ref, cw_ref, cb_ref, lg_ref, lb_ref,
                   o_ref, sh_ref, cv_ref, *, tiles_per_seq):
    i = pl.program_id(1)
    tm = xc_ref.shape[0]
    for c in range(tm // CHUNK):
        for g in range(SG_G):
            rs, cs = slice(c * CHUNK, (c + 1) * CHUNK), slice(g * HEAD, (g + 1) * HEAD)
            sp = jnp.dot(sgw_ref[g], vg_ref[rs, cs], preferred_element_type=F32) + sgb_ref[g]
            o_ref[rs, cs] = (u_ref[rs, cs].astype(F32) * sp).astype(BF16)
    sh_ref[0, 0:CONV_HALO, :] = jnp.where(i > 0, xp_ref[...], 0.0)
    sh_ref[0, CONV_HALO:CONV_HALO + tm, :] = xc_ref[...]
    sh_ref[0, CONV_HALO + tm:, :] = jnp.where(i < tiles_per_seq - 1, xn_ref[...], 0.0)
    keep = tm + 2 * CONV_HALO - SUBLANES
    for s in range(1, SUBLANES):
        sh_ref[s, 0:keep, :] = sh_ref[0, s:s + keep, :]
    off = CONV_HALO - CONV_K // 2
    for r0 in range(0, tm, CONV_ROWS):
        for c0 in range(0, CONV_CH, LANES):
            acc = jnp.zeros((CONV_ROWS, LANES), F32) + cb_ref[:, c0:c0 + LANES]
            for k in range(CONV_K):
                a = r0 + off + k
                acc = acc + (sh_ref[a % SUBLANES, a - a % SUBLANES:a - a % SUBLANES + CONV_ROWS, c0:c0 + LANES]
                             * cw_ref[k:k + 1, c0:c0 + LANES])
            cv_ref[r0:r0 + CONV_ROWS, c0:c0 + LANES] = acc
    y = _ln(cv_ref[...]) * lg_ref[...] + lb_ref[...]
    o_ref[:, SG_W:] = jax.nn.silu(y).astype(BF16)


def _cd_mix(u, vg, xg, sg_w, sg_b, conv_w, conv_b, conv_ln_g, conv_ln_b, *, batch, n_seq):
    tm = 256
    tiles = n_seq // tm
    hpt = tm // CONV_HALO
    n_halo = batch * n_seq // CONV_HALO
    row_spec = lambda w: pl.BlockSpec((tm, w), lambda b, i: (b * tiles + i, 0))
    vec_spec = pl.BlockSpec((1, CONV_CH), lambda b, i: (0, 0))
    sgb_full = jnp.broadcast_to(sg_b[:, :, None], (SG_G, CHUNK, HEAD))
    return pl.pallas_call(
        functools.partial(_cd_mix_kernel, tiles_per_seq=tiles),
        out_shape=jax.ShapeDtypeStruct((batch * n_seq, D), BF16),
        grid=(batch, tiles),
        in_specs=[
            row_spec(SG_W), row_spec(SG_W),
            pl.BlockSpec((CONV_HALO, CONV_CH), lambda b, i: (jnp.maximum((b * tiles + i) * hpt - 1, 0), 0)),
            row_spec(CONV_CH),
            pl.BlockSpec((CONV_HALO, CONV_CH), lambda b, i: (jnp.minimum((b * tiles + i + 1) * hpt, n_halo - 1), 0)),
            pl.BlockSpec((SG_G, CHUNK, CHUNK), lambda b, i: (0, 0, 0)),
            pl.BlockSpec((SG_G, CHUNK, HEAD), lambda b, i: (0, 0, 0)),
            pl.BlockSpec((CONV_K + 1, CONV_CH), lambda b, i: (0, 0)),
            vec_spec, vec_spec, vec_spec,
        ],
        out_specs=pl.BlockSpec((tm, D), lambda b, i: (b * tiles + i, 0)),
        scratch_shapes=[pltpu.VMEM((SUBLANES, tm + 2 * CONV_HALO, CONV_CH), F32), pltpu.VMEM((tm, CONV_CH), F32)],
        compiler_params=_cparams(("arbitrary", "arbitrary")),
        name="cd_mix",
    )(u, vg, xg, xg, xg, sg_w.astype(BF16), sgb_full,
      jnp.pad(conv_w.reshape(CONV_K, CONV_CH), ((0, 1), (0, 0))),
      conv_b.reshape(1, CONV_CH), conv_ln_g.reshape(1, CONV_CH), conv_ln_b.reshape(1, CONV_CH))


def kernel(x, c, ctx, c_ctx, w_mod, b_mod, ln1_g, ln1_b, ln2_g, ln2_b, w_router, w_gate, w_up, w_down,
           ab_w_in, ab_w_out, sink, cd_w_in, cd_w_out, sg_ln_g, sg_ln_b, sg_w, sg_b,
           conv_w, conv_b, conv_ln_g, conv_ln_b):
    batch, n_seq, _ = x.shape
    n_ctx = ctx.shape[1]
    assert x.shape[2] == D and batch + 1 <= MOD_ROWS

    c_pad = jnp.concatenate([c, c_ctx[None, :], jnp.zeros((MOD_ROWS - batch - 1, D), F32)], axis=0)
    mod5 = _modulation(c_pad, w_mod, b_mod, 0).reshape(1, MOD_ROWS, 6, 1, D)
    w_router_pad = jnp.pad(w_router, ((0, 0), (0, 0), (0, LANES - N_EXP))).astype(BF16)

    x0 = x.reshape(batch * n_seq, D)
    ctx2d = ctx.reshape(batch * n_ctx, D)

    w_in = ab_w_in[0]
    tabs = _rope_tables(n_seq)
    p_lat = _ab_in_proj(x0, mod5, 0, lambda b: b, w_in, tabs, n_seq=n_seq,
                        rope_cols=K_END, q_cols=Q_END, name="ab_in_proj")
    p_ctx = _ab_in_proj(ctx2d, mod5, 0, lambda b: batch, w_in[:, Q_END:V_END], tabs, n_seq=n_ctx,
                        rope_cols=0, q_cols=0, name="ab_ctx_proj",
                        fuse_inputs=[False, False, False, True, False, False, False])
    attn = _attention(p_lat, p_ctx, sink[0], batch=batch, n_seq=n_seq, n_ctx=n_ctx)
    four = _fourier(p_lat, batch=batch, n_seq=n_seq)
    x1, h2, logits = _out_proj([attn, four], ab_w_out[0], x0, mod5, 0, ln1_g[0], ln1_b[0],
                               w_router_pad[0], n_seq=n_seq, folded_last=True)
    x2, mod_l1, cd_w_bf = _moe_block(x1, h2, logits, mod5, w_gate, w_up, w_down, 0, ln2_g[0], ln2_b[0],
                                     batch=batch, n_seq=n_seq, mod_side=(c_pad, w_mod, b_mod, 1, cd_w_in[0]))
    mod5 = mod_l1.reshape(1, MOD_ROWS, 6, 1, D)

    u, vg, xg = _cd_in_proj(x2, mod5, 0, cd_w_bf, sg_ln_g[0], sg_ln_b[0], n_seq=n_seq)
    mix = _cd_mix(u, vg, xg, sg_w[0], sg_b[0], conv_w[0], conv_b[0], conv_ln_g[0], conv_ln_b[0],
                  batch=batch, n_seq=n_seq)
    x3, h4, logits1 = _out_proj([mix], cd_w_out[0], x2, mod5, 0, ln1_g[1], ln1_b[1],
                                w_router_pad[1], n_seq=n_seq)
    x4, = _moe_block(x3, h4, logits1, mod5, w_gate, w_up, w_down, 1, ln2_g[1], ln2_b[1], batch=batch, n_seq=n_seq)
    return x4.reshape(batch, n_seq, D)
```
